```python
import math
import jax, jax.numpy as jnp
from jax import lax
import numpy as np

D_MODEL = 2048
BATCH = 4
SEQ = 2048
DEPTH = 4
DEC_BATCH = 8
DEC_SEQ = 4
PAST_LEN = 16384
PAGE_SIZE = 128

ATTN_WIDTH = D_MODEL // 2
POOL_WIDTH = D_MODEL - ATTN_WIDTH
N_HEADS = 8
HEAD_DIM_V = ATTN_WIDTH // N_HEADS
HEAD_DIM_QK = HEAD_DIM_V // 2
KEY_WIDTH = 2 * HEAD_DIM_QK
IN_WIDTH = 3 * ATTN_WIDTH + POOL_WIDTH
POOL_WINDOWS = (2, 4, 8, 16)
N_POOL_GROUPS = len(POOL_WINDOWS)
POOL_GROUP = POOL_WIDTH // N_POOL_GROUPS
POOL_STATE = max(POOL_WINDOWS) - 1
D_FF = 11 * D_MODEL // 4
N_EXPERTS = 8
TOP_K = 2
D_FF_EXPERT = D_FF // 4
Q_BLOCK = 128
EPS = 1e-6
NEG_INF = -1e30

kernel_name = 'hymba_diffattn_pool_moe_decoder_step'


def rms_norm(x, g):
    xf = x.astype(jnp.float32)
    y = xf * lax.rsqrt(jnp.mean(xf * xf, axis=-1, keepdims=True) + EPS)
    return (y * g.astype(jnp.float32)).astype(x.dtype)


def lambda_init(layer):
    return 0.8 - 0.6 * math.exp(-0.3 * layer)


def diff_attend(q, k, v, q_pos, k_pos, lam):
    logits = jnp.einsum('bqhmd,bkhmd->bhmqk', q, k).astype(jnp.float32) * (HEAD_DIM_QK ** -0.5)
    mask = k_pos[None, :] <= q_pos[:, None]
    logits = jnp.where(mask, logits, NEG_INF)
    p = jax.nn.softmax(logits, axis=-1)
    w = p[:, :, 0] - lam * p[:, :, 1]
    return jnp.einsum('bhqk,bkhd->bqhd', w.astype(v.dtype), v)


def prompt_attention(q, k, v, lam):
    B, T = q.shape[0], q.shape[1]
    nb = T // Q_BLOCK
    qb = q.reshape(B, nb, Q_BLOCK, N_HEADS, 2, HEAD_DIM_QK).transpose(1, 0, 2, 3, 4, 5)
    k_pos = jnp.arange(T)

    def one_block(args):
        q_blk, i = args
        q_pos = i * Q_BLOCK + jnp.arange(Q_BLOCK)
        return diff_attend(q_blk, k, v, q_pos, k_pos, lam)

    out = lax.map(one_block, (qb, jnp.arange(nb)))
    return out.transpose(1, 0, 2, 3, 4).reshape(B, T, N_HEADS, HEAD_DIM_V)


def sample_attention(q, k, v, lam, cache_k_l, cache_v_l, page_table):
    B, T = q.shape[0], q.shape[1]
    past = page_table.shape[1] * PAGE_SIZE
    kp = cache_k_l[page_table].reshape(B, past, N_HEADS, 2, HEAD_DIM_QK).astype(k.dtype)
    vp = cache_v_l[page_table].reshape(B, past, N_HEADS, HEAD_DIM_V).astype(v.dtype)
    k_all = jnp.concatenate([kp, k], axis=1)
    v_all = jnp.concatenate([vp, v], axis=1)
    q_pos = past + jnp.arange(T)
    k_pos = jnp.arange(past + T)
    return diff_attend(q, k_all, v_all, q_pos, k_pos, lam)


def pool_mix(u, past, start, w_pool, pool_scale):
    T = u.shape[1]
    full = jnp.concatenate([past.astype(u.dtype), u], axis=1)
    cs = jnp.cumsum(full.astype(jnp.float32), axis=1)
    cs = jnp.concatenate([jnp.zeros_like(cs[:, :1]), cs], axis=1)
    pos = start + jnp.arange(T)
    pooled = []
    for g, w in enumerate(POOL_WINDOWS):
        lo, hi = g * POOL_GROUP, (g + 1) * POOL_GROUP
        win_sum = (cs[:, POOL_STATE + 1:POOL_STATE + 1 + T, lo:hi]
                   - cs[:, POOL_STATE + 1 - w:POOL_STATE + 1 - w + T, lo:hi])
        count = jnp.minimum(w, pos + 1).astype(jnp.float32)[None, :, None]
        pooled.append(win_sum / count - u[..., lo:hi].astype(jnp.float32))
    pooled = jnp.stack(pooled, axis=2).astype(u.dtype)
    out = jnp.einsum('btgc,gcd->btgd', pooled, w_pool).reshape(u.shape) * pool_scale
    return out, full[:, -POOL_STATE:]


def swiglu(h, wg, wu, wd):
    return jnp.einsum('btf,fd->btd', jax.nn.silu(jnp.einsum('btd,df->btf', h, wg)) * jnp.einsum('btd,df->btf', h, wu), wd)


def moe_ffn(h, w_router, b_router, w_gate, w_up, w_down):
    logits = jnp.einsum('btd,de->bte', h, w_router).astype(jnp.float32) + b_router.astype(jnp.float32)
    top_val, top_idx = lax.top_k(logits, TOP_K)
    gates = jax.nn.softmax(top_val, axis=-1)
    combine = jnp.sum(jax.nn.one_hot(top_idx, N_EXPERTS, dtype=jnp.float32) * gates[..., None], axis=-2)
    out = jnp.zeros(h.shape, jnp.float32)
    for e in range(N_EXPERTS):
        out = out + combine[..., e:e + 1] * swiglu(h, w_gate[e], w_up[e], w_down[e]).astype(jnp.float32)
    return out.astype(h.dtype)


def run_trunk(x, c, start, pool_past, attend, p):
    B, T = x.shape[0], x.shape[1]
    ks, vs, pools = [], [], []
    for l in range(DEPTH):
        li = lambda_init(l)
        mod = jnp.einsum('bd,de->be', jax.nn.silu(c), p['w_ada'][l]) + p['b_ada'][l]
        sh_m, sc_m, gt_m, sh_f, sc_f, gt_f = [m[:, None, :] for m in jnp.split(mod, 6, axis=-1)]
        h = rms_norm(x, p['g_norm_mix'][l]) * (1 + sc_m) + sh_m
        proj = jnp.einsum('btd,de->bte', h, p['w_in'][l])
        q = proj[..., :ATTN_WIDTH].reshape(B, T, N_HEADS, 2, HEAD_DIM_QK)
        k = proj[..., ATTN_WIDTH:2 * ATTN_WIDTH].reshape(B, T, N_HEADS, 2, HEAD_DIM_QK)
        v = proj[..., 2 * ATTN_WIDTH:3 * ATTN_WIDTH].reshape(B, T, N_HEADS, HEAD_DIM_V)
        u = proj[..., 3 * ATTN_WIDTH:]
        q = rms_norm(q, p['g_q'][l])
        k = rms_norm(k, p['g_k'][l])
        lam = (jnp.exp(jnp.sum(p['lambda_q1'][l] * p['lambda_k1'][l]).astype(jnp.float32))
               - jnp.exp(jnp.sum(p['lambda_q2'][l] * p['lambda_k2'][l]).astype(jnp.float32)) + li)
        a = attend(l, q, k, v, lam)
        a = rms_norm(a, p['g_sub'][l]) * (1.0 - li)
        pool_out, pool_new = pool_mix(u, pool_past[l], start, p['w_pool'][l], p['pool_scale'][l])
        mix = jnp.concatenate([a.reshape(B, T, ATTN_WIDTH), pool_out], axis=-1)
        x = x + gt_m * jnp.einsum('btd,de->bte', mix, p['w_o'][l])
        h = rms_norm(x, p['g_norm_ffn'][l]) * (1 + sc_f) + sh_f
        j = l // 2
        if l % 2 == 0:
            f = swiglu(h, p['dense_w_gate'][j], p['dense_w_up'][j], p['dense_w_down'][j])
        else:
            f = moe_ffn(h, p['router_w'][j], p['router_b'][j], p['moe_w_gate'][j], p['moe_w_up'][j], p['moe_w_down'][j])
        x = x + gt_f * f
        ks.append(k.reshape(B, T, N_HEADS, KEY_WIDTH))
        vs.append(v)
        pools.append(pool_new)
    return x, jnp.stack(ks), jnp.stack(vs), jnp.stack(pools)


def setup_inputs(seed: int = 0) -> dict:
    key = jax.random.key(seed)
    keys = iter(jax.random.split(key, 48))

    def nrm(shape, scale=1.0):
        return jax.random.normal(next(keys), shape, jnp.float32) * scale

    n_pages = PAST_LEN // PAGE_SIZE
    n_used = DEC_BATCH * n_pages
    n_pool = n_used + max(1, n_used // 4)
    n_dense = (DEPTH + 1) // 2
    n_moe = DEPTH // 2
    page_table = jax.random.permutation(next(keys), n_pool)[:n_used].reshape(DEC_BATCH, n_pages).astype(jnp.int32)
    d_inv = D_MODEL ** -0.5
    return {
        'x_prompt': nrm((BATCH, SEQ, D_MODEL)),
        'x_sample': nrm((DEC_BATCH, DEC_SEQ, D_MODEL)),
        'cache_k': nrm((DEPTH, n_pool, PAGE_SIZE, N_HEADS, KEY_WIDTH)),
        'cache_v': nrm((DEPTH, n_pool, PAGE_SIZE, N_HEADS, HEAD_DIM_V)),
        'state_pool': nrm((DEPTH, DEC_BATCH, POOL_STATE, POOL_WIDTH)),
        'page_table': page_table,
        'c_prompt': nrm((BATCH, D_MODEL)),
        'c_sample': nrm((DEC_BATCH, D_MODEL)),
        'g_norm_mix': 1.0 + nrm((DEPTH, D_MODEL), 0.1),
        'g_norm_ffn': 1.0 + nrm((DEPTH, D_MODEL), 0.1),
        'w_ada': nrm((DEPTH, D_MODEL, 6 * D_MODEL), 0.5 * d_inv),
        'b_ada': nrm((DEPTH, 6 * D_MODEL), 0.02),
        'w_in': nrm((DEPTH, D_MODEL, IN_WIDTH), d_inv),
        'g_q': 1.0 + nrm((DEPTH, HEAD_DIM_QK), 0.1),
        'g_k': 1.0 + nrm((DEPTH, HEAD_DIM_QK), 0.1),
        'lambda_q1': nrm((DEPTH, HEAD_DIM_QK), 0.1),
        'lambda_k1': nrm((DEPTH, HEAD_DIM_QK), 0.1),
        'lambda_q2': nrm((DEPTH, HEAD_DIM_QK), 0.1),
        'lambda_k2': nrm((DEPTH, HEAD_DIM_QK), 0.1),
        'g_sub': 1.0 + nrm((DEPTH, HEAD_DIM_V), 0.1),
        'w_pool': nrm((DEPTH, N_POOL_GROUPS, POOL_GROUP, POOL_GROUP), POOL_GROUP ** -0.5),
        'pool_scale': 1.0 + nrm((DEPTH, POOL_WIDTH), 0.1),
        'w_o': nrm((DEPTH, D_MODEL, D_MODEL), d_inv),
        'dense_w_gate': nrm((n_dense, D_MODEL, D_FF), d_inv),
        'dense_w_up': nrm((n_dense, D_MODEL, D_FF), d_inv),
        'dense_w_down': nrm((n_dense, D_FF, D_MODEL), D_FF ** -0.5),
        'router_w': nrm((n_moe, D_MODEL, N_EXPERTS), d_inv),
        'router_b': nrm((n_moe, N_EXPERTS), 0.01),
        'moe_w_gate': nrm((n_moe, N_EXPERTS, D_MODEL, D_FF_EXPERT), d_inv),
        'moe_w_up': nrm((n_moe, N_EXPERTS, D_MODEL, D_FF_EXPERT), d_inv),
        'moe_w_down': nrm((n_moe, N_EXPERTS, D_FF_EXPERT, D_MODEL), D_FF_EXPERT ** -0.5),
    }


def reference(x_prompt, x_sample, cache_k, cache_v, state_pool, page_table, c_prompt, c_sample,
              g_norm_mix, g_norm_ffn, w_ada, b_ada, w_in, g_q, g_k,
              lambda_q1, lambda_k1, lambda_q2, lambda_k2, g_sub, w_pool, pool_scale, w_o,
              dense_w_gate, dense_w_up, dense_w_down,
              router_w, router_b, moe_w_gate, moe_w_up, moe_w_down):
    p = {
        'g_norm_mix': g_norm_mix, 'g_norm_ffn': g_norm_ffn, 'w_ada': w_ada, 'b_ada': b_ada,
        'w_in': w_in, 'g_q': g_q, 'g_k': g_k,
        'lambda_q1': lambda_q1, 'lambda_k1': lambda_k1, 'lambda_q2': lambda_q2, 'lambda_k2': lambda_k2,
        'g_sub': g_sub, 'w_pool': w_pool, 'pool_scale': pool_scale, 'w_o': w_o,
        'dense_w_gate': dense_w_gate, 'dense_w_up': dense_w_up, 'dense_w_down': dense_w_down,
        'router_w': router_w, 'router_b': router_b,
        'moe_w_gate': moe_w_gate, 'moe_w_up': moe_w_up, 'moe_w_down': moe_w_down,
    }
    pool_empty = jnp.zeros((DEPTH, x_prompt.shape[0], POOL_STATE, POOL_WIDTH), x_prompt.dtype)
    y_prompt, k_prompt, v_prompt, pool_prompt = run_trunk(
        x_prompt, c_prompt, 0, pool_empty,
        lambda l, q, k, v, lam: prompt_attention(q, k, v, lam), p)
    past_len = page_table.shape[1] * PAGE_SIZE

    def sample_attend(l, q, k, v, lam):
        return sample_attention(q, k, v, lam, cache_k[l], cache_v[l], page_table)

    y_sample, k_sample, v_sample, pool_sample = run_trunk(
        x_sample, c_sample, past_len, state_pool, sample_attend, p)
    return (y_prompt, y_sample, k_prompt, v_prompt, pool_prompt, k_sample, v_sample, pool_sample)
```

```python
import functools
import math

import jax
import jax.numpy as jnp
from jax import lax
from jax.experimental import pallas as pl
from jax.experimental.pallas import tpu as pltpu

F32 = jnp.float32
BF16 = jnp.bfloat16

N_HEADS = 8
HEAD_DIM_V = 128
HEAD_DIM_QK = 64
POOL_WINDOWS = (2, 4, 8, 16)
POOL_STATE = 15
PAGE_SIZE = 128
N_EXPERTS = 8
EPS = 1e-6
NEG_INF = -1e30
LANES = 128
VMEM_LIMIT = 56 * 1024 * 1024
COL_CHUNK = 512


def _params(n_axes, vmem=VMEM_LIMIT):
    return pltpu.CompilerParams(dimension_semantics=("arbitrary",) * n_axes,
                                vmem_limit_bytes=vmem)


def _lambda_init(layer):
    return 0.8 - 0.6 * math.exp(-0.3 * layer)


def _mod_spec(arr, tiles_per_group):
    _, r, d = arr.shape
    return pl.BlockSpec((1, r, d), lambda m, *_: (m // tiles_per_group, 0, 0))


def _norm_mod(x, g, sc, sh):
    y = x * lax.rsqrt(jnp.mean(x * x, axis=-1, keepdims=True) + EPS)
    return (y * g) * (1.0 + sc) + sh


def _ada_kernel(c_ref, w_ref, b_ref, o_ref):
    c = c_ref[...]
    s = (c * jax.nn.sigmoid(c)).astype(BF16)
    o_ref[0] = jnp.dot(s, w_ref[0].astype(BF16), preferred_element_type=F32) + b_ref[0]


def _ada(c_all, w_ada, b_ada):
    depth, d, n = w_ada.shape
    rows = c_all.shape[0]
    tn = 1024
    return pl.pallas_call(
        _ada_kernel,
        grid=(depth, n // tn),
        in_specs=[pl.BlockSpec((rows, d), lambda l, j: (0, 0)),
                  pl.BlockSpec((1, d, tn), lambda l, j: (l, 0, j)),
                  pl.BlockSpec((1, 1, tn), lambda l, j: (l, 0, j))],
        out_specs=pl.BlockSpec((1, rows, tn), lambda l, j: (l, 0, j)),
        out_shape=jax.ShapeDtypeStruct((depth, rows, n), F32),
        compiler_params=_params(2),
        name="ada_mod",
    )(c_all, w_ada, b_ada.reshape(depth, 1, n))


def _norm_kernel(x_ref, g_ref, sc_ref, sh_ref, h_ref):
    h_ref[...] = _norm_mod(x_ref[...], g_ref[0], sc_ref[0], sh_ref[0]).astype(BF16)


def _norm(x, g_all, layer, sc, sh, tm, tiles_per_group):
    m, d = x.shape
    return pl.pallas_call(
        _norm_kernel,
        grid=(m // tm,),
        in_specs=[pl.BlockSpec((tm, d), lambda i: (i, 0)),
                  pl.BlockSpec((1, 1, d), lambda i: (layer, 0, 0)),
                  _mod_spec(sc, tiles_per_group), _mod_spec(sh, tiles_per_group)],
        out_specs=pl.BlockSpec((tm, d), lambda i: (i, 0)),
        out_shape=jax.ShapeDtypeStruct((m, d), BF16),
        compiler_params=_params(1),
        name="norm_mod",
    )(x, g_all, sc, sh)


def _qk_norm(p, g128, scale):
    lo = lax.broadcasted_iota(jnp.int32, (1, LANES), 1) < HEAD_DIM_QK
    sq = p * p
    s0 = jnp.sum(jnp.where(lo, sq, 0.0), axis=-1, keepdims=True)
    s1 = jnp.sum(jnp.where(lo, 0.0, sq), axis=-1, keepdims=True)
    inv0 = lax.rsqrt(s0 * (1.0 / HEAD_DIM_QK) + EPS)
    inv1 = lax.rsqrt(s1 * (1.0 / HEAD_DIM_QK) + EPS)
    y = (p * jnp.where(lo, inv0, inv1)) * g128
    return y if scale is None else y * scale


def _inproj_kernel(*refs, mode):
    normed = mode in ("q", "k")
    stacked = mode in ("k", "v")
    h_ref, w_ref = refs[:2]
    g_ref = refs[2] if normed else None
    wbf_ref = refs[-1]
    outs = refs[-3:-1] if stacked else refs[-2:-1]

    @pl.when(pl.program_id(0) == 0)
    def _():
        wbf_ref[...] = w_ref[0].astype(BF16)

    h = h_ref[...]
    scale = HEAD_DIM_QK ** -0.5 if mode == "q" else None
    for c0 in range(0, wbf_ref.shape[1], COL_CHUNK):
        acc = jnp.dot(h, wbf_ref[:, c0:c0 + COL_CHUNK], preferred_element_type=F32)
        if mode == "u":
            outs[0][:, c0:c0 + COL_CHUNK] = acc
            continue
        for hh in range(COL_CHUNK // LANES):
            head = c0 // LANES + hh
            y = acc[:, hh * LANES:(hh + 1) * LANES]
            if normed:
                y = _qk_norm(y, g_ref[...], scale)
            if stacked:
                outs[0][0, :, head, :] = y
                outs[1][:, head * LANES:(head + 1) * LANES] = y.astype(BF16)
            else:
                outs[0][:, head * LANES:(head + 1) * LANES] = y.astype(BF16)


def _inproj(h, w_in, layer, section, mode, tm, g128=None, stack=None):
    m, d = h.shape
    width = N_HEADS * LANES
    ins = [h, w_in]
    in_specs = [pl.BlockSpec((tm, d), lambda i: (i, 0)),
                pl.BlockSpec((1, d, width), lambda i: (layer, 0, section))]
    if g128 is not None:
        ins.append(g128)
        in_specs.append(pl.BlockSpec((1, LANES), lambda i: (0, 0)))
    flat_spec = pl.BlockSpec((tm, width), lambda i: (i, 0))
    aliases = {}
    if stack is not None:
        aliases = {len(ins): 0}
        ins.append(stack)
        in_specs.append(pl.BlockSpec(memory_space=pl.ANY))
        out_specs = [pl.BlockSpec((1, tm, N_HEADS, LANES), lambda i: (layer, i, 0, 0)), flat_spec]
        out_shape = [jax.ShapeDtypeStruct(stack.shape, stack.dtype),
                     jax.ShapeDtypeStruct((m, width), BF16)]
    else:
        out_specs = [flat_spec]
        out_shape = [jax.ShapeDtypeStruct((m, width), BF16 if mode == "q" else F32)]
    n_in = len(ins)

    def body(*refs):
        if stack is not None:
            refs = refs[:n_in - 1] + refs[n_in:]
        _inproj_kernel(*refs, mode=mode)

    out = pl.pallas_call(
        body,
        grid=(m // tm,),
        in_specs=in_specs,
        out_specs=out_specs,
        out_shape=out_shape,
        scratch_shapes=[pltpu.VMEM((d, width), BF16)],
        input_output_aliases=aliases,
        compiler_params=_params(1),
        name="inproj_" + mode,
    )(*ins)
    return out if stack is not None else out[0]


def _sub_norm(o, g_sub, li):
    y = o * lax.rsqrt(jnp.mean(o * o, axis=-1, keepdims=True) + EPS)
    return (y * g_sub) * (1.0 - li)


def _attn_kernel(lam_ref, q_ref, k_ref, v_ref, g_ref, o_ref, m_ref, l_ref, acc_ref, *, tq, li):
    qi = pl.program_id(2)
    lo = lax.broadcasted_iota(jnp.int32, (1, LANES), 1) < HEAD_DIM_QK
    q = q_ref[0]
    zero = jnp.zeros_like(q)
    qmaps = (jnp.where(lo, q, zero), jnp.where(lo, zero, q))
    m_ref[...] = jnp.full(m_ref.shape, NEG_INF, F32)
    l_ref[...] = jnp.zeros(l_ref.shape, F32)
    acc_ref[...] = jnp.zeros(acc_ref.shape, F32)

    def step(j, masked):
        start = pl.multiple_of(j * tq, tq)
        kt = k_ref[0, pl.ds(start, tq), :]
        vt = v_ref[0, pl.ds(start, tq), :]
        for mp in range(2):
            s = lax.dot_general(qmaps[mp], kt, (((1,), (1,)), ((), ())),
                                preferred_element_type=F32)
            if masked:
                row = lax.broadcasted_iota(jnp.int32, (tq, tq), 0)
                col = lax.broadcasted_iota(jnp.int32, (tq, tq), 1)
                s = jnp.where(col <= row, s, NEG_INF)
            m_old = m_ref[mp]
            m_new = jnp.maximum(m_old, jnp.max(s, axis=-1, keepdims=True))
            alpha = jnp.exp(m_old - m_new)
            p = jnp.exp(s - m_new)
            l_ref[mp] = alpha * l_ref[mp] + jnp.sum(p, axis=-1, keepdims=True)
            acc_ref[mp] = alpha * acc_ref[mp] + jnp.dot(p.astype(BF16), vt,
                                                        preferred_element_type=F32)
            m_ref[mp] = m_new

    def off_diag(j, carry):
        step(j, False)
        return carry

    lax.fori_loop(0, qi, off_diag, 0)
    step(qi, True)

    lam = lam_ref[0]
    o = acc_ref[0] / l_ref[0] - lam * (acc_ref[1] / l_ref[1])
    o_ref[0] = _sub_norm(o, g_ref[...], li).astype(BF16)


def _attn_prompt(q, k, v, layer, lam, g_sub, batch, seq, tq):
    width = N_HEADS * LANES
    q3, k3, v3 = (t.reshape(batch, seq, width) for t in (q, k, v))
    kv_spec = pl.BlockSpec((1, seq, LANES), lambda b, h, i: (b, 0, h))
    out = pl.pallas_call(
        functools.partial(_attn_kernel, tq=tq, li=_lambda_init(layer)),
        grid=(batch, N_HEADS, seq // tq),
        in_specs=[pl.BlockSpec(memory_space=pltpu.SMEM),
                  pl.BlockSpec((1, tq, LANES), lambda b, h, i: (b, i, h)),
                  kv_spec, kv_spec,
                  pl.BlockSpec((1, LANES), lambda b, h, i: (0, 0))],
        out_specs=pl.BlockSpec((1, tq, LANES), lambda b, h, i: (b, i, h)),
        out_shape=jax.ShapeDtypeStruct((batch, seq, width), BF16),
        scratch_shapes=[pltpu.VMEM((2, tq, 1), F32), pltpu.VMEM((2, tq, 1), F32),
                        pltpu.VMEM((2, tq, LANES), F32)],
        compiler_params=_params(3),
        name="attn_prompt",
    )(lam, q3, k3, v3, g_sub)
    return out.reshape(batch * seq, width)


def _decode_kernel(pt_ref, lam_ref, q_ref, *refs, pages, n_new, li):
    k_refs = refs[:pages]
    v_refs = refs[pages:2 * pages]
    kn_ref, vn_ref, g_ref, o_ref, knp_ref, vnp_ref, m_ref, l_ref, acc_ref = refs[2 * pages:]
    b = pl.program_id(0)
    c = pl.program_id(1)
    last = pl.num_programs(1) - 1
    shift_q = (2 * n_new).bit_length() - 1

    @pl.when(jnp.logical_and(b == 0, c == 0))
    def _():
        knp_ref[...] = jnp.zeros(knp_ref.shape, F32)
        vnp_ref[...] = jnp.zeros(vnp_ref.shape, F32)

    @pl.when(c == 0)
    def _():
        m_ref[...] = jnp.full(m_ref.shape, NEG_INF, F32)
        l_ref[...] = jnp.zeros(l_ref.shape, F32)
        acc_ref[...] = jnp.zeros(acc_ref.shape, F32)

    q = q_ref[0]
    n_q = q.shape[0]
    page_rows = PAGE_SIZE * N_HEADS
    def own_head(n_cols):
        row = lax.broadcasted_iota(jnp.int32, (n_q, n_cols), 0)
        col = lax.broadcasted_iota(jnp.int32, (n_q, n_cols), 1)
        return row, col, jnp.bitwise_and(col, N_HEADS - 1) == lax.shift_right_logical(row, shift_q)

    def logits(k2d, keep):
        s = lax.dot_general(q, k2d.astype(BF16), (((1,), (1,)), ((), ())),
                            preferred_element_type=F32)
        return jnp.where(keep, s, NEG_INF)

    def update(s_list, v_list):
        m_old = m_ref[...]
        m_new = m_old
        for s in s_list:
            m_new = jnp.maximum(m_new, jnp.max(s, axis=-1, keepdims=True))
        alpha = jnp.exp(m_old - m_new)
        l_new = alpha * l_ref[...]
        pv = None
        for s, v in zip(s_list, v_list):
            p = jnp.exp(s - m_new)
            l_new = l_new + jnp.sum(p, axis=-1, keepdims=True)
            d = jnp.dot(p.astype(BF16), v, preferred_element_type=F32)
            pv = d if pv is None else pv + d
        acc_ref[...] = alpha * acc_ref[...] + pv
        l_ref[...] = l_new
        m_ref[...] = m_new

    s_list, v_list = [], []
    page_keep = own_head(page_rows)[2]
    for j in range(pages):
        s_list.append(logits(k_refs[j][0, 0].reshape(page_rows, LANES), page_keep))
        v_list.append(v_refs[j][0, 0].reshape(page_rows, LANES).astype(BF16))
    update(s_list, v_list)

    @pl.when(c == last)
    def _():
        n_rows = n_new * N_HEADS
        knp_ref[0:n_rows, :] = kn_ref[0, 0]
        vnp_ref[0:n_rows, :] = vn_ref[0, 0]
        row, col, keep = own_head(knp_ref.shape[0])
        tok = jnp.bitwise_and(row, n_new - 1)
        slot = lax.shift_right_logical(col, N_HEADS.bit_length() - 1)
        keep = jnp.logical_and(keep, slot <= tok)
        update([logits(knp_ref[...], keep)], [vnp_ref[...].astype(BF16)])

        o = acc_ref[...] / l_ref[...]
        o = o - lam_ref[0] * pltpu.roll(o, shift=n_q - n_new, axis=0)
        o_ref[0] = _sub_norm(o, g_ref[...], li)


def _attn_decode(q, k_stack, v_stack, cache_k, cache_v, page_table, layer, lam, g_sub,
                 n_new, pages):
    depth = cache_k.shape[0]
    batch, n_pages = page_table.shape
    assert n_new & (n_new - 1) == 0 and N_HEADS & (N_HEADS - 1) == 0
    n_q = N_HEADS * 2 * n_new
    qh = q.reshape(batch, n_new, N_HEADS, 1, LANES).transpose(0, 2, 3, 1, 4)
    in_map = (jnp.arange(LANES) // HEAD_DIM_QK)[None, :] == jnp.arange(2)[:, None]
    qm = jnp.where(in_map[None, None, :, None, :], qh, jnp.zeros_like(qh)).reshape(batch, n_q, LANES)
    kn = k_stack.reshape(depth, batch, n_new * N_HEADS, LANES)
    vn = v_stack.reshape(depth, batch, n_new * N_HEADS, LANES)

    def page_spec(j):
        return pl.BlockSpec((1, 1, PAGE_SIZE, N_HEADS, LANES),
                            lambda b, c, pt: (layer, pt[b, c * pages + j], 0, 0, 0))

    new_spec = pl.BlockSpec((1, 1, n_new * N_HEADS, LANES), lambda b, c, pt: (layer, b, 0, 0))
    grid_spec = pltpu.PrefetchScalarGridSpec(
        num_scalar_prefetch=1,
        grid=(batch, n_pages // pages),
        in_specs=([pl.BlockSpec(memory_space=pltpu.SMEM),
                   pl.BlockSpec((1, n_q, LANES), lambda b, c, pt: (b, 0, 0))]
                  + [page_spec(j) for j in range(pages)]
                  + [page_spec(j) for j in range(pages)]
                  + [new_spec, new_spec,
                     pl.BlockSpec((1, LANES), lambda b, c, pt: (0, 0))]),
        out_specs=pl.BlockSpec((1, n_q, LANES), lambda b, c, pt: (b, 0, 0)),
        scratch_shapes=[pltpu.VMEM((LANES, LANES), F32),
                        pltpu.VMEM((LANES, LANES), F32),
                        pltpu.VMEM((n_q, 1), F32),
                        pltpu.VMEM((n_q, 1), F32),
                        pltpu.VMEM((n_q, LANES), F32)],
    )
    out = pl.pallas_call(
        functools.partial(_decode_kernel, pages=pages, n_new=n_new, li=_lambda_init(layer)),
        grid_spec=grid_spec,
        out_shape=jax.ShapeDtypeStruct((batch, n_q, LANES), F32),
        compiler_params=_params(2),
        name="attn_decode",
    )(page_table, lam, qm, *([cache_k] * pages), *([cache_v] * pages), kn, vn, g_sub)
    out = out.reshape(batch, N_HEADS, 2, n_new, LANES)[:, :, 0]
    return out.transpose(0, 2, 1, 3).reshape(batch * n_new, N_HEADS * LANES).astype(BF16)


def _pool_kernel(u_ref, past_ref, w_ref, sc_ref, o_ref, full_ref, *, start, chunk):
    halo = past_ref.shape[1]
    t = u_ref.shape[1]
    group = u_ref.shape[2] // len(POOL_WINDOWS)
    full_ref[0:halo, :] = past_ref[0]
    full_ref[halo:halo + t, :] = u_ref[0]
    for c0 in range(0, t, chunk):
        pos = start + c0 + lax.broadcasted_iota(jnp.int32, (chunk, 1), 0)
        for g, w in enumerate(POOL_WINDOWS):
            cols = slice(g * group, (g + 1) * group)
            cur = full_ref[halo + c0:halo + c0 + chunk, cols]
            win = cur
            for i in range(1, w):
                win = win + full_ref[halo + c0 - i:halo + c0 - i + chunk, cols]
            count = jnp.minimum(w, pos + 1).astype(F32)
            pooled = (win / count - cur).astype(BF16)
            y = jnp.dot(pooled, w_ref[0, g].astype(BF16), preferred_element_type=F32)
            o_ref[0, c0:c0 + chunk, cols] = (y * sc_ref[0, :, cols]).astype(BF16)


def _pool(u3, past16, w_pool, pool_scale, layer, start):
    batch, t, c = u3.shape
    depth = w_pool.shape[0]
    group = c // len(POOL_WINDOWS)
    chunk = min(t, 256)
    return pl.pallas_call(
        functools.partial(_pool_kernel, start=start, chunk=chunk),
        grid=(batch,),
        in_specs=[pl.BlockSpec((1, t, c), lambda b: (b, 0, 0)),
                  pl.BlockSpec((1, POOL_STATE + 1, c), lambda b: (b, 0, 0)),
                  pl.BlockSpec((1, len(POOL_WINDOWS), group, group), lambda b: (layer, 0, 0, 0)),
                  pl.BlockSpec((1, 1, c), lambda b: (layer, 0, 0))],
        out_specs=pl.BlockSpec((1, t, c), lambda b: (b, 0, 0)),
        out_shape=jax.ShapeDtypeStruct((batch, t, c), BF16),
        scratch_shapes=[pltpu.VMEM((POOL_STATE + 1 + t, c), F32)],
        compiler_params=_params(1),
        name="pool_mix",
    )(u3, past16, w_pool, pool_scale.reshape(depth, 1, c))


def _outproj_kernel(a_ref, p_ref, x_ref, w_ref, gt_ref, g_ref, sc_ref, sh_ref,
                    x1_ref, h2_ref, wbf_ref):
    @pl.when(pl.program_id(0) == 0)
    def _():
        wbf_ref[...] = w_ref[0].astype(BF16)

    half = a_ref.shape[1]
    a = a_ref[...]
    p = p_ref[...]
    for c0 in range(0, x_ref.shape[1], COL_CHUNK):
        cols = slice(c0, c0 + COL_CHUNK)
        y = jnp.dot(a, wbf_ref[0:half, cols], preferred_element_type=F32)
        y = y + jnp.dot(p, wbf_ref[half:, cols], preferred_element_type=F32)
        x1_ref[:, cols] = x_ref[:, cols] + gt_ref[0, :, cols] * y
    h2_ref[...] = _norm_mod(x1_ref[...], g_ref[0], sc_ref[0], sh_ref[0]).astype(BF16)


def _outproj(a, p, x, w_o, layer, gt, g_ffn, sc, sh, tm, tiles_per_group):
    m, d = x.shape
    half = a.shape[1]
    return pl.pallas_call(
        _outproj_kernel,
        grid=(m // tm,),
        in_specs=[pl.BlockSpec((tm, half), lambda i: (i, 0)),
                  pl.BlockSpec((tm, half), lambda i: (i, 0)),
                  pl.BlockSpec((tm, d), lambda i: (i, 0)),
                  pl.BlockSpec((1, d, d), lambda i: (layer, 0, 0),
                               pipeline_mode=pl.Buffered(1)),
                  _mod_spec(gt, tiles_per_group),
                  pl.BlockSpec((1, 1, d), lambda i: (layer, 0, 0)),
                  _mod_spec(sc, tiles_per_group), _mod_spec(sh, tiles_per_group)],
        out_specs=[pl.BlockSpec((tm, d), lambda i: (i, 0)),
                   pl.BlockSpec((tm, d), lambda i: (i, 0))],
        out_shape=[jax.ShapeDtypeStruct((m, d), F32), jax.ShapeDtypeStruct((m, d), BF16)],
        scratch_shapes=[pltpu.VMEM((d, d), BF16)],
        compiler_params=_params(1),
        name="outproj",
    )(a, p, x, w_o, gt, g_ffn, sc, sh)


def _ffn_kernel(*refs, row_weight):
    if row_weight:
        h_ref, x_ref, gt_ref, wg_ref, wu_ref, wd_ref, cw_ref, o_ref = refs
    else:
        h_ref, x_ref, gt_ref, wg_ref, wu_ref, wd_ref, o_ref = refs

    @pl.when(pl.program_id(1) == 0)
    def _():
        o_ref[...] = x_ref[...]

    h = h_ref[...]
    g = jnp.dot(h, wg_ref[0].astype(BF16), preferred_element_type=F32)
    u = jnp.dot(h, wu_ref[0].astype(BF16), preferred_element_type=F32)
    act = ((g * jax.nn.sigmoid(g)) * u).astype(BF16)
    for c0 in range(0, o_ref.shape[1], COL_CHUNK):
        cols = slice(c0, c0 + COL_CHUNK)
        y = jnp.dot(act, wd_ref[0, :, cols].astype(BF16), preferred_element_type=F32)
        scale = gt_ref[0, :, cols]
        if row_weight:
            scale = scale * cw_ref[...]
        o_ref[:, cols] += scale * y


def _ffn(h, x, gt, wg, wu, wd, widx, tm, tf, tiles_per_group, cw=None):
    m, d = x.shape
    f = wg.shape[2]
    ins = [h, x, gt, wg, wu, wd]
    in_specs = [pl.BlockSpec((tm, d), lambda i, j: (i, 0)),
                pl.BlockSpec((tm, d), lambda i, j: (i, 0), pipeline_mode=pl.Buffered(1)),
                _mod_spec(gt, tiles_per_group),
                pl.BlockSpec((1, d, tf), lambda i, j: (widx, 0, j)),
                pl.BlockSpec((1, d, tf), lambda i, j: (widx, 0, j)),
                pl.BlockSpec((1, tf, d), lambda i, j: (widx, j, 0))]
    if cw is not None:
        ins.append(cw)
        in_specs.append(pl.BlockSpec((tm, 1), lambda i, j: (i, 0)))
    return pl.pallas_call(
        functools.partial(_ffn_kernel, row_weight=cw is not None),
        grid=(m // tm, f // tf),
        in_specs=in_specs,
        out_specs=pl.BlockSpec((tm, d), lambda i, j: (i, 0)),
        out_shape=jax.ShapeDtypeStruct((m, d), F32),
        compiler_params=_params(2),
        name="ffn",
    )(*ins)


def _router_kernel(h_ref, whi_ref, wlo_ref, b_ref, o_ref):
    h = h_ref[...]
    logits = (jnp.dot(h, whi_ref[...], preferred_element_type=F32)
              + jnp.dot(h, wlo_ref[...], preferred_element_type=F32) + b_ref[...])
    lane = lax.broadcasted_iota(jnp.int32, logits.shape, 1).astype(F32)
    logits = jnp.where(lane < N_EXPERTS, logits, NEG_INF)
    v1 = jnp.max(logits, axis=-1, keepdims=True)
    i1 = jnp.min(jnp.where(logits == v1, lane, float(LANES)), axis=-1, keepdims=True)
    rest = jnp.where(lane == i1, NEG_INF, logits)
    v2 = jnp.max(rest, axis=-1, keepdims=True)
    i2 = jnp.min(jnp.where(rest == v2, lane, float(LANES)), axis=-1, keepdims=True)
    e2 = jnp.exp(v2 - v1)
    g1 = 1.0 / (1.0 + e2)
    g2 = e2 / (1.0 + e2)
    o_ref[...] = jnp.where(lane == i1, g1, 0.0) + jnp.where(lane == i2, g2, 0.0)


def _router(h, router_w, router_b, tm):
    m, d = h.shape
    pad = LANES - N_EXPERTS
    w = jnp.pad(router_w, ((0, 0), (0, pad)))
    whi = w.astype(BF16)
    wlo = (w - whi.astype(F32)).astype(BF16)
    b = jnp.pad(router_b, (0, pad)).reshape(1, LANES)
    return pl.pallas_call(
        _router_kernel,
        grid=(m // tm,),
        in_specs=[pl.BlockSpec((tm, d), lambda i: (i, 0)),
                  pl.BlockSpec((d, LANES), lambda i: (0, 0)),
                  pl.BlockSpec((d, LANES), lambda i: (0, 0)),
                  pl.BlockSpec((1, LANES), lambda i: (0, 0))],
        out_specs=pl.BlockSpec((tm, LANES), lambda i: (i, 0)),
        out_shape=jax.ShapeDtypeStruct((m, LANES), F32),
        compiler_params=_params(1),
        name="router",
    )(h, whi, wlo, b)


def _trunk(x, mods, start, pool_past, attend, p, batch, seq, tm, tm_out, tm_ffn, k_stack, v_stack):
    depth = p["w_in"].shape[0]
    m, d = x.shape
    tpg = (m // mods[0][0].shape[0]) // tm
    tpg_out = (m // mods[0][0].shape[0]) // tm_out
    tpg_ffn = (m // mods[0][0].shape[0]) // tm_ffn
    pools = []
    sh_m, sc_m = mods[0][0], mods[0][1]
    h = _norm(x, p["g_norm_mix"], 0, sc_m, sh_m, tm, tpg)
    for l in range(depth):
        sh_m, sc_m, gt_m, sh_f, sc_f, gt_f = mods[l]
        if l > 0:
            h = _norm(x, p["g_norm_mix"], l, sc_m, sh_m, tm, tpg)
        g_q = jnp.tile(p["g_q"][l], 2).reshape(1, LANES)
        g_k = jnp.tile(p["g_k"][l], 2).reshape(1, LANES)
        q = _inproj(h, p["w_in"], l, 0, "q", tm, g128=g_q)
        k_stack, k_bf = _inproj(h, p["w_in"], l, 1, "k", tm, g128=g_k, stack=k_stack)
        v_stack, v_bf = _inproj(h, p["w_in"], l, 2, "v", tm, stack=v_stack)
        u = _inproj(h, p["w_in"], l, 3, "u", tm)
        lam = (jnp.exp(jnp.sum(p["lambda_q1"][l] * p["lambda_k1"][l]))
               - jnp.exp(jnp.sum(p["lambda_q2"][l] * p["lambda_k2"][l]))
               + _lambda_init(l)).astype(F32).reshape(1)
        g_sub = p["g_sub"][l].reshape(1, LANES)
        a = attend(l, q, k_bf, v_bf, k_stack, v_stack, lam, g_sub)
        pool_out, pool_new = pool_past(l, u)
        pools.append(pool_new)
        x, h2 = _outproj(a, pool_out, x, p["w_o"], l, gt_m, p["g_norm_ffn"], sc_f, sh_f, tm_out,
                         tpg_out)
        j = l // 2
        if l % 2 == 0:
            x = _ffn(h2, x, gt_f, p["dense_w_gate"], p["dense_w_up"], p["dense_w_down"], j,
                     tm_ffn, 256, tpg_ffn)
        else:
            combine = _router(h2, p["router_w"][j], p["router_b"][j], tm)
            n_exp, _, f_exp = p["moe_w_gate"].shape[1:]
            wg = p["moe_w_gate"].reshape(-1, d, f_exp)
            wu = p["moe_w_up"].reshape(-1, d, f_exp)
            wd = p["moe_w_down"].reshape(-1, f_exp, d)
            for e in range(n_exp):
                x = _ffn(h2, x, gt_f, wg, wu, wd, j * n_exp + e, tm_ffn, 128, tpg_ffn,
                         cw=combine[:, e:e + 1])
    return x, k_stack, v_stack, jnp.stack(pools)


def kernel(x_prompt, x_sample, cache_k, cache_v, state_pool, page_table, c_prompt, c_sample,
           g_norm_mix, g_norm_ffn, w_ada, b_ada, w_in, g_q, g_k,
           lambda_q1, lambda_k1, lambda_q2, lambda_k2, g_sub, w_pool, pool_scale, w_o,
           dense_w_gate, dense_w_up, dense_w_down,
           router_w, router_b, moe_w_gate, moe_w_up, moe_w_down):
    p = dict(g_norm_mix=g_norm_mix.reshape(-1, 1, g_norm_mix.shape[-1]),
             g_norm_ffn=g_norm_ffn.reshape(-1, 1, g_norm_ffn.shape[-1]),
             w_in=w_in, g_q=g_q, g_k=g_k,
             lambda_q1=lambda_q1, lambda_k1=lambda_k1, lambda_q2=lambda_q2, lambda_k2=lambda_k2,
             g_sub=g_sub, w_pool=w_pool, pool_scale=pool_scale, w_o=w_o,
             dense_w_gate=dense_w_gate, dense_w_up=dense_w_up, dense_w_down=dense_w_down,
             router_w=router_w, router_b=router_b,
             moe_w_gate=moe_w_gate, moe_w_up=moe_w_up, moe_w_down=moe_w_down)
    depth = w_in.shape[0]
    bp, tp, d = x_prompt.shape
    bs, ts, _ = x_sample.shape
    width = N_HEADS * LANES
    c_pool = pool_scale.shape[-1]

    n_c = bp + bs
    rows = -(-n_c // 16) * 16
    c_all = jnp.concatenate([c_prompt, c_sample, jnp.zeros((rows - n_c, d), F32)], axis=0)
    mod = _ada(c_all, w_ada, b_ada).reshape(depth, rows, 6, d)
    mods_p = [[mod[l, :bp, i][:, None, :] for i in range(6)] for l in range(depth)]
    mods_s = [[jnp.repeat(mod[l, bp:n_c, i], ts, axis=0)[None] for i in range(6)]
              for l in range(depth)]

    zeros_past = jnp.zeros((bp, POOL_STATE + 1, c_pool), F32)

    def attend_p(l, q, k_bf, v_bf, ks, vs, lam, gs):
        return _attn_prompt(q, k_bf, v_bf, l, lam, gs, bp, tp, 512)

    def pool_p(l, u):
        u3 = u.reshape(bp, tp, c_pool)
        out = _pool(u3, zeros_past, w_pool, pool_scale, l, 0)
        return out.reshape(bp * tp, c_pool), u3[:, tp - POOL_STATE:]

    kp0 = jnp.zeros((depth, bp * tp, N_HEADS, LANES), F32)
    vp0 = jnp.zeros((depth, bp * tp, N_HEADS, LANES), F32)
    y_p, k_p, v_p, pool_p_new = _trunk(x_prompt.reshape(bp * tp, d), mods_p, 0, pool_p, attend_p,
                                       p, bp, tp, 512, 256, 1024, kp0, vp0)

    past_len = page_table.shape[1] * PAGE_SIZE
    t_pad = 16

    def attend_s(l, q, k_bf, v_bf, ks, vs, lam, gs):
        return _attn_decode(q, ks, vs, cache_k, cache_v, page_table, l, lam, gs, ts, 8)

    def pool_s(l, u):
        u3 = u.reshape(bs, ts, c_pool)
        past16 = jnp.concatenate([jnp.zeros((bs, 1, c_pool), F32), state_pool[l]], axis=1)
        u_pad = jnp.concatenate([u3, jnp.zeros((bs, t_pad - ts, c_pool), F32)], axis=1)
        out = _pool(u_pad, past16, w_pool, pool_scale, l, past_len)
        new = jnp.concatenate([state_pool[l], u3], axis=1)[:, -POOL_STATE:]
        return out[:, :ts].reshape(bs * ts, c_pool), new

    ks0 = jnp.zeros((depth, bs * ts, N_HEADS, LANES), F32)
    vs0 = jnp.zeros((depth, bs * ts, N_HEADS, LANES), F32)
    y_s, k_s, v_s, pool_s_new = _trunk(x_sample.reshape(bs * ts, d), mods_s, past_len, pool_s,
                                       attend_s, p, bs, ts, bs * ts, bs * ts, bs * ts, ks0, vs0)

    return (y_p.reshape(bp, tp, d), y_s.reshape(bs, ts, d),
            k_p.reshape(depth, bp, tp, N_HEADS, LANES), v_p.reshape(depth, bp, tp, N_HEADS, LANES),
            pool_p_new,
            k_s.reshape(depth, bs, ts, N_HEADS, LANES), v_s.reshape(depth, bs, ts, N_HEADS, LANES),
            pool_s_new)
```

```python
import functools
import math

import jax
import jax.numpy as jnp
from jax import lax
from jax.experimental import pallas as pl
from jax.experimental.pallas import tpu as pltpu

F32 = jnp.float32
BF16 = jnp.bfloat16

N_HEADS = 8
HEAD_DIM_V = 128
HEAD_DIM_QK = 64
POOL_WINDOWS = (2, 4, 8, 16)
POOL_STATE = 15
PAGE_SIZE = 128
N_EXPERTS = 8
EPS = 1e-6
NEG_INF = -1e30
LANES = 128
VMEM_LIMIT = 56 * 1024 * 1024
COL_CHUNK = 512


def _params(n_axes, vmem=VMEM_LIMIT):
    return pltpu.CompilerParams(dimension_semantics=("arbitrary",) * n_axes,
                                vmem_limit_bytes=vmem)


def _lambda_init(layer):
    return 0.8 - 0.6 * math.exp(-0.3 * layer)


def _mod_spec(arr, tiles_per_group):
    _, r, d = arr.shape
    return pl.BlockSpec((1, r, d), lambda m, *_: (m // tiles_per_group, 0, 0))


def _norm_mod(x, g, sc, sh):
    y = x * lax.rsqrt(jnp.mean(x * x, axis=-1, keepdims=True) + EPS)
    return (y * g) * (1.0 + sc) + sh


def _ada_kernel(c_ref, w_ref, b_ref, o_ref):
    c = c_ref[...]
    s = (c * jax.nn.sigmoid(c)).astype(BF16)
    o_ref[0] = jnp.dot(s, w_ref[0].astype(BF16), preferred_element_type=F32) + b_ref[0]


def _ada(c_all, w_ada, b_ada):
    depth, d, n = w_ada.shape
    rows = c_all.shape[0]
    tn = 1024
    return pl.pallas_call(
        _ada_kernel,
        grid=(depth, n // tn),
        in_specs=[pl.BlockSpec((rows, d), lambda l, j: (0, 0)),
                  pl.BlockSpec((1, d, tn), lambda l, j: (l, 0, j)),
                  pl.BlockSpec((1, 1, tn), lambda l, j: (l, 0, j))],
        out_specs=pl.BlockSpec((1, rows, tn), lambda l, j: (l, 0, j)),
        out_shape=jax.ShapeDtypeStruct((depth, rows, n), F32),
        compiler_params=_params(2),
        name="ada_mod",
    )(c_all, w_ada, b_ada.reshape(depth, 1, n))


def _norm_kernel(x_ref, g_ref, sc_ref, sh_ref, h_ref):
    h_ref[...] = _norm_mod(x_ref[...], g_ref[0], sc_ref[0], sh_ref[0]).astype(BF16)


def _norm(x, g_all, layer, sc, sh, tm, tiles_per_group):
    m, d = x.shape
    return pl.pallas_call(
        _norm_kernel,
        grid=(m // tm,),
        in_specs=[pl.BlockSpec((tm, d), lambda i: (i, 0)),
                  pl.BlockSpec((1, 1, d), lambda i: (layer, 0, 0)),
                  _mod_spec(sc, tiles_per_group), _mod_spec(sh, tiles_per_group)],
        out_specs=pl.BlockSpec((tm, d), lambda i: (i, 0)),
        out_shape=jax.ShapeDtypeStruct((m, d), BF16),
        compiler_params=_params(1),
        name="norm_mod",
    )(x, g_all, sc, sh)


def _qk_norm(p, g128, scale):
    lo = lax.broadcasted_iota(jnp.int32, (1, LANES), 1) < HEAD_DIM_QK
    sq = p * p
    s0 = jnp.sum(jnp.where(lo, sq, 0.0), axis=-1, keepdims=True)
    s1 = jnp.sum(jnp.where(lo, 0.0, sq), axis=-1, keepdims=True)
    inv0 = lax.rsqrt(s0 * (1.0 / HEAD_DIM_QK) + EPS)
    inv1 = lax.rsqrt(s1 * (1.0 / HEAD_DIM_QK) + EPS)
    y = (p * jnp.where(lo, inv0, inv1)) * g128
    return y if scale is None else y * scale


def _inproj_kernel(*refs, mode):
    normed = mode in ("q", "k")
    stacked = mode in ("k", "v")
    h_ref, w_ref = refs[:2]
    g_ref = refs[2] if normed else None
    wbf_ref = refs[-1]
    outs = refs[-3:-1] if stacked else refs[-2:-1]

    @pl.when(pl.program_id(0) == 0)
    def _():
        wbf_ref[...] = w_ref[0].astype(BF16)

    h = h_ref[...]
    scale = HEAD_DIM_QK ** -0.5 if mode == "q" else None
    for c0 in range(0, wbf_ref.shape[1], COL_CHUNK):
        acc = jnp.dot(h, wbf_ref[:, c0:c0 + COL_CHUNK], preferred_element_type=F32)
        if mode == "u":
            outs[0][:, c0:c0 + COL_CHUNK] = acc
            continue
        for hh in range(COL_CHUNK // LANES):
            head = c0 // LANES + hh
            y = acc[:, hh * LANES:(hh + 1) * LANES]
            if normed:
                y = _qk_norm(y, g_ref[...], scale)
            if stacked:
                outs[0][0, :, head, :] = y
                outs[1][:, head * LANES:(head + 1) * LANES] = y.astype(BF16)
            else:
                outs[0][:, head * LANES:(head + 1) * LANES] = y.astype(BF16)


def _inproj(h, w_in, layer, section, mode, tm, g128=None, stack=None):
    m, d = h.shape
    width = N_HEADS * LANES
    ins = [h, w_in]
    in_specs = [pl.BlockSpec((tm, d), lambda i: (i, 0)),
                pl.BlockSpec((1, d, width), lambda i: (layer, 0, section))]
    if g128 is not None:
        ins.append(g128)
        in_specs.append(pl.BlockSpec((1, LANES), lambda i: (0, 0)))
    flat_spec = pl.BlockSpec((tm, width), lambda i: (i, 0))
    aliases = {}
    if stack is not None:
        aliases = {len(ins): 0}
        ins.append(stack)
        in_specs.append(pl.BlockSpec(memory_space=pl.ANY))
        out_specs = [pl.BlockSpec((1, tm, N_HEADS, LANES), lambda i: (layer, i, 0, 0)), flat_spec]
        out_shape = [jax.ShapeDtypeStruct(stack.shape, stack.dtype),
                     jax.ShapeDtypeStruct((m, width), BF16)]
    else:
        out_specs = [flat_spec]
        out_shape = [jax.ShapeDtypeStruct((m, width), BF16 if mode == "q" else F32)]
    n_in = len(ins)

    def body(*refs):
        if stack is not None:
            refs = refs[:n_in - 1] + refs[n_in:]
        _inproj_kernel(*refs, mode=mode)

    out = pl.pallas_call(
        body,
        grid=(m // tm,),
        in_specs=in_specs,
        out_specs=out_specs,
        out_shape=out_shape,
        scratch_shapes=[pltpu.VMEM((d, width), BF16)],
        input_output_aliases=aliases,
        compiler_params=_params(1),
        name="inproj_" + mode,
    )(*ins)
    return out if stack is not None else out[0]


def _sub_norm(o, g_sub, li):
    y = o * lax.rsqrt(jnp.mean(o * o, axis=-1, keepdims=True) + EPS)
    return (y * g_sub) * (1.0 - li)


def _attn_kernel(lam_ref, q_ref, k_ref, v_ref, g_ref, o_ref, vt_ref, m_ref, l_ref, acc_ref,
                 *, tq, li):
    qi = pl.program_id(2)

    @pl.when(qi == 0)
    def _():
        for jt in range(vt_ref.shape[0]):
            vt_ref[jt] = v_ref[0, jt * tq:(jt + 1) * tq, :].astype(F32).T.astype(BF16)

    lo = lax.broadcasted_iota(jnp.int32, (1, LANES), 1) < HEAD_DIM_QK
    q = q_ref[0]
    zero = jnp.zeros_like(q)
    qmaps = (jnp.where(lo, q, zero), jnp.where(lo, zero, q))
    m_ref[...] = jnp.full(m_ref.shape, NEG_INF, F32)
    l_ref[...] = jnp.zeros(l_ref.shape, F32)
    acc_ref[...] = jnp.zeros(acc_ref.shape, F32)

    def step(j, masked):
        kt = k_ref[0, pl.ds(pl.multiple_of(j * tq, tq), tq), :]
        vt = vt_ref[j]
        for mp in range(2):
            s = lax.dot_general(kt, qmaps[mp], (((1,), (1,)), ((), ())),
                                preferred_element_type=F32)
            if masked:
                key = lax.broadcasted_iota(jnp.int32, (tq, tq), 0)
                qry = lax.broadcasted_iota(jnp.int32, (tq, tq), 1)
                s = jnp.where(key <= qry, s, NEG_INF)
            m_old = m_ref[mp]
            m_new = jnp.maximum(m_old, jnp.max(s, axis=0, keepdims=True))
            alpha = jnp.exp(m_old - m_new)
            p = jnp.exp(s - m_new)
            l_ref[mp] = alpha * l_ref[mp] + jnp.sum(p, axis=0, keepdims=True)
            acc_ref[mp] = alpha * acc_ref[mp] + jnp.dot(vt, p.astype(BF16),
                                                        preferred_element_type=F32)
            m_ref[mp] = m_new

    def off_diag(j, carry):
        step(j, False)
        return carry

    lax.fori_loop(0, qi, off_diag, 0)
    step(qi, True)

    o = acc_ref[0] / l_ref[0] - lam_ref[0] * (acc_ref[1] / l_ref[1])
    y = o * lax.rsqrt(jnp.mean(o * o, axis=0, keepdims=True) + EPS)
    o_ref[0] = ((y.T * g_ref[...]) * (1.0 - li)).astype(BF16)


def _attn_prompt(q, k, v, layer, lam, g_sub, batch, seq, tq):
    width = N_HEADS * LANES
    q3, k3, v3 = (t.reshape(batch, seq, width) for t in (q, k, v))
    kv_spec = pl.BlockSpec((1, seq, LANES), lambda b, h, i: (b, 0, h))
    out = pl.pallas_call(
        functools.partial(_attn_kernel, tq=tq, li=_lambda_init(layer)),
        grid=(batch, N_HEADS, seq // tq),
        in_specs=[pl.BlockSpec(memory_space=pltpu.SMEM),
                  pl.BlockSpec((1, tq, LANES), lambda b, h, i: (b, i, h)),
                  kv_spec, kv_spec,
                  pl.BlockSpec((1, LANES), lambda b, h, i: (0, 0))],
        out_specs=pl.BlockSpec((1, tq, LANES), lambda b, h, i: (b, i, h)),
        out_shape=jax.ShapeDtypeStruct((batch, seq, width), BF16),
        scratch_shapes=[pltpu.VMEM((seq // tq, LANES, tq), BF16),
                        pltpu.VMEM((2, 1, tq), F32), pltpu.VMEM((2, 1, tq), F32),
                        pltpu.VMEM((2, LANES, tq), F32)],
        compiler_params=_params(3),
        name="attn_prompt",
    )(lam, q3, k3, v3, g_sub)
    return out.reshape(batch * seq, width)


def _decode_kernel(pt_ref, lam_ref, q_ref, *refs, pages, n_new, li):
    k_refs = refs[:pages]
    v_refs = refs[pages:2 * pages]
    kn_ref, vn_ref, g_ref, o_ref, knp_ref, vnp_ref, m_ref, l_ref, acc_ref = refs[2 * pages:]
    b = pl.program_id(0)
    c = pl.program_id(1)
    last = pl.num_programs(1) - 1
    shift_q = (2 * n_new).bit_length() - 1

    @pl.when(jnp.logical_and(b == 0, c == 0))
    def _():
        knp_ref[...] = jnp.zeros(knp_ref.shape, F32)
        vnp_ref[...] = jnp.zeros(vnp_ref.shape, F32)

    @pl.when(c == 0)
    def _():
        m_ref[...] = jnp.full(m_ref.shape, NEG_INF, F32)
        l_ref[...] = jnp.zeros(l_ref.shape, F32)
        acc_ref[...] = jnp.zeros(acc_ref.shape, F32)

    q = q_ref[0]
    n_q = q.shape[0]
    page_rows = PAGE_SIZE * N_HEADS
    def own_head(n_cols):
        row = lax.broadcasted_iota(jnp.int32, (n_q, n_cols), 0)
        col = lax.broadcasted_iota(jnp.int32, (n_q, n_cols), 1)
        return row, col, jnp.bitwise_and(col, N_HEADS - 1) == lax.shift_right_logical(row, shift_q)

    def logits(k2d, keep):
        s = lax.dot_general(q, k2d.astype(BF16), (((1,), (1,)), ((), ())),
                            preferred_element_type=F32)
        return jnp.where(keep, s, NEG_INF)

    def update(s_list, v_list):
        m_old = m_ref[...]
        m_new = m_old
        for s in s_list:
            m_new = jnp.maximum(m_new, jnp.max(s, axis=-1, keepdims=True))
        alpha = jnp.exp(m_old - m_new)
        l_new = alpha * l_ref[...]
        pv = None
        for s, v in zip(s_list, v_list):
            p = jnp.exp(s - m_new)
            l_new = l_new + jnp.sum(p, axis=-1, keepdims=True)
            d = jnp.dot(p.astype(BF16), v, preferred_element_type=F32)
            pv = d if pv is None else pv + d
        acc_ref[...] = alpha * acc_ref[...] + pv
        l_ref[...] = l_new
        m_ref[...] = m_new

    s_list, v_list = [], []
    page_keep = own_head(page_rows)[2]
    for j in range(pages):
        s_list.append(logits(k_refs[j][0, 0].reshape(page_rows, LANES), page_keep))
        v_list.append(v_refs[j][0, 0].reshape(page_rows, LANES).astype(BF16))
    update(s_list, v_list)

    @pl.when(c == last)
    def _():
        n_rows = n_new * N_HEADS
        knp_ref[0:n_rows, :] = kn_ref[0, 0]
        vnp_ref[0:n_rows, :] = vn_ref[0, 0]
        row, col, keep = own_head(knp_ref.shape[0])
        tok = jnp.bitwise_and(row, n_new - 1)
        slot = lax.shift_right_logical(col, N_HEADS.bit_length() - 1)
        keep = jnp.logical_and(keep, slot <= tok)
        update([logits(knp_ref[...], keep)], [vnp_ref[...].astype(BF16)])

        o = acc_ref[...] / l_ref[...]
        o = o - lam_ref[0] * pltpu.roll(o, shift=n_q - n_new, axis=0)
        o_ref[0] = _sub_norm(o, g_ref[...], li)


def _attn_decode(q, k_stack, v_stack, cache_k, cache_v, page_table, layer, lam, g_sub,
                 n_new, pages):
    depth = cache_k.shape[0]
    batch, n_pages = page_table.shape
    assert n_new & (n_new - 1) == 0 and N_HEADS & (N_HEADS - 1) == 0
    n_q = N_HEADS * 2 * n_new
    qh = q.reshape(batch, n_new, N_HEADS, 1, LANES).transpose(0, 2, 3, 1, 4)
    in_map = (jnp.arange(LANES) // HEAD_DIM_QK)[None, :] == jnp.arange(2)[:, None]
    qm = jnp.where(in_map[None, None, :, None, :], qh, jnp.zeros_like(qh)).reshape(batch, n_q, LANES)
    kn = k_stack.reshape(depth, batch, n_new * N_HEADS, LANES)
    vn = v_stack.reshape(depth, batch, n_new * N_HEADS, LANES)

    def page_spec(j):
        return pl.BlockSpec((1, 1, PAGE_SIZE, N_HEADS, LANES),
                            lambda b, c, pt: (layer, pt[b, c * pages + j], 0, 0, 0))

    new_spec = pl.BlockSpec((1, 1, n_new * N_HEADS, LANES), lambda b, c, pt: (layer, b, 0, 0))
    grid_spec = pltpu.PrefetchScalarGridSpec(
        num_scalar_prefetch=1,
        grid=(batch, n_pages // pages),
        in_specs=([pl.BlockSpec(memory_space=pltpu.SMEM),
                   pl.BlockSpec((1, n_q, LANES), lambda b, c, pt: (b, 0, 0))]
                  + [page_spec(j) for j in range(pages)]
                  + [page_spec(j) for j in range(pages)]
                  + [new_spec, new_spec,
                     pl.BlockSpec((1, LANES), lambda b, c, pt: (0, 0))]),
        out_specs=pl.BlockSpec((1, n_q, LANES), lambda b, c, pt: (b, 0, 0)),
        scratch_shapes=[pltpu.VMEM((LANES, LANES), F32),
                        pltpu.VMEM((LANES, LANES), F32),
                        pltpu.VMEM((n_q, 1), F32),
                        pltpu.VMEM((n_q, 1), F32),
                        pltpu.VMEM((n_q, LANES), F32)],
    )
    out = pl.pallas_call(
        functools.partial(_decode_kernel, pages=pages, n_new=n_new, li=_lambda_init(layer)),
        grid_spec=grid_spec,
        out_shape=jax.ShapeDtypeStruct((batch, n_q, LANES), F32),
        compiler_params=_params(2),
        name="attn_decode",
    )(page_table, lam, qm, *([cache_k] * pages), *([cache_v] * pages), kn, vn, g_sub)
    out = out.reshape(batch, N_HEADS, 2, n_new, LANES)[:, :, 0]
    return out.transpose(0, 2, 1, 3).reshape(batch * n_new, N_HEADS * LANES).astype(BF16)


def _pool_kernel(u_ref, past_ref, w_ref, sc_ref, o_ref, full_ref, *, start, chunk):
    halo = past_ref.shape[1]
    t = u_ref.shape[1]
    group = u_ref.shape[2] // len(POOL_WINDOWS)
    full_ref[0:halo, :] = past_ref[0]
    full_ref[halo:halo + t, :] = u_ref[0]
    for c0 in range(0, t, chunk):
        pos = start + c0 + lax.broadcasted_iota(jnp.int32, (chunk, 1), 0)
        for g, w in enumerate(POOL_WINDOWS):
            cols = slice(g * group, (g + 1) * group)
            cur = full_ref[halo + c0:halo + c0 + chunk, cols]
            win = cur
            for i in range(1, w):
                win = win + full_ref[halo + c0 - i:halo + c0 - i + chunk, cols]
            count = jnp.minimum(w, pos + 1).astype(F32)
            pooled = (win / count - cur).astype(BF16)
            y = jnp.dot(pooled, w_ref[0, g].astype(BF16), preferred_element_type=F32)
            o_ref[0, c0:c0 + chunk, cols] = (y * sc_ref[0, :, cols]).astype(BF16)


def _pool(u3, past16, w_pool, pool_scale, layer, start):
    batch, t, c = u3.shape
    depth = w_pool.shape[0]
    group = c // len(POOL_WINDOWS)
    chunk = min(t, 256)
    return pl.pallas_call(
        functools.partial(_pool_kernel, start=start, chunk=chunk),
        grid=(batch,),
        in_specs=[pl.BlockSpec((1, t, c), lambda b: (b, 0, 0)),
                  pl.BlockSpec((1, POOL_STATE + 1, c), lambda b: (b, 0, 0)),
                  pl.BlockSpec((1, len(POOL_WINDOWS), group, group), lambda b: (layer, 0, 0, 0)),
                  pl.BlockSpec((1, 1, c), lambda b: (layer, 0, 0))],
        out_specs=pl.BlockSpec((1, t, c), lambda b: (b, 0, 0)),
        out_shape=jax.ShapeDtypeStruct((batch, t, c), BF16),
        scratch_shapes=[pltpu.VMEM((POOL_STATE + 1 + t, c), F32)],
        compiler_params=_params(1),
        name="pool_mix",
    )(u3, past16, w_pool, pool_scale.reshape(depth, 1, c))


def _outproj_kernel(a_ref, p_ref, x_ref, w_ref, gt_ref, g_ref, sc_ref, sh_ref,
                    x1_ref, h2_ref, wbf_ref):
    @pl.when(pl.program_id(0) == 0)
    def _():
        wbf_ref[...] = w_ref[0].astype(BF16)

    half = a_ref.shape[1]
    a = a_ref[...]
    p = p_ref[...]
    for c0 in range(0, x_ref.shape[1], COL_CHUNK):
        cols = slice(c0, c0 + COL_CHUNK)
        y = jnp.dot(a, wbf_ref[0:half, cols], preferred_element_type=F32)
        y = y + jnp.dot(p, wbf_ref[half:, cols], preferred_element_type=F32)
        x1_ref[:, cols] = x_ref[:, cols] + gt_ref[0, :, cols] * y
    h2_ref[...] = _norm_mod(x1_ref[...], g_ref[0], sc_ref[0], sh_ref[0]).astype(h2_ref.dtype)


def _outproj(a, p, x, w_o, layer, gt, g_ffn, sc, sh, tm, tiles_per_group, h2_dtype):
    m, d = x.shape
    half = a.shape[1]
    return pl.pallas_call(
        _outproj_kernel,
        grid=(m // tm,),
        in_specs=[pl.BlockSpec((tm, half), lambda i: (i, 0)),
                  pl.BlockSpec((tm, half), lambda i: (i, 0)),
                  pl.BlockSpec((tm, d), lambda i: (i, 0)),
                  pl.BlockSpec((1, d, d), lambda i: (layer, 0, 0),
                               pipeline_mode=pl.Buffered(1)),
                  _mod_spec(gt, tiles_per_group),
                  pl.BlockSpec((1, 1, d), lambda i: (layer, 0, 0)),
                  _mod_spec(sc, tiles_per_group), _mod_spec(sh, tiles_per_group)],
        out_specs=[pl.BlockSpec((tm, d), lambda i: (i, 0)),
                   pl.BlockSpec((tm, d), lambda i: (i, 0))],
        out_shape=[jax.ShapeDtypeStruct((m, d), F32), jax.ShapeDtypeStruct((m, d), h2_dtype)],
        scratch_shapes=[pltpu.VMEM((d, d), BF16)],
        compiler_params=_params(1),
        name="outproj",
    )(a, p, x, w_o, gt, g_ffn, sc, sh)


def _ffn_kernel(*refs, row_weight):
    if row_weight:
        h_ref, x_ref, gt_ref, wg_ref, wu_ref, wd_ref, cw_ref, o_ref = refs
    else:
        h_ref, x_ref, gt_ref, wg_ref, wu_ref, wd_ref, o_ref = refs

    @pl.when(pl.program_id(1) == 0)
    def _():
        o_ref[...] = x_ref[...]

    h = h_ref[...]
    g = jnp.dot(h, wg_ref[0].astype(BF16), preferred_element_type=F32)
    u = jnp.dot(h, wu_ref[0].astype(BF16), preferred_element_type=F32)
    act = ((g * jax.nn.sigmoid(g)) * u).astype(BF16)
    for c0 in range(0, o_ref.shape[1], COL_CHUNK):
        cols = slice(c0, c0 + COL_CHUNK)
        y = jnp.dot(act, wd_ref[0, :, cols].astype(BF16), preferred_element_type=F32)
        scale = gt_ref[0, :, cols]
        if row_weight:
            scale = scale * cw_ref[...]
        o_ref[:, cols] += scale * y


def _ffn(h, x, gt, wg, wu, wd, widx, tm, tf, tiles_per_group, cw=None):
    m, d = x.shape
    f = wg.shape[2]
    ins = [h, x, gt, wg, wu, wd]
    in_specs = [pl.BlockSpec((tm, d), lambda i, j: (i, 0)),
                pl.BlockSpec((tm, d), lambda i, j: (i, 0), pipeline_mode=pl.Buffered(1)),
                _mod_spec(gt, tiles_per_group),
                pl.BlockSpec((1, d, tf), lambda i, j: (widx, 0, j)),
                pl.BlockSpec((1, d, tf), lambda i, j: (widx, 0, j)),
                pl.BlockSpec((1, tf, d), lambda i, j: (widx, j, 0))]
    if cw is not None:
        ins.append(cw)
        in_specs.append(pl.BlockSpec((tm, 1), lambda i, j: (i, 0)))
    return pl.pallas_call(
        functools.partial(_ffn_kernel, row_weight=cw is not None),
        grid=(m // tm, f // tf),
        in_specs=in_specs,
        out_specs=pl.BlockSpec((tm, d), lambda i, j: (i, 0)),
        out_shape=jax.ShapeDtypeStruct((m, d), F32),
        compiler_params=_params(2),
        name="ffn",
    )(*ins)


ROUTE_E1, ROUTE_E2, ROUTE_POS1, ROUTE_POS2, ROUTE_G1, ROUTE_G2 = range(N_EXPERTS, N_EXPERTS + 6)


def _router_kernel(h_ref, whi_ref, wlo_ref, b_ref, o_ref, cnt_ref, carry_ref):
    @pl.when(pl.program_id(0) == 0)
    def _():
        carry_ref[...] = jnp.zeros(carry_ref.shape, F32)

    h = h_ref[...].astype(BF16)
    logits = (jnp.dot(h, whi_ref[...], preferred_element_type=F32)
              + jnp.dot(h, wlo_ref[...], preferred_element_type=F32) + b_ref[...])
    lane = lax.broadcasted_iota(jnp.int32, logits.shape, 1).astype(F32)
    logits = jnp.where(lane < N_EXPERTS, logits, NEG_INF)
    v1 = jnp.max(logits, axis=-1, keepdims=True)
    i1 = jnp.min(jnp.where(logits == v1, lane, float(LANES)), axis=-1, keepdims=True)
    rest = jnp.where(lane == i1, NEG_INF, logits)
    v2 = jnp.max(rest, axis=-1, keepdims=True)
    i2 = jnp.min(jnp.where(rest == v2, lane, float(LANES)), axis=-1, keepdims=True)
    e2 = jnp.exp(v2 - v1)
    g1 = 1.0 / (1.0 + e2)
    g2 = e2 / (1.0 + e2)

    tm = h.shape[0]
    sel = jnp.where(jnp.logical_or(lane == i1, lane == i2), 1.0, 0.0)
    earlier = (lax.broadcasted_iota(jnp.int32, (tm, tm), 1)
               < lax.broadcasted_iota(jnp.int32, (tm, tm), 0))
    rank = jnp.dot(jnp.where(earlier, 1.0, 0.0).astype(BF16), sel.astype(BF16),
                   preferred_element_type=F32) + carry_ref[...]
    pos1 = jnp.sum(jnp.where(lane == i1, rank, 0.0), axis=-1, keepdims=True)
    pos2 = jnp.sum(jnp.where(lane == i2, rank, 0.0), axis=-1, keepdims=True)
    total = carry_ref[...] + jnp.sum(sel, axis=0, keepdims=True)
    carry_ref[...] = total
    cnt_ref[...] = total

    out = jnp.where(lane == i1, g1, 0.0) + jnp.where(lane == i2, g2, 0.0)
    for idx, val in ((ROUTE_E1, i1), (ROUTE_E2, i2), (ROUTE_POS1, pos1), (ROUTE_POS2, pos2),
                     (ROUTE_G1, g1), (ROUTE_G2, g2)):
        out = out + jnp.where(lane == float(idx), val, 0.0)
    o_ref[...] = out


def _router(h, router_w, router_b, tm):
    m, d = h.shape
    pad = LANES - N_EXPERTS
    w = jnp.pad(router_w, ((0, 0), (0, pad)))
    whi = w.astype(BF16)
    wlo = (w - whi.astype(F32)).astype(BF16)
    b = jnp.pad(router_b, (0, pad)).reshape(1, LANES)
    return pl.pallas_call(
        _router_kernel,
        grid=(m // tm,),
        in_specs=[pl.BlockSpec((tm, d), lambda i: (i, 0)),
                  pl.BlockSpec((d, LANES), lambda i: (0, 0)),
                  pl.BlockSpec((d, LANES), lambda i: (0, 0)),
                  pl.BlockSpec((1, LANES), lambda i: (0, 0))],
        out_specs=[pl.BlockSpec((tm, LANES), lambda i: (i, 0)),
                   pl.BlockSpec((1, LANES), lambda i: (0, 0))],
        out_shape=[jax.ShapeDtypeStruct((m, LANES), F32),
                   jax.ShapeDtypeStruct((1, LANES), F32)],
        scratch_shapes=[pltpu.VMEM((1, LANES), F32)],
        compiler_params=_params(1),
        name="router",
    )(h, whi, wlo, b)


def _route_plan(route, counts, tm):
    m = route.shape[0]
    n_tiles = 2 * m // tm + N_EXPERTS
    col = lambda i: route[:, i].astype(jnp.int32)
    cnt = counts[0, :N_EXPERTS].astype(jnp.int32)
    tiles = (cnt + tm - 1) // tm
    tile_end = jnp.cumsum(tiles)
    row_start = (tile_end - tiles) * tm
    d1 = row_start[col(ROUTE_E1)] + col(ROUTE_POS1)
    d2 = row_start[col(ROUTE_E2)] + col(ROUTE_POS2)
    token = jnp.arange(m, dtype=jnp.int32)
    src = jnp.zeros((n_tiles * tm,), jnp.int32).at[d1].set(token).at[d2].set(token)
    tile_expert = jnp.minimum(
        jnp.searchsorted(tile_end, jnp.arange(n_tiles, dtype=jnp.int32), side="right"),
        N_EXPERTS - 1).astype(jnp.int32)
    return d1, d2, src, tile_expert, tile_end[-1:].astype(jnp.int32)


def _row_copy(src_hbm, row, dst_vmem, slot, sem):
    return pltpu.make_async_copy(src_hbm.at[pl.ds(row, 1), :], dst_vmem.at[pl.ds(slot, 1), :], sem)


def _gather_kernel(nv_ref, src_ref, h_hbm, o_ref, buf_ref, sem):
    i = pl.program_id(0)
    tm = buf_ref.shape[0]

    @pl.when(i < nv_ref[0])
    def _():
        def issue(r, c):
            _row_copy(h_hbm, src_ref[0, 0, r], buf_ref, r, sem).start()
            return c

        def wait(r, c):
            _row_copy(h_hbm, src_ref[0, 0, r], buf_ref, r, sem).wait()
            return c

        lax.fori_loop(0, tm, issue, 0)
        lax.fori_loop(0, tm, wait, 0)
        o_ref[...] = buf_ref[...].astype(BF16)

    @pl.when(i >= nv_ref[0])
    def _():
        o_ref[...] = jnp.zeros(o_ref.shape, BF16)


def _gather_rows(h, src, n_valid, tm):
    d = h.shape[1]
    n_rows = src.shape[0]
    grid_spec = pltpu.PrefetchScalarGridSpec(
        num_scalar_prefetch=1,
        grid=(n_rows // tm,),
        in_specs=[pl.BlockSpec((1, 1, tm), lambda i, nv: (i, 0, 0), memory_space=pltpu.SMEM),
                  pl.BlockSpec(memory_space=pl.ANY)],
        out_specs=pl.BlockSpec((tm, d), lambda i, nv: (i, 0)),
        scratch_shapes=[pltpu.VMEM((tm, d), F32), pltpu.SemaphoreType.DMA(())],
    )
    return pl.pallas_call(
        _gather_kernel,
        grid_spec=grid_spec,
        out_shape=jax.ShapeDtypeStruct((n_rows, d), BF16),
        compiler_params=_params(1),
        name="moe_gather",
    )(n_valid, src.reshape(n_rows // tm, 1, tm), h)


def _expert_ffn_kernel(te_ref, nv_ref, h_ref, wg_ref, wu_ref, wd_ref, wgt_ref, wut_ref, wdt_ref,
                       o_ref, *, n_main):
    i = pl.program_id(0)
    j = pl.program_id(1)

    @pl.when(j == 0)
    def _():
        o_ref[...] = jnp.zeros(o_ref.shape, F32)

    def accumulate(wg, wu, wd):
        h = h_ref[...]
        g = jnp.dot(h, wg[0].astype(BF16), preferred_element_type=F32)
        u = jnp.dot(h, wu[0].astype(BF16), preferred_element_type=F32)
        act = ((g * jax.nn.sigmoid(g)) * u).astype(BF16)
        for c0 in range(0, o_ref.shape[1], COL_CHUNK):
            cols = slice(c0, c0 + COL_CHUNK)
            o_ref[:, cols] += jnp.dot(act, wd[0, :, cols].astype(BF16),
                                      preferred_element_type=F32)

    in_use = i < nv_ref[0]

    @pl.when(jnp.logical_and(in_use, j < n_main))
    def _():
        accumulate(wg_ref, wu_ref, wd_ref)

    @pl.when(jnp.logical_and(in_use, j == n_main))
    def _():
        accumulate(wgt_ref, wut_ref, wdt_ref)


def _expert_ffn(xs, wg, wu, wd, layer_base, tile_expert, n_valid, tm):
    n_rows, d = xs.shape
    f = wg.shape[2]
    wide, tail = 2 * LANES, LANES
    n_main = f // wide
    assert f == n_main * wide + tail
    tail_blk = f // tail - 1

    def main_blk(i, j, nv):
        return jnp.where(i < nv[0], jnp.minimum(j, n_main - 1), 0)

    grid_spec = pltpu.PrefetchScalarGridSpec(
        num_scalar_prefetch=2,
        grid=(n_rows // tm, n_main + 1),
        in_specs=[
            pl.BlockSpec((tm, d), lambda i, j, te, nv: (i, 0)),
            pl.BlockSpec((1, d, wide), lambda i, j, te, nv: (layer_base + te[i], 0, main_blk(i, j, nv))),
            pl.BlockSpec((1, d, wide), lambda i, j, te, nv: (layer_base + te[i], 0, main_blk(i, j, nv))),
            pl.BlockSpec((1, wide, d), lambda i, j, te, nv: (layer_base + te[i], main_blk(i, j, nv), 0)),
            pl.BlockSpec((1, d, tail), lambda i, j, te, nv: (layer_base + te[i], 0, tail_blk)),
            pl.BlockSpec((1, d, tail), lambda i, j, te, nv: (layer_base + te[i], 0, tail_blk)),
            pl.BlockSpec((1, tail, d), lambda i, j, te, nv: (layer_base + te[i], tail_blk, 0)),
        ],
        out_specs=pl.BlockSpec((tm, d), lambda i, j, te, nv: (i, 0)),
    )
    return pl.pallas_call(
        functools.partial(_expert_ffn_kernel, n_main=n_main),
        grid_spec=grid_spec,
        out_shape=jax.ShapeDtypeStruct((n_rows, d), F32),
        compiler_params=_params(2),
        name="moe_ffn",
    )(tile_expert, n_valid, xs, wg, wu, wd, wg, wu, wd)


def _combine_kernel(d1_ref, d2_ref, x_ref, r_ref, gt_ref, y_hbm, o_ref, a_ref, b_ref, sems):
    tc = a_ref.shape[0]

    def copies(r):
        return (_row_copy(y_hbm, d1_ref[0, 0, r], a_ref, r, sems.at[0]),
                _row_copy(y_hbm, d2_ref[0, 0, r], b_ref, r, sems.at[1]))

    def issue(r, c):
        for cp in copies(r):
            cp.start()
        return c

    def wait(r, c):
        for cp in copies(r):
            cp.wait()
        return c

    lax.fori_loop(0, tc, issue, 0)
    lax.fori_loop(0, tc, wait, 0)
    route = r_ref[...]
    lane = lax.broadcasted_iota(jnp.int32, route.shape, 1)
    g1 = jnp.sum(jnp.where(lane == ROUTE_G1, route, 0.0), axis=-1, keepdims=True)
    g2 = jnp.sum(jnp.where(lane == ROUTE_G2, route, 0.0), axis=-1, keepdims=True)
    o_ref[...] = x_ref[...] + gt_ref[0] * (g1 * a_ref[...] + g2 * b_ref[...])


def _combine(x, route, gt, y, d1, d2, tc, tiles_per_group):
    m, d = x.shape
    idx_spec = pl.BlockSpec((1, 1, tc), lambda i: (i, 0, 0), memory_space=pltpu.SMEM)
    return pl.pallas_call(
        _combine_kernel,
        grid=(m // tc,),
        in_specs=[idx_spec, idx_spec,
                  pl.BlockSpec((tc, d), lambda i: (i, 0)),
                  pl.BlockSpec((tc, LANES), lambda i: (i, 0)),
                  _mod_spec(gt, tiles_per_group),
                  pl.BlockSpec(memory_space=pl.ANY)],
        out_specs=pl.BlockSpec((tc, d), lambda i: (i, 0)),
        out_shape=jax.ShapeDtypeStruct((m, d), F32),
        scratch_shapes=[pltpu.VMEM((tc, d), F32), pltpu.VMEM((tc, d), F32),
                        pltpu.SemaphoreType.DMA((2,))],
        compiler_params=_params(1),
        name="moe_combine",
    )(d1.reshape(m // tc, 1, tc), d2.reshape(m // tc, 1, tc), x, route, gt, y)


def _moe_routed(h2, x, gt, p, j, tm_router, tm, tc, tiles_per_group):
    d = x.shape[1]
    n_exp, _, f_exp = p["moe_w_gate"].shape[1:]
    route, counts = _router(h2, p["router_w"][j], p["router_b"][j], tm_router)
    d1, d2, src, tile_expert, n_valid = _route_plan(route, counts, tm)
    xs = _gather_rows(h2, src, n_valid, tm)
    y = _expert_ffn(xs, p["moe_w_gate"].reshape(-1, d, f_exp), p["moe_w_up"].reshape(-1, d, f_exp),
                    p["moe_w_down"].reshape(-1, f_exp, d), j * n_exp, tile_expert, n_valid, tm)
    return _combine(x, route, gt, y, d1, d2, tc, tiles_per_group)


def _trunk(x, mods, start, pool_past, attend, p, batch, seq, tm, tm_out, tm_ffn, tm_moe,
           k_stack, v_stack):
    depth = p["w_in"].shape[0]
    m, d = x.shape
    tpg = (m // mods[0][0].shape[0]) // tm
    tpg_out = (m // mods[0][0].shape[0]) // tm_out
    tpg_ffn = (m // mods[0][0].shape[0]) // tm_ffn
    pools = []
    sh_m, sc_m = mods[0][0], mods[0][1]
    h = _norm(x, p["g_norm_mix"], 0, sc_m, sh_m, tm, tpg)
    for l in range(depth):
        sh_m, sc_m, gt_m, sh_f, sc_f, gt_f = mods[l]
        if l > 0:
            h = _norm(x, p["g_norm_mix"], l, sc_m, sh_m, tm, tpg)
        g_q = jnp.tile(p["g_q"][l], 2).reshape(1, LANES)
        g_k = jnp.tile(p["g_k"][l], 2).reshape(1, LANES)
        q = _inproj(h, p["w_in"], l, 0, "q", tm, g128=g_q)
        k_stack, k_bf = _inproj(h, p["w_in"], l, 1, "k", tm, g128=g_k, stack=k_stack)
        v_stack, v_bf = _inproj(h, p["w_in"], l, 2, "v", tm, stack=v_stack)
        u = _inproj(h, p["w_in"], l, 3, "u", tm)
        lam = (jnp.exp(jnp.sum(p["lambda_q1"][l] * p["lambda_k1"][l]))
               - jnp.exp(jnp.sum(p["lambda_q2"][l] * p["lambda_k2"][l]))
               + _lambda_init(l)).astype(F32).reshape(1)
        g_sub = p["g_sub"][l].reshape(1, LANES)
        a = attend(l, q, k_bf, v_bf, k_stack, v_stack, lam, g_sub)
        pool_out, pool_new = pool_past(l, u)
        pools.append(pool_new)
        j = l // 2
        routed = l % 2 == 1 and tm_moe is not None
        x, h2 = _outproj(a, pool_out, x, p["w_o"], l, gt_m, p["g_norm_ffn"], sc_f, sh_f, tm_out,
                         tpg_out, F32 if routed else BF16)
        if l % 2 == 0:
            x = _ffn(h2, x, gt_f, p["dense_w_gate"], p["dense_w_up"], p["dense_w_down"], j,
                     tm_ffn, 256, tpg_ffn)
        elif routed:
            x = _moe_routed(h2, x, gt_f, p, j, tm, tm_moe, tm_out, tpg_out)
        else:
            combine, _ = _router(h2, p["router_w"][j], p["router_b"][j], tm)
            n_exp, _, f_exp = p["moe_w_gate"].shape[1:]
            wg = p["moe_w_gate"].reshape(-1, d, f_exp)
            wu = p["moe_w_up"].reshape(-1, d, f_exp)
            wd = p["moe_w_down"].reshape(-1, f_exp, d)
            for e in range(n_exp):
                x = _ffn(h2, x, gt_f, wg, wu, wd, j * n_exp + e, tm_ffn, 128, tpg_ffn,
                         cw=combine[:, e:e + 1])
    return x, k_stack, v_stack, jnp.stack(pools)


def kernel(x_prompt, x_sample, cache_k, cache_v, state_pool, page_table, c_prompt, c_sample,
           g_norm_mix, g_norm_ffn, w_ada, b_ada, w_in, g_q, g_k,
           lambda_q1, lambda_k1, lambda_q2, lambda_k2, g_sub, w_pool, pool_scale, w_o,
           dense_w_gate, dense_w_up, dense_w_down,
           router_w, router_b, moe_w_gate, moe_w_up, moe_w_down):
    p = dict(g_norm_mix=g_norm_mix.reshape(-1, 1, g_norm_mix.shape[-1]),
             g_norm_ffn=g_norm_ffn.reshape(-1, 1, g_norm_ffn.shape[-1]),
             w_in=w_in, g_q=g_q, g_k=g_k,
             lambda_q1=lambda_q1, lambda_k1=lambda_k1, lambda_q2=lambda_q2, lambda_k2=lambda_k2,
             g_sub=g_sub, w_pool=w_pool, pool_scale=pool_scale, w_o=w_o,
             dense_w_gate=dense_w_gate, dense_w_up=dense_w_up, dense_w_down=dense_w_down,
             router_w=router_w, router_b=router_b,
             moe_w_gate=moe_w_gate, moe_w_up=moe_w_up, moe_w_down=moe_w_down)
    depth = w_in.shape[0]
    bp, tp, d = x_prompt.shape
    bs, ts, _ = x_sample.shape
    width = N_HEADS * LANES
    c_pool = pool_scale.shape[-1]

    n_c = bp + bs
    rows = -(-n_c // 16) * 16
    c_all = jnp.concatenate([c_prompt, c_sample, jnp.zeros((rows - n_c, d), F32)], axis=0)
    mod = _ada(c_all, w_ada, b_ada).reshape(depth, rows, 6, d)
    mods_p = [[mod[l, :bp, i][:, None, :] for i in range(6)] for l in range(depth)]
    mods_s = [[jnp.repeat(mod[l, bp:n_c, i], ts, axis=0)[None] for i in range(6)]
              for l in range(depth)]

    zeros_past = jnp.zeros((bp, POOL_STATE + 1, c_pool), F32)

    def attend_p(l, q, k_bf, v_bf, ks, vs, lam, gs):
        return _attn_prompt(q, k_bf, v_bf, l, lam, gs, bp, tp, 512)

    def pool_p(l, u):
        u3 = u.reshape(bp, tp, c_pool)
        out = _pool(u3, zeros_past, w_pool, pool_scale, l, 0)
        return out.reshape(bp * tp, c_pool), u3[:, tp - POOL_STATE:]

    kp0 = jnp.zeros((depth, bp * tp, N_HEADS, LANES), F32)
    vp0 = jnp.zeros((depth, bp * tp, N_HEADS, LANES), F32)
    y_p, k_p, v_p, pool_p_new = _trunk(x_prompt.reshape(bp * tp, d), mods_p, 0, pool_p, attend_p,
                                       p, bp, tp, 512, 256, 1024, 1024, kp0, vp0)

    past_len = page_table.shape[1] * PAGE_SIZE
    t_pad = 16

    def attend_s(l, q, k_bf, v_bf, ks, vs, lam, gs):
        return _attn_decode(q, ks, vs, cache_k, cache_v, page_table, l, lam, gs, ts, 8)

    def pool_s(l, u):
        u3 = u.reshape(bs, ts, c_pool)
        past16 = jnp.concatenate([jnp.zeros((bs, 1, c_pool), F32), state_pool[l]], axis=1)
        u_pad = jnp.concatenate([u3, jnp.zeros((bs, t_pad - ts, c_pool), F32)], axis=1)
        out = _pool(u_pad, past16, w_pool, pool_scale, l, past_len)
        new = jnp.concatenate([state_pool[l], u3], axis=1)[:, -POOL_STATE:]
        return out[:, :ts].reshape(bs * ts, c_pool), new

    ks0 = jnp.zeros((depth, bs * ts, N_HEADS, LANES), F32)
    vs0 = jnp.zeros((depth, bs * ts, N_HEADS, LANES), F32)
    y_s, k_s, v_s, pool_s_new = _trunk(x_sample.reshape(bs * ts, d), mods_s, past_len, pool_s,
                                       attend_s, p, bs, ts, bs * ts, bs * ts, bs * ts, None,
                                       ks0, vs0)

    return (y_p.reshape(bp, tp, d), y_s.reshape(bs, ts, d),
            k_p.reshape(depth, bp, tp, N_HEADS, LANES), v_p.reshape(depth, bp, tp, N_HEADS, LANES),
            pool_p_new,
            k_s.reshape(depth, bs, ts, N_HEADS, LANES), v_s.reshape(depth, bs, ts, N_HEADS, LANES),
            pool_s_new)
```

```python
import functools
import math

import jax
import jax.numpy as jnp
from jax import lax
from jax.experimental import pallas as pl
from jax.experimental.pallas import tpu as pltpu

F32 = jnp.float32
BF16 = jnp.bfloat16

N_HEADS = 8
HEAD_DIM_V = 128
HEAD_DIM_QK = 64
POOL_WINDOWS = (2, 4, 8, 16)
POOL_STATE = 15
PAGE_SIZE = 128
N_EXPERTS = 8
EPS = 1e-6
NEG_INF = -1e30
LANES = 128
VMEM_LIMIT = 56 * 1024 * 1024
COL_CHUNK = 512
Q_SCALE = HEAD_DIM_QK ** -0.5 * math.log2(math.e)


def _params(n_axes, vmem=VMEM_LIMIT):
    return pltpu.CompilerParams(dimension_semantics=("arbitrary",) * n_axes,
                                vmem_limit_bytes=vmem)


def _lambda_init(layer):
    return 0.8 - 0.6 * math.exp(-0.3 * layer)


def _mod_spec(arr, tiles_per_group):
    _, r, d = arr.shape
    return pl.BlockSpec((1, r, d), lambda m, *_: (m // tiles_per_group, 0, 0))


def _norm_mod(x, g, sc, sh):
    y = x * lax.rsqrt(jnp.mean(x * x, axis=-1, keepdims=True) + EPS)
    return (y * g) * (1.0 + sc) + sh


def _ada_kernel(c_ref, w_ref, b_ref, o_ref):
    c = c_ref[...]
    s = (c * jax.nn.sigmoid(c)).astype(BF16)
    o_ref[0] = jnp.dot(s, w_ref[0].astype(BF16), preferred_element_type=F32) + b_ref[0]


def _ada(c_all, w_ada, b_ada):
    depth, d, n = w_ada.shape
    rows = c_all.shape[0]
    tn = 1024
    return pl.pallas_call(
        _ada_kernel,
        grid=(depth, n // tn),
        in_specs=[pl.BlockSpec((rows, d), lambda l, j: (0, 0)),
                  pl.BlockSpec((1, d, tn), lambda l, j: (l, 0, j)),
                  pl.BlockSpec((1, 1, tn), lambda l, j: (l, 0, j))],
        out_specs=pl.BlockSpec((1, rows, tn), lambda l, j: (l, 0, j)),
        out_shape=jax.ShapeDtypeStruct((depth, rows, n), F32),
        compiler_params=_params(2),
        name="ada_mod",
    )(c_all, w_ada, b_ada.reshape(depth, 1, n))


def _norm_kernel(x_ref, g_ref, sc_ref, sh_ref, h_ref):
    h_ref[...] = _norm_mod(x_ref[...], g_ref[0], sc_ref[0], sh_ref[0]).astype(BF16)


def _norm(x, g_all, layer, sc, sh, tm, tiles_per_group):
    m, d = x.shape
    return pl.pallas_call(
        _norm_kernel,
        grid=(m // tm,),
        in_specs=[pl.BlockSpec((tm, d), lambda i: (i, 0)),
                  pl.BlockSpec((1, 1, d), lambda i: (layer, 0, 0)),
                  _mod_spec(sc, tiles_per_group), _mod_spec(sh, tiles_per_group)],
        out_specs=pl.BlockSpec((tm, d), lambda i: (i, 0)),
        out_shape=jax.ShapeDtypeStruct((m, d), BF16),
        compiler_params=_params(1),
        name="norm_mod",
    )(x, g_all, sc, sh)


def _qk_norm(p, g128, scale):
    lo = lax.broadcasted_iota(jnp.int32, (1, LANES), 1) < HEAD_DIM_QK
    sq = p * p
    s0 = jnp.sum(jnp.where(lo, sq, 0.0), axis=-1, keepdims=True)
    s1 = jnp.sum(jnp.where(lo, 0.0, sq), axis=-1, keepdims=True)
    inv0 = lax.rsqrt(s0 * (1.0 / HEAD_DIM_QK) + EPS)
    inv1 = lax.rsqrt(s1 * (1.0 / HEAD_DIM_QK) + EPS)
    y = (p * jnp.where(lo, inv0, inv1)) * g128
    return y if scale is None else y * scale


def _inproj_kernel(*refs, mode):
    normed = mode in ("q", "k")
    stacked = mode in ("k", "v")
    h_ref, w_ref = refs[:2]
    g_ref = refs[2] if normed else None
    wbf_ref = refs[-1]
    outs = refs[-3:-1] if stacked else refs[-2:-1]

    @pl.when(pl.program_id(0) == 0)
    def _():
        wbf_ref[...] = w_ref[0].astype(BF16)

    h = h_ref[...]
    scale = Q_SCALE if mode == "q" else None
    for c0 in range(0, wbf_ref.shape[1], COL_CHUNK):
        acc = jnp.dot(h, wbf_ref[:, c0:c0 + COL_CHUNK], preferred_element_type=F32)
        if mode == "u":
            outs[0][:, c0:c0 + COL_CHUNK] = acc
            continue
        for hh in range(COL_CHUNK // LANES):
            head = c0 // LANES + hh
            y = acc[:, hh * LANES:(hh + 1) * LANES]
            if normed:
                y = _qk_norm(y, g_ref[...], scale)
            if stacked:
                outs[0][0, pl.ds(head, h.shape[0], stride=N_HEADS), :] = y
                outs[1][:, head * LANES:(head + 1) * LANES] = y.astype(BF16)
            else:
                outs[0][:, head * LANES:(head + 1) * LANES] = y.astype(BF16)


def _inproj(h, w_in, layer, section, mode, tm, g128=None, stack=None):
    m, d = h.shape
    width = N_HEADS * LANES
    ins = [h, w_in]
    in_specs = [pl.BlockSpec((tm, d), lambda i: (i, 0)),
                pl.BlockSpec((1, d, width), lambda i: (layer, 0, section))]
    if g128 is not None:
        ins.append(g128)
        in_specs.append(pl.BlockSpec((1, LANES), lambda i: (0, 0)))
    flat_spec = pl.BlockSpec((tm, width), lambda i: (i, 0))
    aliases = {}
    if stack is not None:
        aliases = {len(ins): 0}
        ins.append(stack)
        in_specs.append(pl.BlockSpec(memory_space=pl.ANY))
        out_specs = [pl.BlockSpec((1, tm * N_HEADS, LANES), lambda i: (layer, i, 0)), flat_spec]
        out_shape = [jax.ShapeDtypeStruct(stack.shape, stack.dtype),
                     jax.ShapeDtypeStruct((m, width), BF16)]
    else:
        out_specs = [flat_spec]
        out_shape = [jax.ShapeDtypeStruct((m, width), BF16 if mode == "q" else F32)]
    n_in = len(ins)

    def body(*refs):
        if stack is not None:
            refs = refs[:n_in - 1] + refs[n_in:]
        _inproj_kernel(*refs, mode=mode)

    out = pl.pallas_call(
        body,
        grid=(m // tm,),
        in_specs=in_specs,
        out_specs=out_specs,
        out_shape=out_shape,
        scratch_shapes=[pltpu.VMEM((d, width), BF16)],
        input_output_aliases=aliases,
        compiler_params=_params(1),
        name="inproj_" + mode,
    )(*ins)
    return out if stack is not None else out[0]


def _sub_norm(o, g_sub, li):
    y = o * lax.rsqrt(jnp.mean(o * o, axis=-1, keepdims=True) + EPS)
    return (y * g_sub) * (1.0 - li)


def _attn_kernel(lam_ref, q_ref, k_ref, v_ref, g_ref, o_ref, vt_ref, s_ref, m_ref, l_ref, acc_ref,
                 *, tq, li):
    n_t = vt_ref.shape[0]
    for jt in range(n_t):
        vt_ref[jt] = v_ref[0, jt * tq:(jt + 1) * tq, :].astype(F32).T.astype(BF16)
    lo = lax.broadcasted_iota(jnp.int32, (1, LANES), 1) < HEAD_DIM_QK
    pairs = [(i, j) for i in range(n_t) for j in range(i + 1)]

    def logits(n):
        i, j = pairs[n]
        q = q_ref[0, i * tq:(i + 1) * tq, :]
        zero = jnp.zeros_like(q)
        kt = k_ref[0, j * tq:(j + 1) * tq, :]
        for mp, qm in enumerate((jnp.where(lo, q, zero), jnp.where(lo, zero, q))):
            s_ref[n % 2, mp] = lax.dot_general(kt, qm, (((1,), (1,)), ((), ())),
                                               preferred_element_type=F32)

    def consume(n):
        i, j = pairs[n]
        vt = vt_ref[j]
        for mp in range(2):
            s = s_ref[n % 2, mp]
            if j == i:
                key = lax.broadcasted_iota(jnp.int32, (tq, tq), 0)
                qry = lax.broadcasted_iota(jnp.int32, (tq, tq), 1)
                s = jnp.where(key <= qry, s, NEG_INF)
            m_old = m_ref[mp]
            m_new = jnp.maximum(m_old, jnp.max(s, axis=0, keepdims=True))
            alpha = jnp.exp2(m_old - m_new)
            p = jnp.exp2(s - m_new)
            l_ref[mp] = alpha * l_ref[mp] + jnp.sum(p, axis=0, keepdims=True)
            acc_ref[mp] = alpha * acc_ref[mp] + jnp.dot(vt, p.astype(BF16),
                                                        preferred_element_type=F32)
            m_ref[mp] = m_new

    logits(0)
    for n, (i, j) in enumerate(pairs):
        if j == 0:
            m_ref[...] = jnp.full(m_ref.shape, NEG_INF, F32)
            l_ref[...] = jnp.zeros(l_ref.shape, F32)
            acc_ref[...] = jnp.zeros(acc_ref.shape, F32)
        if n + 1 < len(pairs):
            logits(n + 1)
        consume(n)
        if j == i:
            o = acc_ref[0] / l_ref[0] - lam_ref[0] * (acc_ref[1] / l_ref[1])
            y = o * lax.rsqrt(jnp.mean(o * o, axis=0, keepdims=True) + EPS)
            o_ref[0, i * tq:(i + 1) * tq, :] = ((y.T * g_ref[...]) * (1.0 - li)).astype(BF16)


def _attn_prompt(q, k, v, layer, lam, g_sub, batch, seq, tq):
    width = N_HEADS * LANES
    q3, k3, v3 = (t.reshape(batch, seq, width) for t in (q, k, v))
    head_spec = pl.BlockSpec((1, seq, LANES), lambda b, h: (b, 0, h))
    out = pl.pallas_call(
        functools.partial(_attn_kernel, tq=tq, li=_lambda_init(layer)),
        grid=(batch, N_HEADS),
        in_specs=[pl.BlockSpec(memory_space=pltpu.SMEM),
                  head_spec, head_spec, head_spec,
                  pl.BlockSpec((1, LANES), lambda b, h: (0, 0))],
        out_specs=head_spec,
        out_shape=jax.ShapeDtypeStruct((batch, seq, width), BF16),
        scratch_shapes=[pltpu.VMEM((seq // tq, LANES, tq), BF16),
                        pltpu.VMEM((2, 2, tq, tq), F32),
                        pltpu.VMEM((2, 1, tq), F32), pltpu.VMEM((2, 1, tq), F32),
                        pltpu.VMEM((2, LANES, tq), F32)],
        compiler_params=_params(2),
        name="attn_prompt",
    )(lam, q3, k3, v3, g_sub)
    return out.reshape(batch * seq, width)


def _decode_kernel(pt_ref, lam_ref, q_ref, *refs, pages, n_new, li):
    k_refs = refs[:pages]
    v_refs = refs[pages:2 * pages]
    kn_ref, vn_ref, g_ref, o_ref, knp_ref, vnp_ref, m_ref, l_ref, acc_ref = refs[2 * pages:]
    b = pl.program_id(0)
    c = pl.program_id(1)
    last = pl.num_programs(1) - 1
    shift_q = (2 * n_new).bit_length() - 1

    @pl.when(jnp.logical_and(b == 0, c == 0))
    def _():
        knp_ref[...] = jnp.zeros(knp_ref.shape, F32)
        vnp_ref[...] = jnp.zeros(vnp_ref.shape, F32)

    @pl.when(c == 0)
    def _():
        m_ref[...] = jnp.full(m_ref.shape, NEG_INF, F32)
        l_ref[...] = jnp.zeros(l_ref.shape, F32)
        acc_ref[...] = jnp.zeros(acc_ref.shape, F32)

    q = q_ref[0]
    n_q = q.shape[0]
    page_rows = PAGE_SIZE * N_HEADS
    def own_head(n_cols):
        row = lax.broadcasted_iota(jnp.int32, (n_q, n_cols), 0)
        col = lax.broadcasted_iota(jnp.int32, (n_q, n_cols), 1)
        return row, col, jnp.bitwise_and(col, N_HEADS - 1) == lax.shift_right_logical(row, shift_q)

    def logits(k2d, keep):
        s = lax.dot_general(q, k2d.astype(BF16), (((1,), (1,)), ((), ())),
                            preferred_element_type=F32)
        return jnp.where(keep, s, NEG_INF)

    def update(s_list, v_list):
        m_old = m_ref[...]
        m_new = m_old
        for s in s_list:
            m_new = jnp.maximum(m_new, jnp.max(s, axis=-1, keepdims=True))
        alpha = jnp.exp2(m_old - m_new)
        l_new = alpha * l_ref[...]
        pv = None
        for s, v in zip(s_list, v_list):
            p = jnp.exp2(s - m_new)
            l_new = l_new + jnp.sum(p, axis=-1, keepdims=True)
            d = jnp.dot(p.astype(BF16), v, preferred_element_type=F32)
            pv = d if pv is None else pv + d
        acc_ref[...] = alpha * acc_ref[...] + pv
        l_ref[...] = l_new
        m_ref[...] = m_new

    s_list, v_list = [], []
    page_keep = own_head(page_rows)[2]
    for j in range(pages):
        s_list.append(logits(k_refs[j][0, 0].reshape(page_rows, LANES), page_keep))
        v_list.append(v_refs[j][0, 0].reshape(page_rows, LANES).astype(BF16))
    update(s_list, v_list)

    @pl.when(c == last)
    def _():
        n_rows = n_new * N_HEADS
        knp_ref[0:n_rows, :] = kn_ref[0, 0]
        vnp_ref[0:n_rows, :] = vn_ref[0, 0]
        row, col, keep = own_head(knp_ref.shape[0])
        tok = jnp.bitwise_and(row, n_new - 1)
        slot = lax.shift_right_logical(col, N_HEADS.bit_length() - 1)
        keep = jnp.logical_and(keep, slot <= tok)
        update([logits(knp_ref[...], keep)], [vnp_ref[...].astype(BF16)])

        o = acc_ref[...] / l_ref[...]
        o = o - lam_ref[0] * pltpu.roll(o, shift=n_q - n_new, axis=0)
        o_ref[0] = _sub_norm(o, g_ref[...], li)


def _attn_decode(q, k_stack, v_stack, cache_k, cache_v, page_table, layer, lam, g_sub,
                 n_new, pages):
    depth = cache_k.shape[0]
    batch, n_pages = page_table.shape
    assert n_new & (n_new - 1) == 0 and N_HEADS & (N_HEADS - 1) == 0
    n_q = N_HEADS * 2 * n_new
    qh = q.reshape(batch, n_new, N_HEADS, 1, LANES).transpose(0, 2, 3, 1, 4)
    in_map = (jnp.arange(LANES) // HEAD_DIM_QK)[None, :] == jnp.arange(2)[:, None]
    qm = jnp.where(in_map[None, None, :, None, :], qh, jnp.zeros_like(qh)).reshape(batch, n_q, LANES)
    kn = k_stack.reshape(depth, batch, n_new * N_HEADS, LANES)
    vn = v_stack.reshape(depth, batch, n_new * N_HEADS, LANES)

    def page_spec(j):
        return pl.BlockSpec((1, 1, PAGE_SIZE, N_HEADS, LANES),
                            lambda b, c, pt: (layer, pt[b, c * pages + j], 0, 0, 0))

    new_spec = pl.BlockSpec((1, 1, n_new * N_HEADS, LANES), lambda b, c, pt: (layer, b, 0, 0))
    grid_spec = pltpu.PrefetchScalarGridSpec(
        num_scalar_prefetch=1,
        grid=(batch, n_pages // pages),
        in_specs=([pl.BlockSpec(memory_space=pltpu.SMEM),
                   pl.BlockSpec((1, n_q, LANES), lambda b, c, pt: (b, 0, 0))]
                  + [page_spec(j) for j in range(pages)]
                  + [page_spec(j) for j in range(pages)]
                  + [new_spec, new_spec,
                     pl.BlockSpec((1, LANES), lambda b, c, pt: (0, 0))]),
        out_specs=pl.BlockSpec((1, n_q, LANES), lambda b, c, pt: (b, 0, 0)),
        scratch_shapes=[pltpu.VMEM((LANES, LANES), F32),
                        pltpu.VMEM((LANES, LANES), F32),
                        pltpu.VMEM((n_q, 1), F32),
                        pltpu.VMEM((n_q, 1), F32),
                        pltpu.VMEM((n_q, LANES), F32)],
    )
    out = pl.pallas_call(
        functools.partial(_decode_kernel, pages=pages, n_new=n_new, li=_lambda_init(layer)),
        grid_spec=grid_spec,
        out_shape=jax.ShapeDtypeStruct((batch, n_q, LANES), F32),
        compiler_params=_params(2),
        name="attn_decode",
    )(page_table, lam, qm, *([cache_k] * pages), *([cache_v] * pages), kn, vn, g_sub)
    out = out.reshape(batch, N_HEADS, 2, n_new, LANES)[:, :, 0]
    return out.transpose(0, 2, 1, 3).reshape(batch * n_new, N_HEADS * LANES).astype(BF16)


def _pool_kernel(u_ref, past_ref, w_ref, sc_ref, o_ref, full_ref, *, start, chunk):
    halo = past_ref.shape[1]
    t = u_ref.shape[1]
    group = u_ref.shape[2] // len(POOL_WINDOWS)
    full_ref[0:halo, :] = past_ref[0]
    full_ref[halo:halo + t, :] = u_ref[0]
    for c0 in range(0, t, chunk):
        pos = start + c0 + lax.broadcasted_iota(jnp.int32, (chunk, 1), 0)
        for g, w in enumerate(POOL_WINDOWS):
            cols = slice(g * group, (g + 1) * group)
            cur = full_ref[halo + c0:halo + c0 + chunk, cols]
            win = cur
            for i in range(1, w):
                win = win + full_ref[halo + c0 - i:halo + c0 - i + chunk, cols]
            count = jnp.minimum(w, pos + 1).astype(F32)
            pooled = (win / count - cur).astype(BF16)
            y = jnp.dot(pooled, w_ref[0, g].astype(BF16), preferred_element_type=F32)
            o_ref[0, c0:c0 + chunk, cols] = (y * sc_ref[0, :, cols]).astype(BF16)


def _pool(u3, past16, w_pool, pool_scale, layer, start):
    batch, t, c = u3.shape
    depth = w_pool.shape[0]
    group = c // len(POOL_WINDOWS)
    chunk = min(t, 256)
    return pl.pallas_call(
        functools.partial(_pool_kernel, start=start, chunk=chunk),
        grid=(batch,),
        in_specs=[pl.BlockSpec((1, t, c), lambda b: (b, 0, 0)),
                  pl.BlockSpec((1, POOL_STATE + 1, c), lambda b: (b, 0, 0)),
                  pl.BlockSpec((1, len(POOL_WINDOWS), group, group), lambda b: (layer, 0, 0, 0)),
                  pl.BlockSpec((1, 1, c), lambda b: (layer, 0, 0))],
        out_specs=pl.BlockSpec((1, t, c), lambda b: (b, 0, 0)),
        out_shape=jax.ShapeDtypeStruct((batch, t, c), BF16),
        scratch_shapes=[pltpu.VMEM((POOL_STATE + 1 + t, c), F32)],
        compiler_params=_params(1),
        name="pool_mix",
    )(u3, past16, w_pool, pool_scale.reshape(depth, 1, c))


def _outproj_kernel(a_ref, p_ref, x_ref, w_ref, gt_ref, g_ref, sc_ref, sh_ref,
                    x1_ref, h2_ref, h2_rows_ref, wbf_ref):
    @pl.when(pl.program_id(0) == 0)
    def _():
        wbf_ref[...] = w_ref[0].astype(BF16)

    half = a_ref.shape[1]
    a = a_ref[...]
    p = p_ref[...]
    for c0 in range(0, x_ref.shape[1], COL_CHUNK):
        cols = slice(c0, c0 + COL_CHUNK)
        y = jnp.dot(a, wbf_ref[0:half, cols], preferred_element_type=F32)
        y = y + jnp.dot(p, wbf_ref[half:, cols], preferred_element_type=F32)
        x1_ref[:, cols] = x_ref[:, cols] + gt_ref[0, :, cols] * y
    h2 = _norm_mod(x1_ref[...], g_ref[0], sc_ref[0], sh_ref[0])
    h2_ref[...] = h2.astype(BF16)
    if h2_rows_ref is not None:
        tm, chunks = h2.shape[0], h2.shape[1] // LANES
        for c in range(chunks):
            h2_rows_ref[pl.ds(c, tm, stride=chunks), :] = h2[:, c * LANES:(c + 1) * LANES]


def _outproj(a, p, x, w_o, layer, gt, g_ffn, sc, sh, tm, tiles_per_group, row_major):
    m, d = x.shape
    half = a.shape[1]
    out_specs = [pl.BlockSpec((tm, d), lambda i: (i, 0)), pl.BlockSpec((tm, d), lambda i: (i, 0))]
    out_shape = [jax.ShapeDtypeStruct((m, d), F32), jax.ShapeDtypeStruct((m, d), BF16)]
    chunks = d // LANES
    if row_major:
        out_specs.append(pl.BlockSpec((tm * chunks, LANES), lambda i: (i, 0)))
        out_shape.append(jax.ShapeDtypeStruct((m * chunks, LANES), F32))

    def body(*refs):
        if row_major:
            _outproj_kernel(*refs)
        else:
            _outproj_kernel(*refs[:-1], None, refs[-1])

    outs = pl.pallas_call(
        body,
        grid=(m // tm,),
        in_specs=[pl.BlockSpec((tm, half), lambda i: (i, 0)),
                  pl.BlockSpec((tm, half), lambda i: (i, 0)),
                  pl.BlockSpec((tm, d), lambda i: (i, 0)),
                  pl.BlockSpec((1, d, d), lambda i: (layer, 0, 0),
                               pipeline_mode=pl.Buffered(1)),
                  _mod_spec(gt, tiles_per_group),
                  pl.BlockSpec((1, 1, d), lambda i: (layer, 0, 0)),
                  _mod_spec(sc, tiles_per_group), _mod_spec(sh, tiles_per_group)],
        out_specs=out_specs,
        out_shape=out_shape,
        scratch_shapes=[pltpu.VMEM((d, d), BF16)],
        compiler_params=_params(1),
        name="outproj",
    )(a, p, x, w_o, gt, g_ffn, sc, sh)
    if row_major:
        return outs[0], outs[1], outs[2].reshape(m, chunks, LANES)
    return outs


def _ffn_kernel(*refs, row_weight):
    if row_weight:
        h_ref, x_ref, gt_ref, wg_ref, wu_ref, wd_ref, cw_ref, o_ref = refs
    else:
        h_ref, x_ref, gt_ref, wg_ref, wu_ref, wd_ref, o_ref = refs

    @pl.when(pl.program_id(1) == 0)
    def _():
        o_ref[...] = x_ref[...]

    h = h_ref[...]
    g = jnp.dot(h, wg_ref[0].astype(BF16), preferred_element_type=F32)
    u = jnp.dot(h, wu_ref[0].astype(BF16), preferred_element_type=F32)
    act = ((g * jax.nn.sigmoid(g)) * u).astype(BF16)
    for c0 in range(0, o_ref.shape[1], COL_CHUNK):
        cols = slice(c0, c0 + COL_CHUNK)
        y = jnp.dot(act, wd_ref[0, :, cols].astype(BF16), preferred_element_type=F32)
        scale = gt_ref[0, :, cols]
        if row_weight:
            scale = scale * cw_ref[...]
        o_ref[:, cols] += scale * y


def _ffn(h, x, gt, wg, wu, wd, widx, tm, tf, tiles_per_group, cw=None):
    m, d = x.shape
    f = wg.shape[2]
    ins = [h, x, gt, wg, wu, wd]
    in_specs = [pl.BlockSpec((tm, d), lambda i, j: (i, 0)),
                pl.BlockSpec((tm, d), lambda i, j: (i, 0), pipeline_mode=pl.Buffered(1)),
                _mod_spec(gt, tiles_per_group),
                pl.BlockSpec((1, d, tf), lambda i, j: (widx, 0, j)),
                pl.BlockSpec((1, d, tf), lambda i, j: (widx, 0, j)),
                pl.BlockSpec((1, tf, d), lambda i, j: (widx, j, 0))]
    if cw is not None:
        ins.append(cw)
        in_specs.append(pl.BlockSpec((tm, 1), lambda i, j: (i, 0)))
    return pl.pallas_call(
        functools.partial(_ffn_kernel, row_weight=cw is not None),
        grid=(m // tm, f // tf),
        in_specs=in_specs,
        out_specs=pl.BlockSpec((tm, d), lambda i, j: (i, 0)),
        out_shape=jax.ShapeDtypeStruct((m, d), F32),
        compiler_params=_params(2),
        name="ffn",
    )(*ins)


ROUTE_E1, ROUTE_E2, ROUTE_POS1, ROUTE_POS2, ROUTE_G1, ROUTE_G2 = range(N_EXPERTS, N_EXPERTS + 6)


def _router_kernel(h_ref, whi_ref, wlo_ref, b_ref, o_ref, cnt_ref, carry_ref):
    @pl.when(pl.program_id(0) == 0)
    def _():
        carry_ref[...] = jnp.zeros(carry_ref.shape, F32)

    h = h_ref[...].astype(BF16)
    logits = (jnp.dot(h, whi_ref[...], preferred_element_type=F32)
              + jnp.dot(h, wlo_ref[...], preferred_element_type=F32) + b_ref[...])
    lane = lax.broadcasted_iota(jnp.int32, logits.shape, 1).astype(F32)
    logits = jnp.where(lane < N_EXPERTS, logits, NEG_INF)
    v1 = jnp.max(logits, axis=-1, keepdims=True)
    i1 = jnp.min(jnp.where(logits == v1, lane, float(LANES)), axis=-1, keepdims=True)
    rest = jnp.where(lane == i1, NEG_INF, logits)
    v2 = jnp.max(rest, axis=-1, keepdims=True)
    i2 = jnp.min(jnp.where(rest == v2, lane, float(LANES)), axis=-1, keepdims=True)
    e2 = jnp.exp(v2 - v1)
    g1 = 1.0 / (1.0 + e2)
    g2 = e2 / (1.0 + e2)

    tm = h.shape[0]
    sel = jnp.where(jnp.logical_or(lane == i1, lane == i2), 1.0, 0.0)
    earlier = (lax.broadcasted_iota(jnp.int32, (tm, tm), 1)
               < lax.broadcasted_iota(jnp.int32, (tm, tm), 0))
    rank = jnp.dot(jnp.where(earlier, 1.0, 0.0).astype(BF16), sel.astype(BF16),
                   preferred_element_type=F32) + carry_ref[...]
    pos1 = jnp.sum(jnp.where(lane == i1, rank, 0.0), axis=-1, keepdims=True)
    pos2 = jnp.sum(jnp.where(lane == i2, rank, 0.0), axis=-1, keepdims=True)
    total = carry_ref[...] + jnp.sum(sel, axis=0, keepdims=True)
    carry_ref[...] = total
    cnt_ref[...] = total

    out = jnp.where(lane == i1, g1, 0.0) + jnp.where(lane == i2, g2, 0.0)
    for idx, val in ((ROUTE_E1, i1), (ROUTE_E2, i2), (ROUTE_POS1, pos1), (ROUTE_POS2, pos2),
                     (ROUTE_G1, g1), (ROUTE_G2, g2)):
        out = out + jnp.where(lane == float(idx), val, 0.0)
    o_ref[...] = out


def _router(h, router_w, router_b, tm):
    m, d = h.shape
    pad = LANES - N_EXPERTS
    w = jnp.pad(router_w, ((0, 0), (0, pad)))
    whi = w.astype(BF16)
    wlo = (w - whi.astype(F32)).astype(BF16)
    b = jnp.pad(router_b, (0, pad)).reshape(1, LANES)
    return pl.pallas_call(
        _router_kernel,
        grid=(m // tm,),
        in_specs=[pl.BlockSpec((tm, d), lambda i: (i, 0)),
                  pl.BlockSpec((d, LANES), lambda i: (0, 0)),
                  pl.BlockSpec((d, LANES), lambda i: (0, 0)),
                  pl.BlockSpec((1, LANES), lambda i: (0, 0))],
        out_specs=[pl.BlockSpec((tm, LANES), lambda i: (i, 0)),
                   pl.BlockSpec((1, LANES), lambda i: (0, 0))],
        out_shape=[jax.ShapeDtypeStruct((m, LANES), F32),
                   jax.ShapeDtypeStruct((1, LANES), F32)],
        scratch_shapes=[pltpu.VMEM((1, LANES), F32)],
        compiler_params=_params(1),
        name="router",
    )(h, whi, wlo, b)


def _route_plan(route, counts, tm):
    m = route.shape[0]
    n_tiles = 2 * m // tm + N_EXPERTS
    col = lambda i: route[:, i].astype(jnp.int32)
    cnt = counts[0, :N_EXPERTS].astype(jnp.int32)
    tiles = (cnt + tm - 1) // tm
    tile_end = jnp.cumsum(tiles)
    row_start = (tile_end - tiles) * tm
    d1 = row_start[col(ROUTE_E1)] + col(ROUTE_POS1)
    d2 = row_start[col(ROUTE_E2)] + col(ROUTE_POS2)
    token = jnp.arange(m, dtype=jnp.int32)
    src = jnp.zeros((n_tiles * tm,), jnp.int32).at[d1].set(token).at[d2].set(token)
    tile_id = jnp.arange(n_tiles, dtype=jnp.int32)
    tile_expert = jnp.minimum(jnp.searchsorted(tile_end, tile_id, side="right"),
                              N_EXPERTS - 1).astype(jnp.int32)
    tile_rows = jnp.clip(cnt[tile_expert] - (tile_id - (tile_end - tiles)[tile_expert]) * tm, 0, tm)
    tile_rows = jnp.where(tile_id < tile_end[-1], tile_rows, 0).astype(jnp.int32)
    return d1, d2, src, tile_expert, tile_rows, tile_end[-1:].astype(jnp.int32)


def _row_copy(src_hbm, row, dst_vmem, slot, sem):
    return pltpu.make_async_copy(src_hbm.at[pl.ds(row, 1), :], dst_vmem.at[pl.ds(slot, 1), :], sem)


def _slab_copy(src_hbm, row, dst_vmem, slot, sem):
    chunks = src_hbm.shape[1]
    return pltpu.make_async_copy(
        src_hbm.at[row], dst_vmem.at[pl.ds(pl.multiple_of(slot * chunks, chunks), chunks), :], sem)


def _gather_kernel(rows_ref, src_ref, h_hbm, o_ref, buf_ref, sem):
    i = pl.program_id(0)
    tm = o_ref.shape[0]
    chunks = h_hbm.shape[1]

    @pl.when(i == 0)
    def _():
        buf_ref[...] = jnp.zeros(buf_ref.shape, F32)

    def issue(r, c):
        _slab_copy(h_hbm, src_ref[0, 0, r], buf_ref, r, sem).start()
        return c

    def wait(r, c):
        _slab_copy(h_hbm, src_ref[0, 0, r], buf_ref, r, sem).wait()
        return c

    lax.fori_loop(0, rows_ref[i], issue, 0)
    lax.fori_loop(0, rows_ref[i], wait, 0)
    for c in range(chunks):
        o_ref[:, c * LANES:(c + 1) * LANES] = buf_ref[pl.ds(c, tm, stride=chunks), :].astype(BF16)


def _gather_rows(h_rows, src, tile_rows, tm):
    _, chunks, _ = h_rows.shape
    n_rows = src.shape[0]
    grid_spec = pltpu.PrefetchScalarGridSpec(
        num_scalar_prefetch=1,
        grid=(n_rows // tm,),
        in_specs=[pl.BlockSpec((1, 1, tm), lambda i, nr: (i, 0, 0), memory_space=pltpu.SMEM),
                  pl.BlockSpec(memory_space=pl.ANY)],
        out_specs=pl.BlockSpec((tm, chunks * LANES), lambda i, nr: (i, 0)),
        scratch_shapes=[pltpu.VMEM((tm * chunks, LANES), F32), pltpu.SemaphoreType.DMA(())],
    )
    return pl.pallas_call(
        _gather_kernel,
        grid_spec=grid_spec,
        out_shape=jax.ShapeDtypeStruct((n_rows, chunks * LANES), BF16),
        compiler_params=_params(1),
        name="moe_gather",
    )(tile_rows, src.reshape(n_rows // tm, 1, tm), h_rows)


def _expert_ffn_kernel(te_ref, nv_ref, h_ref, wg_ref, wu_ref, wd_ref, wgt_ref, wut_ref, wdt_ref,
                       o_ref, *, n_main):
    i = pl.program_id(0)
    j = pl.program_id(1)

    @pl.when(j == 0)
    def _():
        o_ref[...] = jnp.zeros(o_ref.shape, F32)

    def accumulate(wg, wu, wd):
        h = h_ref[...]
        g = jnp.dot(h, wg[0].astype(BF16), preferred_element_type=F32)
        u = jnp.dot(h, wu[0].astype(BF16), preferred_element_type=F32)
        act = ((g * jax.nn.sigmoid(g)) * u).astype(BF16)
        for c0 in range(0, o_ref.shape[1], COL_CHUNK):
            cols = slice(c0, c0 + COL_CHUNK)
            o_ref[:, cols] += jnp.dot(act, wd[0, :, cols].astype(BF16),
                                      preferred_element_type=F32)

    in_use = i < nv_ref[0]

    @pl.when(jnp.logical_and(in_use, j < n_main))
    def _():
        accumulate(wg_ref, wu_ref, wd_ref)

    @pl.when(jnp.logical_and(in_use, j == n_main))
    def _():
        accumulate(wgt_ref, wut_ref, wdt_ref)


def _expert_ffn(xs, wg, wu, wd, layer_base, tile_expert, n_valid, tm):
    n_rows, d = xs.shape
    f = wg.shape[2]
    wide, tail = 2 * LANES, LANES
    n_main = f // wide
    assert f == n_main * wide + tail
    tail_blk = f // tail - 1

    def main_blk(i, j, nv):
        return jnp.where(i < nv[0], jnp.minimum(j, n_main - 1), 0)

    grid_spec = pltpu.PrefetchScalarGridSpec(
        num_scalar_prefetch=2,
        grid=(n_rows // tm, n_main + 1),
        in_specs=[
            pl.BlockSpec((tm, d), lambda i, j, te, nv: (i, 0)),
            pl.BlockSpec((1, d, wide), lambda i, j, te, nv: (layer_base + te[i], 0, main_blk(i, j, nv))),
            pl.BlockSpec((1, d, wide), lambda i, j, te, nv: (layer_base + te[i], 0, main_blk(i, j, nv))),
            pl.BlockSpec((1, wide, d), lambda i, j, te, nv: (layer_base + te[i], main_blk(i, j, nv), 0)),
            pl.BlockSpec((1, d, tail), lambda i, j, te, nv: (layer_base + te[i], 0, tail_blk)),
            pl.BlockSpec((1, d, tail), lambda i, j, te, nv: (layer_base + te[i], 0, tail_blk)),
            pl.BlockSpec((1, tail, d), lambda i, j, te, nv: (layer_base + te[i], tail_blk, 0)),
        ],
        out_specs=pl.BlockSpec((tm, d), lambda i, j, te, nv: (i, 0)),
    )
    return pl.pallas_call(
        functools.partial(_expert_ffn_kernel, n_main=n_main),
        grid_spec=grid_spec,
        out_shape=jax.ShapeDtypeStruct((n_rows, d), F32),
        compiler_params=_params(2),
        name="moe_ffn",
    )(tile_expert, n_valid, xs, wg, wu, wd, wg, wu, wd)


def _combine_kernel(d1_ref, d2_ref, x_ref, r_ref, gt_ref, y_hbm, o_ref, a_ref, b_ref, sems):
    tc = a_ref.shape[0]

    def copies(r):
        return (_row_copy(y_hbm, d1_ref[0, 0, r], a_ref, r, sems.at[0]),
                _row_copy(y_hbm, d2_ref[0, 0, r], b_ref, r, sems.at[1]))

    def issue(r, c):
        for cp in copies(r):
            cp.start()
        return c

    def wait(r, c):
        for cp in copies(r):
            cp.wait()
        return c

    lax.fori_loop(0, tc, issue, 0)
    lax.fori_loop(0, tc, wait, 0)
    route = r_ref[...]
    lane = lax.broadcasted_iota(jnp.int32, route.shape, 1)
    g1 = jnp.sum(jnp.where(lane == ROUTE_G1, route, 0.0), axis=-1, keepdims=True)
    g2 = jnp.sum(jnp.where(lane == ROUTE_G2, route, 0.0), axis=-1, keepdims=True)
    o_ref[...] = x_ref[...] + gt_ref[0] * (g1 * a_ref[...] + g2 * b_ref[...])


def _combine(x, route, gt, y, d1, d2, tc, tiles_per_group):
    m, d = x.shape
    idx_spec = pl.BlockSpec((1, 1, tc), lambda i: (i, 0, 0), memory_space=pltpu.SMEM)
    return pl.pallas_call(
        _combine_kernel,
        grid=(m // tc,),
        in_specs=[idx_spec, idx_spec,
                  pl.BlockSpec((tc, d), lambda i: (i, 0)),
                  pl.BlockSpec((tc, LANES), lambda i: (i, 0)),
                  _mod_spec(gt, tiles_per_group),
                  pl.BlockSpec(memory_space=pl.ANY)],
        out_specs=pl.BlockSpec((tc, d), lambda i: (i, 0)),
        out_shape=jax.ShapeDtypeStruct((m, d), F32),
        scratch_shapes=[pltpu.VMEM((tc, d), F32), pltpu.VMEM((tc, d), F32),
                        pltpu.SemaphoreType.DMA((2,))],
        compiler_params=_params(1),
        name="moe_combine",
    )(d1.reshape(m // tc, 1, tc), d2.reshape(m // tc, 1, tc), x, route, gt, y)


def _moe_routed(h2, h2_rows, x, gt, p, j, tm_router, tm, tc, tiles_per_group):
    d = x.shape[1]
    n_exp, _, f_exp = p["moe_w_gate"].shape[1:]
    route, counts = _router(h2, p["router_w"][j], p["router_b"][j], tm_router)
    d1, d2, src, tile_expert, tile_rows, n_valid = _route_plan(route, counts, tm)
    xs = _gather_rows(h2_rows, src, tile_rows, tm)
    y = _expert_ffn(xs, p["moe_w_gate"].reshape(-1, d, f_exp), p["moe_w_up"].reshape(-1, d, f_exp),
                    p["moe_w_down"].reshape(-1, f_exp, d), j * n_exp, tile_expert, n_valid, tm)
    return _combine(x, route, gt, y, d1, d2, tc, tiles_per_group)


def _trunk(x, mods, start, pool_past, attend, p, batch, seq, tm, tm_out, tm_ffn, tm_moe,
           k_stack, v_stack):
    depth = p["w_in"].shape[0]
    m, d = x.shape
    tpg = (m // mods[0][0].shape[0]) // tm
    tpg_out = (m // mods[0][0].shape[0]) // tm_out
    tpg_ffn = (m // mods[0][0].shape[0]) // tm_ffn
    pools = []
    sh_m, sc_m = mods[0][0], mods[0][1]
    h = _norm(x, p["g_norm_mix"], 0, sc_m, sh_m, tm, tpg)
    for l in range(depth):
        sh_m, sc_m, gt_m, sh_f, sc_f, gt_f = mods[l]
        if l > 0:
            h = _norm(x, p["g_norm_mix"], l, sc_m, sh_m, tm, tpg)
        g_q = jnp.tile(p["g_q"][l], 2).reshape(1, LANES)
        g_k = jnp.tile(p["g_k"][l], 2).reshape(1, LANES)
        q = _inproj(h, p["w_in"], l, 0, "q", tm, g128=g_q)
        k_stack, k_bf = _inproj(h, p["w_in"], l, 1, "k", tm, g128=g_k, stack=k_stack)
        v_stack, v_bf = _inproj(h, p["w_in"], l, 2, "v", tm, stack=v_stack)
        u = _inproj(h, p["w_in"], l, 3, "u", tm)
        lam = (jnp.exp(jnp.sum(p["lambda_q1"][l] * p["lambda_k1"][l]))
               - jnp.exp(jnp.sum(p["lambda_q2"][l] * p["lambda_k2"][l]))
               + _lambda_init(l)).astype(F32).reshape(1)
        g_sub = p["g_sub"][l].reshape(1, LANES)
        a = attend(l, q, k_bf, v_bf, k_stack, v_stack, lam, g_sub)
        pool_out, pool_new = pool_past(l, u)
        pools.append(pool_new)
        j = l // 2
        routed = l % 2 == 1 and tm_moe is not None
        x, h2, *h2_rows = _outproj(a, pool_out, x, p["w_o"], l, gt_m, p["g_norm_ffn"], sc_f, sh_f,
                                   tm_out, tpg_out, routed)
        if l % 2 == 0:
            x = _ffn(h2, x, gt_f, p["dense_w_gate"], p["dense_w_up"], p["dense_w_down"], j,
                     tm_ffn, 256, tpg_ffn)
        elif routed:
            x = _moe_routed(h2, h2_rows[0], x, gt_f, p, j, tm, tm_moe, tm_out, tpg_out)
        else:
            combine, _ = _router(h2, p["router_w"][j], p["router_b"][j], tm)
            n_exp, _, f_exp = p["moe_w_gate"].shape[1:]
            wg = p["moe_w_gate"].reshape(-1, d, f_exp)
            wu = p["moe_w_up"].reshape(-1, d, f_exp)
            wd = p["moe_w_down"].reshape(-1, f_exp, d)
            for e in range(n_exp):
                x = _ffn(h2, x, gt_f, wg, wu, wd, j * n_exp + e, tm_ffn, 128, tpg_ffn,
                         cw=combine[:, e:e + 1])
    return x, k_stack, v_stack, jnp.stack(pools)


def kernel(x_prompt, x_sample, cache_k, cache_v, state_pool, page_table, c_prompt, c_sample,
           g_norm_mix, g_norm_ffn, w_ada, b_ada, w_in, g_q, g_k,
           lambda_q1, lambda_k1, lambda_q2, lambda_k2, g_sub, w_pool, pool_scale, w_o,
           dense_w_gate, dense_w_up, dense_w_down,
           router_w, router_b, moe_w_gate, moe_w_up, moe_w_down):
    p = dict(g_norm_mix=g_norm_mix.reshape(-1, 1, g_norm_mix.shape[-1]),
             g_norm_ffn=g_norm_ffn.reshape(-1, 1, g_norm_ffn.shape[-1]),
             w_in=w_in, g_q=g_q, g_k=g_k,
             lambda_q1=lambda_q1, lambda_k1=lambda_k1, lambda_q2=lambda_q2, lambda_k2=lambda_k2,
             g_sub=g_sub, w_pool=w_pool, pool_scale=pool_scale, w_o=w_o,
             dense_w_gate=dense_w_gate, dense_w_up=dense_w_up, dense_w_down=dense_w_down,
             router_w=router_w, router_b=router_b,
             moe_w_gate=moe_w_gate, moe_w_up=moe_w_up, moe_w_down=moe_w_down)
    depth = w_in.shape[0]
    bp, tp, d = x_prompt.shape
    bs, ts, _ = x_sample.shape
    width = N_HEADS * LANES
    c_pool = pool_scale.shape[-1]

    n_c = bp + bs
    rows = -(-n_c // 16) * 16
    c_all = jnp.concatenate([c_prompt, c_sample, jnp.zeros((rows - n_c, d), F32)], axis=0)
    mod = _ada(c_all, w_ada, b_ada).reshape(depth, rows, 6, d)
    mods_p = [[mod[l, :bp, i][:, None, :] for i in range(6)] for l in range(depth)]
    mods_s = [[jnp.repeat(mod[l, bp:n_c, i], ts, axis=0)[None] for i in range(6)]
              for l in range(depth)]

    zeros_past = jnp.zeros((bp, POOL_STATE + 1, c_pool), F32)

    def attend_p(l, q, k_bf, v_bf, ks, vs, lam, gs):
        return _attn_prompt(q, k_bf, v_bf, l, lam, gs, bp, tp, 512)

    def pool_p(l, u):
        u3 = u.reshape(bp, tp, c_pool)
        out = _pool(u3, zeros_past, w_pool, pool_scale, l, 0)
        return out.reshape(bp * tp, c_pool), u3[:, tp - POOL_STATE:]

    kp0 = jnp.zeros((depth, bp * tp * N_HEADS, LANES), F32)
    vp0 = jnp.zeros((depth, bp * tp * N_HEADS, LANES), F32)
    y_p, k_p, v_p, pool_p_new = _trunk(x_prompt.reshape(bp * tp, d), mods_p, 0, pool_p, attend_p,
                                       p, bp, tp, 512, 256, 1024, 1024, kp0, vp0)

    past_len = page_table.shape[1] * PAGE_SIZE
    t_pad = 16

    def attend_s(l, q, k_bf, v_bf, ks, vs, lam, gs):
        return _attn_decode(q, ks, vs, cache_k, cache_v, page_table, l, lam, gs, ts, 8)

    def pool_s(l, u):
        u3 = u.reshape(bs, ts, c_pool)
        past16 = jnp.concatenate([jnp.zeros((bs, 1, c_pool), F32), state_pool[l]], axis=1)
        u_pad = jnp.concatenate([u3, jnp.zeros((bs, t_pad - ts, c_pool), F32)], axis=1)
        out = _pool(u_pad, past16, w_pool, pool_scale, l, past_len)
        new = jnp.concatenate([state_pool[l], u3], axis=1)[:, -POOL_STATE:]
        return out[:, :ts].reshape(bs * ts, c_pool), new

    ks0 = jnp.zeros((depth, bs * ts * N_HEADS, LANES), F32)
    vs0 = jnp.zeros((depth, bs * ts * N_HEADS, LANES), F32)
    y_s, k_s, v_s, pool_s_new = _trunk(x_sample.reshape(bs * ts, d), mods_s, past_len, pool_s,
                                       attend_s, p, bs, ts, bs * ts, bs * ts, bs * ts, None,
                                       ks0, vs0)

    return (y_p.reshape(bp, tp, d), y_s.reshape(bs, ts, d),
            k_p.reshape(depth, bp, tp, N_HEADS, LANES), v_p.reshape(depth, bp, tp, N_HEADS, LANES),
            pool_p_new,
            k_s.reshape(depth, bs, ts, N_HEADS, LANES), v_s.reshape(depth, bs, ts, N_HEADS, LANES),
            pool_s_new)
```

```python
import functools
import math

import jax
import jax.numpy as jnp
from jax import lax
from jax.experimental import pallas as pl
from jax.experimental.pallas import tpu as pltpu

F32 = jnp.float32
BF16 = jnp.bfloat16

N_HEADS = 8
HEAD_DIM_V = 128
HEAD_DIM_QK = 64
POOL_WINDOWS = (2, 4, 8, 16)
POOL_STATE = 15
PAGE_SIZE = 128
N_EXPERTS = 8
EPS = 1e-6
NEG_INF = -1e30
LANES = 128
VMEM_LIMIT = 56 * 1024 * 1024
COL_CHUNK = 512
Q_SCALE = HEAD_DIM_QK ** -0.5 * math.log2(math.e)


def _params(n_axes, vmem=VMEM_LIMIT):
    return pltpu.CompilerParams(dimension_semantics=("arbitrary",) * n_axes,
                                vmem_limit_bytes=vmem)


def _lambda_init(layer):
    return 0.8 - 0.6 * math.exp(-0.3 * layer)


def _mod_spec(arr, tiles_per_group):
    _, r, d = arr.shape
    return pl.BlockSpec((1, r, d), lambda m, *_: (m // tiles_per_group, 0, 0))


def _norm_mod(x, g, sc, sh):
    y = x * lax.rsqrt(jnp.mean(x * x, axis=-1, keepdims=True) + EPS)
    return (y * g) * (1.0 + sc) + sh


def _ada_kernel(c_ref, w_ref, b_ref, o_ref):
    c = c_ref[...]
    s = (c * jax.nn.sigmoid(c)).astype(BF16)
    o_ref[0] = jnp.dot(s, w_ref[0].astype(BF16), preferred_element_type=F32) + b_ref[0]


def _ada(c_all, w_ada, b_ada):
    depth, d, n = w_ada.shape
    rows = c_all.shape[0]
    tn = 1024
    return pl.pallas_call(
        _ada_kernel,
        grid=(depth, n // tn),
        in_specs=[pl.BlockSpec((rows, d), lambda l, j: (0, 0)),
                  pl.BlockSpec((1, d, tn), lambda l, j: (l, 0, j)),
                  pl.BlockSpec((1, 1, tn), lambda l, j: (l, 0, j))],
        out_specs=pl.BlockSpec((1, rows, tn), lambda l, j: (l, 0, j)),
        out_shape=jax.ShapeDtypeStruct((depth, rows, n), F32),
        compiler_params=_params(2),
        name="ada_mod",
    )(c_all, w_ada, b_ada.reshape(depth, 1, n))


def _norm_kernel(x_ref, g_ref, sc_ref, sh_ref, h_ref):
    h_ref[...] = _norm_mod(x_ref[...], g_ref[0], sc_ref[0], sh_ref[0]).astype(BF16)


def _norm(x, g_all, layer, sc, sh, tm, tiles_per_group):
    m, d = x.shape
    return pl.pallas_call(
        _norm_kernel,
        grid=(m // tm,),
        in_specs=[pl.BlockSpec((tm, d), lambda i: (i, 0)),
                  pl.BlockSpec((1, 1, d), lambda i: (layer, 0, 0)),
                  _mod_spec(sc, tiles_per_group), _mod_spec(sh, tiles_per_group)],
        out_specs=pl.BlockSpec((tm, d), lambda i: (i, 0)),
        out_shape=jax.ShapeDtypeStruct((m, d), BF16),
        compiler_params=_params(1),
        name="norm_mod",
    )(x, g_all, sc, sh)


def _qk_norm(p, g128, scale):
    lo = lax.broadcasted_iota(jnp.int32, (1, LANES), 1) < HEAD_DIM_QK
    sq = p * p
    s0 = jnp.sum(jnp.where(lo, sq, 0.0), axis=-1, keepdims=True)
    s1 = jnp.sum(jnp.where(lo, 0.0, sq), axis=-1, keepdims=True)
    inv0 = lax.rsqrt(s0 * (1.0 / HEAD_DIM_QK) + EPS)
    inv1 = lax.rsqrt(s1 * (1.0 / HEAD_DIM_QK) + EPS)
    y = (p * jnp.where(lo, inv0, inv1)) * g128
    return y if scale is None else y * scale


def _inproj_kernel(*refs, mode):
    normed = mode in ("q", "k")
    stacked = mode in ("k", "v")
    h_ref, w_ref = refs[:2]
    g_ref = refs[2] if normed else None
    wbf_ref = refs[-1]
    outs = refs[-3:-1] if stacked else refs[-2:-1]

    @pl.when(pl.program_id(0) == 0)
    def _():
        wbf_ref[...] = w_ref[0].astype(BF16)

    h = h_ref[...]
    scale = Q_SCALE if mode == "q" else None
    for c0 in range(0, wbf_ref.shape[1], COL_CHUNK):
        acc = jnp.dot(h, wbf_ref[:, c0:c0 + COL_CHUNK], preferred_element_type=F32)
        if mode == "u":
            outs[0][:, c0:c0 + COL_CHUNK] = acc
            continue
        for hh in range(COL_CHUNK // LANES):
            head = c0 // LANES + hh
            y = acc[:, hh * LANES:(hh + 1) * LANES]
            if normed:
                y = _qk_norm(y, g_ref[...], scale)
            if stacked:
                outs[0][0, pl.ds(head, h.shape[0], stride=N_HEADS), :] = y
                outs[1][:, head * LANES:(head + 1) * LANES] = y.astype(BF16)
            else:
                outs[0][:, head * LANES:(head + 1) * LANES] = y.astype(BF16)


def _inproj(h, w_in, layer, section, mode, tm, g128=None, stack=None):
    m, d = h.shape
    width = N_HEADS * LANES
    ins = [h, w_in]
    in_specs = [pl.BlockSpec((tm, d), lambda i: (i, 0)),
                pl.BlockSpec((1, d, width), lambda i: (layer, 0, section))]
    if g128 is not None:
        ins.append(g128)
        in_specs.append(pl.BlockSpec((1, LANES), lambda i: (0, 0)))
    flat_spec = pl.BlockSpec((tm, width), lambda i: (i, 0))
    aliases = {}
    if stack is not None:
        aliases = {len(ins): 0}
        ins.append(stack)
        in_specs.append(pl.BlockSpec(memory_space=pl.ANY))
        out_specs = [pl.BlockSpec((1, tm * N_HEADS, LANES), lambda i: (layer, i, 0)), flat_spec]
        out_shape = [jax.ShapeDtypeStruct(stack.shape, stack.dtype),
                     jax.ShapeDtypeStruct((m, width), BF16)]
    else:
        out_specs = [flat_spec]
        out_shape = [jax.ShapeDtypeStruct((m, width), BF16 if mode == "q" else F32)]
    n_in = len(ins)

    def body(*refs):
        if stack is not None:
            refs = refs[:n_in - 1] + refs[n_in:]
        _inproj_kernel(*refs, mode=mode)

    out = pl.pallas_call(
        body,
        grid=(m // tm,),
        in_specs=in_specs,
        out_specs=out_specs,
        out_shape=out_shape,
        scratch_shapes=[pltpu.VMEM((d, width), BF16)],
        input_output_aliases=aliases,
        compiler_params=_params(1),
        name="inproj_" + mode,
    )(*ins)
    return out if stack is not None else out[0]


def _sub_norm(o, g_sub, li):
    y = o * lax.rsqrt(jnp.mean(o * o, axis=-1, keepdims=True) + EPS)
    return (y * g_sub) * (1.0 - li)


def _attn_kernel(lam_ref, q_ref, k_ref, v_ref, g_ref, o_ref, vt_ref, s_ref, m_ref, l_ref, acc_ref,
                 *, tq, li):
    n_t = vt_ref.shape[0]
    for jt in range(n_t):
        vt_ref[jt] = v_ref[0, jt * tq:(jt + 1) * tq, :].astype(F32).T.astype(BF16)
    lo = lax.broadcasted_iota(jnp.int32, (1, LANES), 1) < HEAD_DIM_QK
    pairs = [(i, j) for i in range(n_t) for j in range(i + 1)]

    def logits(n):
        i, j = pairs[n]
        q = q_ref[0, i * tq:(i + 1) * tq, :]
        zero = jnp.zeros_like(q)
        kt = k_ref[0, j * tq:(j + 1) * tq, :]
        for mp, qm in enumerate((jnp.where(lo, q, zero), jnp.where(lo, zero, q))):
            s_ref[n % 2, mp] = lax.dot_general(kt, qm, (((1,), (1,)), ((), ())),
                                               preferred_element_type=F32)

    def consume(n):
        i, j = pairs[n]
        vt = vt_ref[j]
        for mp in range(2):
            s = s_ref[n % 2, mp]
            if j == i:
                key = lax.broadcasted_iota(jnp.int32, (tq, tq), 0)
                qry = lax.broadcasted_iota(jnp.int32, (tq, tq), 1)
                s = jnp.where(key <= qry, s, NEG_INF)
            m_old = m_ref[mp]
            m_new = jnp.maximum(m_old, jnp.max(s, axis=0, keepdims=True))
            alpha = jnp.exp2(m_old - m_new)
            p = jnp.exp2(s - m_new)
            l_ref[mp] = alpha * l_ref[mp] + jnp.sum(p, axis=0, keepdims=True)
            acc_ref[mp] = alpha * acc_ref[mp] + jnp.dot(vt, p.astype(BF16),
                                                        preferred_element_type=F32)
            m_ref[mp] = m_new

    logits(0)
    for n, (i, j) in enumerate(pairs):
        if j == 0:
            m_ref[...] = jnp.full(m_ref.shape, NEG_INF, F32)
            l_ref[...] = jnp.zeros(l_ref.shape, F32)
            acc_ref[...] = jnp.zeros(acc_ref.shape, F32)
        if n + 1 < len(pairs):
            logits(n + 1)
        consume(n)
        if j == i:
            o = acc_ref[0] / l_ref[0] - lam_ref[0] * (acc_ref[1] / l_ref[1])
            y = o * lax.rsqrt(jnp.mean(o * o, axis=0, keepdims=True) + EPS)
            o_ref[0, i * tq:(i + 1) * tq, :] = ((y.T * g_ref[...]) * (1.0 - li)).astype(BF16)


def _attn_prompt(q, k, v, layer, lam, g_sub, batch, seq, tq):
    width = N_HEADS * LANES
    q3, k3, v3 = (t.reshape(batch, seq, width) for t in (q, k, v))
    head_spec = pl.BlockSpec((1, seq, LANES), lambda b, h: (b, 0, h))
    out = pl.pallas_call(
        functools.partial(_attn_kernel, tq=tq, li=_lambda_init(layer)),
        grid=(batch, N_HEADS),
        in_specs=[pl.BlockSpec(memory_space=pltpu.SMEM),
                  head_spec, head_spec, head_spec,
                  pl.BlockSpec((1, LANES), lambda b, h: (0, 0))],
        out_specs=head_spec,
        out_shape=jax.ShapeDtypeStruct((batch, seq, width), BF16),
        scratch_shapes=[pltpu.VMEM((seq // tq, LANES, tq), BF16),
                        pltpu.VMEM((2, 2, tq, tq), F32),
                        pltpu.VMEM((2, 1, tq), F32), pltpu.VMEM((2, 1, tq), F32),
                        pltpu.VMEM((2, LANES, tq), F32)],
        compiler_params=_params(2),
        name="attn_prompt",
    )(lam, q3, k3, v3, g_sub)
    return out.reshape(batch * seq, width)


def _decode_kernel(pt_ref, lam_ref, q_ref, *refs, pages, n_new, li):
    k_refs = refs[:pages]
    v_refs = refs[pages:2 * pages]
    kn_ref, vn_ref, g_ref, o_ref, knp_ref, vnp_ref, m_ref, l_ref, acc_ref = refs[2 * pages:]
    b = pl.program_id(0)
    c = pl.program_id(1)
    last = pl.num_programs(1) - 1
    shift_q = (2 * n_new).bit_length() - 1

    @pl.when(jnp.logical_and(b == 0, c == 0))
    def _():
        knp_ref[...] = jnp.zeros(knp_ref.shape, F32)
        vnp_ref[...] = jnp.zeros(vnp_ref.shape, F32)

    @pl.when(c == 0)
    def _():
        m_ref[...] = jnp.full(m_ref.shape, NEG_INF, F32)
        l_ref[...] = jnp.zeros(l_ref.shape, F32)
        acc_ref[...] = jnp.zeros(acc_ref.shape, F32)

    q = q_ref[0]
    n_q = q.shape[0]
    page_rows = PAGE_SIZE * N_HEADS
    def own_head(n_cols):
        row = lax.broadcasted_iota(jnp.int32, (n_q, n_cols), 0)
        col = lax.broadcasted_iota(jnp.int32, (n_q, n_cols), 1)
        return row, col, jnp.bitwise_and(col, N_HEADS - 1) == lax.shift_right_logical(row, shift_q)

    def logits(k2d, keep):
        s = lax.dot_general(q, k2d.astype(BF16), (((1,), (1,)), ((), ())),
                            preferred_element_type=F32)
        return jnp.where(keep, s, NEG_INF)

    def update(s_list, v_list):
        m_old = m_ref[...]
        m_new = m_old
        for s in s_list:
            m_new = jnp.maximum(m_new, jnp.max(s, axis=-1, keepdims=True))
        alpha = jnp.exp2(m_old - m_new)
        l_new = alpha * l_ref[...]
        pv = None
        for s, v in zip(s_list, v_list):
            p = jnp.exp2(s - m_new)
            l_new = l_new + jnp.sum(p, axis=-1, keepdims=True)
            d = jnp.dot(p.astype(BF16), v, preferred_element_type=F32)
            pv = d if pv is None else pv + d
        acc_ref[...] = alpha * acc_ref[...] + pv
        l_ref[...] = l_new
        m_ref[...] = m_new

    s_list, v_list = [], []
    page_keep = own_head(page_rows)[2]
    for j in range(pages):
        s_list.append(logits(k_refs[j][0, 0].reshape(page_rows, LANES), page_keep))
        v_list.append(v_refs[j][0, 0].reshape(page_rows, LANES).astype(BF16))
    update(s_list, v_list)

    @pl.when(c == last)
    def _():
        n_rows = n_new * N_HEADS
        knp_ref[0:n_rows, :] = kn_ref[0, 0]
        vnp_ref[0:n_rows, :] = vn_ref[0, 0]
        row, col, keep = own_head(knp_ref.shape[0])
        tok = jnp.bitwise_and(row, n_new - 1)
        slot = lax.shift_right_logical(col, N_HEADS.bit_length() - 1)
        keep = jnp.logical_and(keep, slot <= tok)
        update([logits(knp_ref[...], keep)], [vnp_ref[...].astype(BF16)])

        o = acc_ref[...] / l_ref[...]
        o = o - lam_ref[0] * pltpu.roll(o, shift=n_q - n_new, axis=0)
        o_ref[0] = _sub_norm(o, g_ref[...], li)


def _attn_decode(q, k_stack, v_stack, cache_k, cache_v, page_table, layer, lam, g_sub,
                 n_new, pages):
    depth = cache_k.shape[0]
    batch, n_pages = page_table.shape
    assert n_new & (n_new - 1) == 0 and N_HEADS & (N_HEADS - 1) == 0
    n_q = N_HEADS * 2 * n_new
    qh = q.reshape(batch, n_new, N_HEADS, 1, LANES).transpose(0, 2, 3, 1, 4)
    in_map = (jnp.arange(LANES) // HEAD_DIM_QK)[None, :] == jnp.arange(2)[:, None]
    qm = jnp.where(in_map[None, None, :, None, :], qh, jnp.zeros_like(qh)).reshape(batch, n_q, LANES)
    kn = k_stack.reshape(depth, batch, n_new * N_HEADS, LANES)
    vn = v_stack.reshape(depth, batch, n_new * N_HEADS, LANES)

    def page_spec(j):
        return pl.BlockSpec((1, 1, PAGE_SIZE, N_HEADS, LANES),
                            lambda b, c, pt: (layer, pt[b, c * pages + j], 0, 0, 0))

    new_spec = pl.BlockSpec((1, 1, n_new * N_HEADS, LANES), lambda b, c, pt: (layer, b, 0, 0))
    grid_spec = pltpu.PrefetchScalarGridSpec(
        num_scalar_prefetch=1,
        grid=(batch, n_pages // pages),
        in_specs=([pl.BlockSpec(memory_space=pltpu.SMEM),
                   pl.BlockSpec((1, n_q, LANES), lambda b, c, pt: (b, 0, 0))]
                  + [page_spec(j) for j in range(pages)]
                  + [page_spec(j) for j in range(pages)]
                  + [new_spec, new_spec,
                     pl.BlockSpec((1, LANES), lambda b, c, pt: (0, 0))]),
        out_specs=pl.BlockSpec((1, n_q, LANES), lambda b, c, pt: (b, 0, 0)),
        scratch_shapes=[pltpu.VMEM((LANES, LANES), F32),
                        pltpu.VMEM((LANES, LANES), F32),
                        pltpu.VMEM((n_q, 1), F32),
                        pltpu.VMEM((n_q, 1), F32),
                        pltpu.VMEM((n_q, LANES), F32)],
    )
    out = pl.pallas_call(
        functools.partial(_decode_kernel, pages=pages, n_new=n_new, li=_lambda_init(layer)),
        grid_spec=grid_spec,
        out_shape=jax.ShapeDtypeStruct((batch, n_q, LANES), F32),
        compiler_params=_params(2),
        name="attn_decode",
    )(page_table, lam, qm, *([cache_k] * pages), *([cache_v] * pages), kn, vn, g_sub)
    out = out.reshape(batch, N_HEADS, 2, n_new, LANES)[:, :, 0]
    return out.transpose(0, 2, 1, 3).reshape(batch * n_new, N_HEADS * LANES).astype(BF16)


def _pool_kernel(u_ref, past_ref, w_ref, sc_ref, o_ref, full_ref, *, start, chunk):
    halo = past_ref.shape[1]
    t = u_ref.shape[1]
    group = u_ref.shape[2] // len(POOL_WINDOWS)
    full_ref[0:halo, :] = past_ref[0]
    full_ref[halo:halo + t, :] = u_ref[0]
    for c0 in range(0, t, chunk):
        pos = start + c0 + lax.broadcasted_iota(jnp.int32, (chunk, 1), 0)
        for g, w in enumerate(POOL_WINDOWS):
            cols = slice(g * group, (g + 1) * group)
            cur = full_ref[halo + c0:halo + c0 + chunk, cols]
            win = cur
            for i in range(1, w):
                win = win + full_ref[halo + c0 - i:halo + c0 - i + chunk, cols]
            count = jnp.minimum(w, pos + 1).astype(F32)
            pooled = (win / count - cur).astype(BF16)
            y = jnp.dot(pooled, w_ref[0, g].astype(BF16), preferred_element_type=F32)
            o_ref[0, c0:c0 + chunk, cols] = (y * sc_ref[0, :, cols]).astype(BF16)


def _pool(u3, past16, w_pool, pool_scale, layer, start):
    batch, t, c = u3.shape
    depth = w_pool.shape[0]
    group = c // len(POOL_WINDOWS)
    chunk = min(t, 256)
    return pl.pallas_call(
        functools.partial(_pool_kernel, start=start, chunk=chunk),
        grid=(batch,),
        in_specs=[pl.BlockSpec((1, t, c), lambda b: (b, 0, 0)),
                  pl.BlockSpec((1, POOL_STATE + 1, c), lambda b: (b, 0, 0)),
                  pl.BlockSpec((1, len(POOL_WINDOWS), group, group), lambda b: (layer, 0, 0, 0)),
                  pl.BlockSpec((1, 1, c), lambda b: (layer, 0, 0))],
        out_specs=pl.BlockSpec((1, t, c), lambda b: (b, 0, 0)),
        out_shape=jax.ShapeDtypeStruct((batch, t, c), BF16),
        scratch_shapes=[pltpu.VMEM((POOL_STATE + 1 + t, c), F32)],
        compiler_params=_params(1),
        name="pool_mix",
    )(u3, past16, w_pool, pool_scale.reshape(depth, 1, c))


def _outproj_kernel(a_ref, p_ref, x_ref, w_ref, gt_ref, g_ref, sc_ref, sh_ref,
                    x1_ref, h2_ref, h2_rows_ref, wbf_ref):
    @pl.when(pl.program_id(0) == 0)
    def _():
        wbf_ref[...] = w_ref[0].astype(BF16)

    half = a_ref.shape[1]
    a = a_ref[...]
    p = p_ref[...]
    for c0 in range(0, x_ref.shape[1], COL_CHUNK):
        cols = slice(c0, c0 + COL_CHUNK)
        y = jnp.dot(a, wbf_ref[0:half, cols], preferred_element_type=F32)
        y = y + jnp.dot(p, wbf_ref[half:, cols], preferred_element_type=F32)
        x1_ref[:, cols] = x_ref[:, cols] + gt_ref[0, :, cols] * y
    h2 = _norm_mod(x1_ref[...], g_ref[0], sc_ref[0], sh_ref[0])
    h2_ref[...] = h2.astype(BF16)
    if h2_rows_ref is not None:
        tm, chunks = h2.shape[0], h2.shape[1] // LANES
        for c in range(chunks):
            h2_rows_ref[pl.ds(c, tm, stride=chunks), :] = h2[:, c * LANES:(c + 1) * LANES]


def _outproj(a, p, x, w_o, layer, gt, g_ffn, sc, sh, tm, tiles_per_group, row_major):
    m, d = x.shape
    half = a.shape[1]
    out_specs = [pl.BlockSpec((tm, d), lambda i: (i, 0)), pl.BlockSpec((tm, d), lambda i: (i, 0))]
    out_shape = [jax.ShapeDtypeStruct((m, d), F32), jax.ShapeDtypeStruct((m, d), BF16)]
    chunks = d // LANES
    if row_major:
        out_specs.append(pl.BlockSpec((tm * chunks, LANES), lambda i: (i, 0)))
        out_shape.append(jax.ShapeDtypeStruct((m * chunks, LANES), F32))

    def body(*refs):
        if row_major:
            _outproj_kernel(*refs)
        else:
            _outproj_kernel(*refs[:-1], None, refs[-1])

    outs = pl.pallas_call(
        body,
        grid=(m // tm,),
        in_specs=[pl.BlockSpec((tm, half), lambda i: (i, 0)),
                  pl.BlockSpec((tm, half), lambda i: (i, 0)),
                  pl.BlockSpec((tm, d), lambda i: (i, 0)),
                  pl.BlockSpec((1, d, d), lambda i: (layer, 0, 0),
                               pipeline_mode=pl.Buffered(1)),
                  _mod_spec(gt, tiles_per_group),
                  pl.BlockSpec((1, 1, d), lambda i: (layer, 0, 0)),
                  _mod_spec(sc, tiles_per_group), _mod_spec(sh, tiles_per_group)],
        out_specs=out_specs,
        out_shape=out_shape,
        scratch_shapes=[pltpu.VMEM((d, d), BF16)],
        compiler_params=_params(1),
        name="outproj",
    )(a, p, x, w_o, gt, g_ffn, sc, sh)
    if row_major:
        return outs[0], outs[1], outs[2].reshape(m, chunks, LANES)
    return outs


def _ffn_kernel(h_ref, x_ref, gt_ref, wg_ref, wu_ref, wd_ref, o_ref):
    @pl.when(pl.program_id(1) == 0)
    def _():
        o_ref[...] = x_ref[...]

    h = h_ref[...]
    g = jnp.dot(h, wg_ref[0].astype(BF16), preferred_element_type=F32)
    u = jnp.dot(h, wu_ref[0].astype(BF16), preferred_element_type=F32)
    act = ((g * jax.nn.sigmoid(g)) * u).astype(BF16)
    for c0 in range(0, o_ref.shape[1], COL_CHUNK):
        cols = slice(c0, c0 + COL_CHUNK)
        y = jnp.dot(act, wd_ref[0, :, cols].astype(BF16), preferred_element_type=F32)
        o_ref[:, cols] += gt_ref[0, :, cols] * y


def _ffn(h, x, gt, wg, wu, wd, widx, tm, tf, tiles_per_group):
    m, d = x.shape
    f = wg.shape[2]
    return pl.pallas_call(
        _ffn_kernel,
        grid=(m // tm, f // tf),
        in_specs=[pl.BlockSpec((tm, d), lambda i, j: (i, 0)),
                  pl.BlockSpec((tm, d), lambda i, j: (i, 0), pipeline_mode=pl.Buffered(1)),
                  _mod_spec(gt, tiles_per_group),
                  pl.BlockSpec((1, d, tf), lambda i, j: (widx, 0, j)),
                  pl.BlockSpec((1, d, tf), lambda i, j: (widx, 0, j)),
                  pl.BlockSpec((1, tf, d), lambda i, j: (widx, j, 0))],
        out_specs=pl.BlockSpec((tm, d), lambda i, j: (i, 0)),
        out_shape=jax.ShapeDtypeStruct((m, d), F32),
        compiler_params=_params(2),
        name="ffn",
    )(h, x, gt, wg, wu, wd)


def _all_experts_kernel(h_ref, x_ref, gt_ref, r_ref, wg_ref, wu_ref, wd_ref, o_ref):
    e = pl.program_id(1)

    @pl.when(jnp.logical_and(e == 0, pl.program_id(2) == 0))
    def _():
        o_ref[...] = x_ref[...]

    h = h_ref[...]
    g = jnp.dot(h, wg_ref[0].astype(BF16), preferred_element_type=F32)
    u = jnp.dot(h, wu_ref[0].astype(BF16), preferred_element_type=F32)
    act = ((g * jax.nn.sigmoid(g)) * u).astype(BF16)
    route = r_ref[...]
    lane = lax.broadcasted_iota(jnp.int32, route.shape, 1)
    cw = jnp.sum(jnp.where(lane == e, route, 0.0), axis=-1, keepdims=True)
    for c0 in range(0, o_ref.shape[1], COL_CHUNK):
        cols = slice(c0, c0 + COL_CHUNK)
        y = jnp.dot(act, wd_ref[0, :, cols].astype(BF16), preferred_element_type=F32)
        o_ref[:, cols] += (gt_ref[0, :, cols] * cw) * y


def _all_experts(h, x, gt, route, wg, wu, wd, layer_base, tm, tf, tiles_per_group):
    m, d = x.shape
    f = wg.shape[2]
    return pl.pallas_call(
        _all_experts_kernel,
        grid=(m // tm, N_EXPERTS, f // tf),
        in_specs=[pl.BlockSpec((tm, d), lambda i, e, j: (i, 0)),
                  pl.BlockSpec((tm, d), lambda i, e, j: (i, 0)),
                  _mod_spec(gt, tiles_per_group),
                  pl.BlockSpec((tm, LANES), lambda i, e, j: (i, 0)),
                  pl.BlockSpec((1, d, tf), lambda i, e, j: (layer_base + e, 0, j)),
                  pl.BlockSpec((1, d, tf), lambda i, e, j: (layer_base + e, 0, j)),
                  pl.BlockSpec((1, tf, d), lambda i, e, j: (layer_base + e, j, 0))],
        out_specs=pl.BlockSpec((tm, d), lambda i, e, j: (i, 0)),
        out_shape=jax.ShapeDtypeStruct((m, d), F32),
        compiler_params=_params(3),
        name="moe_all_experts",
    )(h, x, gt, route, wg, wu, wd)


ROUTE_E1, ROUTE_E2, ROUTE_POS1, ROUTE_POS2, ROUTE_G1, ROUTE_G2 = range(N_EXPERTS, N_EXPERTS + 6)


def _router_kernel(h_ref, whi_ref, wlo_ref, b_ref, o_ref, cnt_ref, carry_ref):
    @pl.when(pl.program_id(0) == 0)
    def _():
        carry_ref[...] = jnp.zeros(carry_ref.shape, F32)

    h = h_ref[...].astype(BF16)
    logits = (jnp.dot(h, whi_ref[...], preferred_element_type=F32)
              + jnp.dot(h, wlo_ref[...], preferred_element_type=F32) + b_ref[...])
    lane = lax.broadcasted_iota(jnp.int32, logits.shape, 1).astype(F32)
    logits = jnp.where(lane < N_EXPERTS, logits, NEG_INF)
    v1 = jnp.max(logits, axis=-1, keepdims=True)
    i1 = jnp.min(jnp.where(logits == v1, lane, float(LANES)), axis=-1, keepdims=True)
    rest = jnp.where(lane == i1, NEG_INF, logits)
    v2 = jnp.max(rest, axis=-1, keepdims=True)
    i2 = jnp.min(jnp.where(rest == v2, lane, float(LANES)), axis=-1, keepdims=True)
    e2 = jnp.exp(v2 - v1)
    g1 = 1.0 / (1.0 + e2)
    g2 = e2 / (1.0 + e2)

    tm = h.shape[0]
    sel = jnp.where(jnp.logical_or(lane == i1, lane == i2), 1.0, 0.0)
    earlier = (lax.broadcasted_iota(jnp.int32, (tm, tm), 1)
               < lax.broadcasted_iota(jnp.int32, (tm, tm), 0))
    rank = jnp.dot(jnp.where(earlier, 1.0, 0.0).astype(BF16), sel.astype(BF16),
                   preferred_element_type=F32) + carry_ref[...]
    pos1 = jnp.sum(jnp.where(lane == i1, rank, 0.0), axis=-1, keepdims=True)
    pos2 = jnp.sum(jnp.where(lane == i2, rank, 0.0), axis=-1, keepdims=True)
    total = carry_ref[...] + jnp.sum(sel, axis=0, keepdims=True)
    carry_ref[...] = total
    cnt_ref[...] = total

    out = jnp.where(lane == i1, g1, 0.0) + jnp.where(lane == i2, g2, 0.0)
    for idx, val in ((ROUTE_E1, i1), (ROUTE_E2, i2), (ROUTE_POS1, pos1), (ROUTE_POS2, pos2),
                     (ROUTE_G1, g1), (ROUTE_G2, g2)):
        out = out + jnp.where(lane == float(idx), val, 0.0)
    o_ref[...] = out


def _router(h, router_w, router_b, tm):
    m, d = h.shape
    pad = LANES - N_EXPERTS
    w = jnp.pad(router_w, ((0, 0), (0, pad)))
    whi = w.astype(BF16)
    wlo = (w - whi.astype(F32)).astype(BF16)
    b = jnp.pad(router_b, (0, pad)).reshape(1, LANES)
    return pl.pallas_call(
        _router_kernel,
        grid=(m // tm,),
        in_specs=[pl.BlockSpec((tm, d), lambda i: (i, 0)),
                  pl.BlockSpec((d, LANES), lambda i: (0, 0)),
                  pl.BlockSpec((d, LANES), lambda i: (0, 0)),
                  pl.BlockSpec((1, LANES), lambda i: (0, 0))],
        out_specs=[pl.BlockSpec((tm, LANES), lambda i: (i, 0)),
                   pl.BlockSpec((1, LANES), lambda i: (0, 0))],
        out_shape=[jax.ShapeDtypeStruct((m, LANES), F32),
                   jax.ShapeDtypeStruct((1, LANES), F32)],
        scratch_shapes=[pltpu.VMEM((1, LANES), F32)],
        compiler_params=_params(1),
        name="router",
    )(h, whi, wlo, b)


def _route_plan(route, counts, tm):
    m = route.shape[0]
    n_tiles = 2 * m // tm + N_EXPERTS
    col = lambda i: route[:, i].astype(jnp.int32)
    cnt = counts[0, :N_EXPERTS].astype(jnp.int32)
    tiles = (cnt + tm - 1) // tm
    tile_end = jnp.cumsum(tiles)
    row_start = (tile_end - tiles) * tm
    d1 = row_start[col(ROUTE_E1)] + col(ROUTE_POS1)
    d2 = row_start[col(ROUTE_E2)] + col(ROUTE_POS2)
    token = jnp.arange(m, dtype=jnp.int32)
    src = jnp.zeros((n_tiles * tm,), jnp.int32).at[d1].set(token).at[d2].set(token)
    tile_id = jnp.arange(n_tiles, dtype=jnp.int32)
    tile_expert = jnp.minimum(jnp.searchsorted(tile_end, tile_id, side="right"),
                              N_EXPERTS - 1).astype(jnp.int32)
    tile_rows = jnp.clip(cnt[tile_expert] - (tile_id - (tile_end - tiles)[tile_expert]) * tm, 0, tm)
    tile_rows = jnp.where(tile_id < tile_end[-1], tile_rows, 0).astype(jnp.int32)
    return d1, d2, src, tile_expert, tile_rows, tile_end[-1:].astype(jnp.int32)


def _row_copy(src_hbm, row, dst_vmem, slot, sem):
    return pltpu.make_async_copy(src_hbm.at[pl.ds(row, 1), :], dst_vmem.at[pl.ds(slot, 1), :], sem)


def _slab_copy(src_hbm, row, dst_vmem, slot, sem):
    chunks = src_hbm.shape[1]
    return pltpu.make_async_copy(
        src_hbm.at[row], dst_vmem.at[pl.ds(pl.multiple_of(slot * chunks, chunks), chunks), :], sem)


def _gather_kernel(rows_ref, src_ref, h_hbm, o_ref, buf_ref, sems):
    i = pl.program_id(0)
    n_tiles = pl.num_programs(0) - 1
    tm = o_ref.shape[0]
    chunks = h_hbm.shape[1]

    @pl.when(i == 0)
    def _():
        buf_ref[...] = jnp.zeros(buf_ref.shape, F32)

    def copy(slot, r, row):
        return _slab_copy(h_hbm, row, buf_ref.at[slot], r, sems.at[slot])

    @pl.when(i < n_tiles)
    def _():
        slot = jnp.bitwise_and(i, 1)

        def issue(r, c):
            copy(slot, r, src_ref[0, 0, r]).start()
            return c

        lax.fori_loop(0, rows_ref[i], issue, 0)

    @pl.when(i > 0)
    def _():
        slot = jnp.bitwise_and(i - 1, 1)

        def wait(r, c):
            copy(slot, r, 0).wait()
            return c

        lax.fori_loop(0, rows_ref[i - 1], wait, 0)
        for c in range(chunks):
            o_ref[:, c * LANES:(c + 1) * LANES] = (
                buf_ref[slot, pl.ds(c, tm, stride=chunks), :].astype(BF16))


def _gather_rows(h_rows, src, tile_rows, tm):
    _, chunks, _ = h_rows.shape
    n_rows = src.shape[0]
    n_tiles = n_rows // tm
    grid_spec = pltpu.PrefetchScalarGridSpec(
        num_scalar_prefetch=1,
        grid=(n_tiles + 1,),
        in_specs=[pl.BlockSpec((1, 1, tm), lambda i, nr: (jnp.minimum(i, n_tiles - 1), 0, 0),
                               memory_space=pltpu.SMEM),
                  pl.BlockSpec(memory_space=pl.ANY)],
        out_specs=pl.BlockSpec((tm, chunks * LANES), lambda i, nr: (jnp.maximum(i - 1, 0), 0)),
        scratch_shapes=[pltpu.VMEM((2, tm * chunks, LANES), F32), pltpu.SemaphoreType.DMA((2,))],
    )
    return pl.pallas_call(
        _gather_kernel,
        grid_spec=grid_spec,
        out_shape=jax.ShapeDtypeStruct((n_rows, chunks * LANES), BF16),
        compiler_params=_params(1),
        name="moe_gather",
    )(tile_rows, src.reshape(n_rows // tm, 1, tm), h_rows)


def _expert_ffn_kernel(te_ref, nv_ref, h_ref, wg_ref, wu_ref, wd_ref, wgt_ref, wut_ref, wdt_ref,
                       o_ref, *, n_main):
    i = pl.program_id(0)
    j = pl.program_id(1)

    @pl.when(j == 0)
    def _():
        o_ref[...] = jnp.zeros(o_ref.shape, F32)

    def accumulate(wg, wu, wd):
        h = h_ref[...]
        g = jnp.dot(h, wg[0].astype(BF16), preferred_element_type=F32)
        u = jnp.dot(h, wu[0].astype(BF16), preferred_element_type=F32)
        act = ((g * jax.nn.sigmoid(g)) * u).astype(BF16)
        for c0 in range(0, o_ref.shape[1], COL_CHUNK):
            cols = slice(c0, c0 + COL_CHUNK)
            o_ref[:, cols] += jnp.dot(act, wd[0, :, cols].astype(BF16),
                                      preferred_element_type=F32)

    in_use = i < nv_ref[0]

    @pl.when(jnp.logical_and(in_use, j < n_main))
    def _():
        accumulate(wg_ref, wu_ref, wd_ref)

    @pl.when(jnp.logical_and(in_use, j == n_main))
    def _():
        accumulate(wgt_ref, wut_ref, wdt_ref)


def _expert_ffn(xs, wg, wu, wd, layer_base, tile_expert, n_valid, tm):
    n_rows, d = xs.shape
    f = wg.shape[2]
    wide, tail = 2 * LANES, LANES
    n_main = f // wide
    assert f == n_main * wide + tail
    tail_blk = f // tail - 1

    def main_blk(i, j, nv):
        return jnp.where(i < nv[0], jnp.minimum(j, n_main - 1), 0)

    grid_spec = pltpu.PrefetchScalarGridSpec(
        num_scalar_prefetch=2,
        grid=(n_rows // tm, n_main + 1),
        in_specs=[
            pl.BlockSpec((tm, d), lambda i, j, te, nv: (i, 0)),
            pl.BlockSpec((1, d, wide), lambda i, j, te, nv: (layer_base + te[i], 0, main_blk(i, j, nv))),
            pl.BlockSpec((1, d, wide), lambda i, j, te, nv: (layer_base + te[i], 0, main_blk(i, j, nv))),
            pl.BlockSpec((1, wide, d), lambda i, j, te, nv: (layer_base + te[i], main_blk(i, j, nv), 0)),
            pl.BlockSpec((1, d, tail), lambda i, j, te, nv: (layer_base + te[i], 0, tail_blk)),
            pl.BlockSpec((1, d, tail), lambda i, j, te, nv: (layer_base + te[i], 0, tail_blk)),
            pl.BlockSpec((1, tail, d), lambda i, j, te, nv: (layer_base + te[i], tail_blk, 0)),
        ],
        out_specs=pl.BlockSpec((tm, d), lambda i, j, te, nv: (i, 0)),
    )
    return pl.pallas_call(
        functools.partial(_expert_ffn_kernel, n_main=n_main),
        grid_spec=grid_spec,
        out_shape=jax.ShapeDtypeStruct((n_rows, d), F32),
        compiler_params=_params(2),
        name="moe_ffn",
    )(tile_expert, n_valid, xs, wg, wu, wd, wg, wu, wd)


def _combine_kernel(d1_ref, d2_ref, x_ref, r_ref, gt_ref, y_hbm, o_ref, a_ref, b_ref, sems):
    i = pl.program_id(0)
    n_tiles = pl.num_programs(0) - 1
    tc = a_ref.shape[1]

    def copies(slot, r, row1, row2):
        return (_row_copy(y_hbm, row1, a_ref.at[slot], r, sems.at[0, slot]),
                _row_copy(y_hbm, row2, b_ref.at[slot], r, sems.at[1, slot]))

    @pl.when(i < n_tiles)
    def _():
        slot = jnp.bitwise_and(i, 1)

        def issue(r, c):
            for cp in copies(slot, r, d1_ref[0, 0, r], d2_ref[0, 0, r]):
                cp.start()
            return c

        lax.fori_loop(0, tc, issue, 0, unroll=4)

    @pl.when(i > 0)
    def _():
        slot = jnp.bitwise_and(i - 1, 1)

        def wait(r, c):
            for cp in copies(slot, r, 0, 0):
                cp.wait()
            return c

        lax.fori_loop(0, tc, wait, 0, unroll=4)
        route = r_ref[...]
        lane = lax.broadcasted_iota(jnp.int32, route.shape, 1)
        g1 = jnp.sum(jnp.where(lane == ROUTE_G1, route, 0.0), axis=-1, keepdims=True)
        g2 = jnp.sum(jnp.where(lane == ROUTE_G2, route, 0.0), axis=-1, keepdims=True)
        o_ref[...] = x_ref[...] + gt_ref[0] * (g1 * a_ref[slot] + g2 * b_ref[slot])


def _combine(x, route, gt, y, d1, d2, tc, tiles_per_group):
    m, d = x.shape
    n_tiles = m // tc
    ahead = lambda i: jnp.minimum(i, n_tiles - 1)
    done = lambda i: jnp.maximum(i - 1, 0)
    idx_spec = pl.BlockSpec((1, 1, tc), lambda i: (ahead(i), 0, 0), memory_space=pltpu.SMEM)
    _, r, _ = gt.shape
    return pl.pallas_call(
        _combine_kernel,
        grid=(n_tiles + 1,),
        in_specs=[idx_spec, idx_spec,
                  pl.BlockSpec((tc, d), lambda i: (done(i), 0)),
                  pl.BlockSpec((tc, LANES), lambda i: (done(i), 0)),
                  pl.BlockSpec((1, r, d), lambda i: (done(i) // tiles_per_group, 0, 0)),
                  pl.BlockSpec(memory_space=pl.ANY)],
        out_specs=pl.BlockSpec((tc, d), lambda i: (done(i), 0)),
        out_shape=jax.ShapeDtypeStruct((m, d), F32),
        scratch_shapes=[pltpu.VMEM((2, tc, d), F32), pltpu.VMEM((2, tc, d), F32),
                        pltpu.SemaphoreType.DMA((2, 2))],
        compiler_params=_params(1),
        name="moe_combine",
    )(d1.reshape(n_tiles, 1, tc), d2.reshape(n_tiles, 1, tc), x, route, gt, y)


def _moe_routed(h2, h2_rows, x, gt, p, j, tm_router, tm, tc, tiles_per_group):
    d = x.shape[1]
    n_exp, _, f_exp = p["moe_w_gate"].shape[1:]
    route, counts = _router(h2, p["router_w"][j], p["router_b"][j], tm_router)
    d1, d2, src, tile_expert, tile_rows, n_valid = _route_plan(route, counts, tm)
    xs = _gather_rows(h2_rows, src, tile_rows, tm)
    y = _expert_ffn(xs, p["moe_w_gate"].reshape(-1, d, f_exp), p["moe_w_up"].reshape(-1, d, f_exp),
                    p["moe_w_down"].reshape(-1, f_exp, d), j * n_exp, tile_expert, n_valid, tm)
    return _combine(x, route, gt, y, d1, d2, tc, tiles_per_group)


def _trunk(x, mods, start, pool_past, attend, p, batch, seq, tm, tm_out, tm_ffn, tm_moe,
           k_stack, v_stack):
    depth = p["w_in"].shape[0]
    m, d = x.shape
    tpg = (m // mods[0][0].shape[0]) // tm
    tpg_out = (m // mods[0][0].shape[0]) // tm_out
    tpg_ffn = (m // mods[0][0].shape[0]) // tm_ffn
    pools = []
    sh_m, sc_m = mods[0][0], mods[0][1]
    h = _norm(x, p["g_norm_mix"], 0, sc_m, sh_m, tm, tpg)
    for l in range(depth):
        sh_m, sc_m, gt_m, sh_f, sc_f, gt_f = mods[l]
        if l > 0:
            h = _norm(x, p["g_norm_mix"], l, sc_m, sh_m, tm, tpg)
        g_q = jnp.tile(p["g_q"][l], 2).reshape(1, LANES)
        g_k = jnp.tile(p["g_k"][l], 2).reshape(1, LANES)
        q = _inproj(h, p["w_in"], l, 0, "q", tm, g128=g_q)
        k_stack, k_bf = _inproj(h, p["w_in"], l, 1, "k", tm, g128=g_k, stack=k_stack)
        v_stack, v_bf = _inproj(h, p["w_in"], l, 2, "v", tm, stack=v_stack)
        u = _inproj(h, p["w_in"], l, 3, "u", tm)
        lam = (jnp.exp(jnp.sum(p["lambda_q1"][l] * p["lambda_k1"][l]))
               - jnp.exp(jnp.sum(p["lambda_q2"][l] * p["lambda_k2"][l]))
               + _lambda_init(l)).astype(F32).reshape(1)
        g_sub = p["g_sub"][l].reshape(1, LANES)
        a = attend(l, q, k_bf, v_bf, k_stack, v_stack, lam, g_sub)
        pool_out, pool_new = pool_past(l, u)
        pools.append(pool_new)
        j = l // 2
        routed = l % 2 == 1 and tm_moe is not None
        x, h2, *h2_rows = _outproj(a, pool_out, x, p["w_o"], l, gt_m, p["g_norm_ffn"], sc_f, sh_f,
                                   tm_out, tpg_out, routed)
        if l % 2 == 0:
            x = _ffn(h2, x, gt_f, p["dense_w_gate"], p["dense_w_up"], p["dense_w_down"], j,
                     tm_ffn, 256, tpg_ffn)
        elif routed:
            x = _moe_routed(h2, h2_rows[0], x, gt_f, p, j, tm, tm_moe, tm_out, tpg_out)
        else:
            route, _ = _router(h2, p["router_w"][j], p["router_b"][j], tm)
            n_exp, _, f_exp = p["moe_w_gate"].shape[1:]
            x = _all_experts(h2, x, gt_f, route, p["moe_w_gate"].reshape(-1, d, f_exp),
                             p["moe_w_up"].reshape(-1, d, f_exp),
                             p["moe_w_down"].reshape(-1, f_exp, d), j * n_exp, tm_ffn, LANES, tpg_ffn)
    return x, k_stack, v_stack, jnp.stack(pools)


def kernel(x_prompt, x_sample, cache_k, cache_v, state_pool, page_table, c_prompt, c_sample,
           g_norm_mix, g_norm_ffn, w_ada, b_ada, w_in, g_q, g_k,
           lambda_q1, lambda_k1, lambda_q2, lambda_k2, g_sub, w_pool, pool_scale, w_o,
           dense_w_gate, dense_w_up, dense_w_down,
           router_w, router_b, moe_w_gate, moe_w_up, moe_w_down):
    p = dict(g_norm_mix=g_norm_mix.reshape(-1, 1, g_norm_mix.shape[-1]),
             g_norm_ffn=g_norm_ffn.reshape(-1, 1, g_norm_ffn.shape[-1]),
             w_in=w_in, g_q=g_q, g_k=g_k,
             lambda_q1=lambda_q1, lambda_k1=lambda_k1, lambda_q2=lambda_q2, lambda_k2=lambda_k2,
             g_sub=g_sub, w_pool=w_pool, pool_scale=pool_scale, w_o=w_o,
             dense_w_gate=dense_w_gate, dense_w_up=dense_w_up, dense_w_down=dense_w_down,
             router_w=router_w, router_b=router_b,
             moe_w_gate=moe_w_gate, moe_w_up=moe_w_up, moe_w_down=moe_w_down)
    depth = w_in.shape[0]
    bp, tp, d = x_prompt.shape
    bs, ts, _ = x_sample.shape
    width = N_HEADS * LANES
    c_pool = pool_scale.shape[-1]

    n_c = bp + bs
    rows = -(-n_c // 16) * 16
    c_all = jnp.concatenate([c_prompt, c_sample, jnp.zeros((rows - n_c, d), F32)], axis=0)
    mod = _ada(c_all, w_ada, b_ada).reshape(depth, rows, 6, d)
    mods_p = [[mod[l, :bp, i][:, None, :] for i in range(6)] for l in range(depth)]
    mods_s = [[jnp.repeat(mod[l, bp:n_c, i], ts, axis=0)[None] for i in range(6)]
              for l in range(depth)]

    zeros_past = jnp.zeros((bp, POOL_STATE + 1, c_pool), F32)

    def attend_p(l, q, k_bf, v_bf, ks, vs, lam, gs):
        return _attn_prompt(q, k_bf, v_bf, l, lam, gs, bp, tp, 512)

    def pool_p(l, u):
        u3 = u.reshape(bp, tp, c_pool)
        out = _pool(u3, zeros_past, w_pool, pool_scale, l, 0)
        return out.reshape(bp * tp, c_pool), u3[:, tp - POOL_STATE:]

    kp0 = jnp.zeros((depth, bp * tp * N_HEADS, LANES), F32)
    vp0 = jnp.zeros((depth, bp * tp * N_HEADS, LANES), F32)
    y_p, k_p, v_p, pool_p_new = _trunk(x_prompt.reshape(bp * tp, d), mods_p, 0, pool_p, attend_p,
                                       p, bp, tp, 512, 256, 1024, 1024, kp0, vp0)

    past_len = page_table.shape[1] * PAGE_SIZE
    t_pad = 16

    def attend_s(l, q, k_bf, v_bf, ks, vs, lam, gs):
        return _attn_decode(q, ks, vs, cache_k, cache_v, page_table, l, lam, gs, ts, 8)

    def pool_s(l, u):
        u3 = u.reshape(bs, ts, c_pool)
        past16 = jnp.concatenate([jnp.zeros((bs, 1, c_pool), F32), state_pool[l]], axis=1)
        u_pad = jnp.concatenate([u3, jnp.zeros((bs, t_pad - ts, c_pool), F32)], axis=1)
        out = _pool(u_pad, past16, w_pool, pool_scale, l, past_len)
        new = jnp.concatenate([state_pool[l], u3], axis=1)[:, -POOL_STATE:]
        return out[:, :ts].reshape(bs * ts, c_pool), new

    ks0 = jnp.zeros((depth, bs * ts * N_HEADS, LANES), F32)
    vs0 = jnp.zeros((depth, bs * ts * N_HEADS, LANES), F32)
    y_s, k_s, v_s, pool_s_new = _trunk(x_sample.reshape(bs * ts, d), mods_s, past_len, pool_s,
                                       attend_s, p, bs, ts, bs * ts, bs * ts, bs * ts, None,
                                       ks0, vs0)

    return (y_p.reshape(bp, tp, d), y_s.reshape(bs, ts, d),
            k_p.reshape(depth, bp, tp, N_HEADS, LANES), v_p.reshape(depth, bp, tp, N_HEADS, LANES),
            pool_p_new,
            k_s.reshape(depth, bs, ts, N_HEADS, LANES), v_s.reshape(depth, bs, ts, N_HEADS, LANES),
            pool_s_new)
```

```python
import functools
import math

import jax
import jax.numpy as jnp
from jax import lax
from jax.experimental import pallas as pl
from jax.experimental.pallas import tpu as pltpu

F32 = jnp.float32
BF16 = jnp.bfloat16

N_HEADS = 8
HEAD_DIM_V = 128
HEAD_DIM_QK = 64
POOL_WINDOWS = (2, 4, 8, 16)
POOL_STATE = 15
PAGE_SIZE = 128
N_EXPERTS = 8
EPS = 1e-6
NEG_INF = -1e30
LANES = 128
VMEM_LIMIT = 56 * 1024 * 1024
COL_CHUNK = 512
Q_SCALE = HEAD_DIM_QK ** -0.5 * math.log2(math.e)


def _params(n_axes, vmem=VMEM_LIMIT):
    return pltpu.CompilerParams(dimension_semantics=("arbitrary",) * n_axes,
                                vmem_limit_bytes=vmem)


def _lambda_init(layer):
    return 0.8 - 0.6 * math.exp(-0.3 * layer)


def _mod_spec(arr, tiles_per_group):
    _, r, d = arr.shape
    return pl.BlockSpec((1, r, d), lambda m, *_: (m // tiles_per_group, 0, 0))


def _norm_mod(x, g, sc, sh):
    y = x * lax.rsqrt(jnp.mean(x * x, axis=-1, keepdims=True) + EPS)
    return (y * g) * (1.0 + sc) + sh


def _ada_kernel(c_ref, w_ref, b_ref, o_ref):
    c = c_ref[...]
    s = (c * jax.nn.sigmoid(c)).astype(BF16)
    o_ref[0] = jnp.dot(s, w_ref[0].astype(BF16), preferred_element_type=F32) + b_ref[0]


def _ada(c_all, w_ada, b_ada):
    depth, d, n = w_ada.shape
    rows = c_all.shape[0]
    tn = 1024
    return pl.pallas_call(
        _ada_kernel,
        grid=(depth, n // tn),
        in_specs=[pl.BlockSpec((rows, d), lambda l, j: (0, 0)),
                  pl.BlockSpec((1, d, tn), lambda l, j: (l, 0, j)),
                  pl.BlockSpec((1, 1, tn), lambda l, j: (l, 0, j))],
        out_specs=pl.BlockSpec((1, rows, tn), lambda l, j: (l, 0, j)),
        out_shape=jax.ShapeDtypeStruct((depth, rows, n), F32),
        compiler_params=_params(2),
        name="ada_mod",
    )(c_all, w_ada, b_ada.reshape(depth, 1, n))


def _norm_kernel(x_ref, g_ref, sc_ref, sh_ref, h_ref):
    h_ref[...] = _norm_mod(x_ref[...], g_ref[0], sc_ref[0], sh_ref[0]).astype(BF16)


def _norm(x, g_all, layer, sc, sh, tm, tiles_per_group):
    m, d = x.shape
    return pl.pallas_call(
        _norm_kernel,
        grid=(m // tm,),
        in_specs=[pl.BlockSpec((tm, d), lambda i: (i, 0)),
                  pl.BlockSpec((1, 1, d), lambda i: (layer, 0, 0)),
                  _mod_spec(sc, tiles_per_group), _mod_spec(sh, tiles_per_group)],
        out_specs=pl.BlockSpec((tm, d), lambda i: (i, 0)),
        out_shape=jax.ShapeDtypeStruct((m, d), BF16),
        compiler_params=_params(1),
        name="norm_mod",
    )(x, g_all, sc, sh)


def _qk_norm(p, g128, scale):
    lo = lax.broadcasted_iota(jnp.int32, (1, LANES), 1) < HEAD_DIM_QK
    sq = p * p
    s0 = jnp.sum(jnp.where(lo, sq, 0.0), axis=-1, keepdims=True)
    s1 = jnp.sum(jnp.where(lo, 0.0, sq), axis=-1, keepdims=True)
    inv0 = lax.rsqrt(s0 * (1.0 / HEAD_DIM_QK) + EPS)
    inv1 = lax.rsqrt(s1 * (1.0 / HEAD_DIM_QK) + EPS)
    y = (p * jnp.where(lo, inv0, inv1)) * g128
    return y if scale is None else y * scale


def _inproj_kernel(*refs, mode):
    normed = mode in ("q", "k")
    stacked = mode in ("k", "v")
    h_ref, w_ref = refs[:2]
    g_ref = refs[2] if normed else None
    wbf_ref = refs[-1]
    outs = refs[-3:-1] if stacked else refs[-2:-1]

    @pl.when(pl.program_id(0) == 0)
    def _():
        wbf_ref[...] = w_ref[0].astype(BF16)

    h = h_ref[...]
    scale = Q_SCALE if mode == "q" else None
    for c0 in range(0, wbf_ref.shape[1], COL_CHUNK):
        acc = jnp.dot(h, wbf_ref[:, c0:c0 + COL_CHUNK], preferred_element_type=F32)
        if mode == "u":
            outs[0][:, c0:c0 + COL_CHUNK] = acc
            continue
        for hh in range(COL_CHUNK // LANES):
            head = c0 // LANES + hh
            y = acc[:, hh * LANES:(hh + 1) * LANES]
            if normed:
                y = _qk_norm(y, g_ref[...], scale)
            if stacked:
                outs[0][0, pl.ds(head, h.shape[0], stride=N_HEADS), :] = y
                outs[1][:, head * LANES:(head + 1) * LANES] = y.astype(BF16)
            else:
                outs[0][:, head * LANES:(head + 1) * LANES] = y.astype(BF16)


def _inproj(h, w_in, layer, section, mode, tm, g128=None, stack=None):
    m, d = h.shape
    width = N_HEADS * LANES
    ins = [h, w_in]
    in_specs = [pl.BlockSpec((tm, d), lambda i: (i, 0)),
                pl.BlockSpec((1, d, width), lambda i: (layer, 0, section))]
    if g128 is not None:
        ins.append(g128)
        in_specs.append(pl.BlockSpec((1, LANES), lambda i: (0, 0)))
    flat_spec = pl.BlockSpec((tm, width), lambda i: (i, 0))
    aliases = {}
    if stack is not None:
        aliases = {len(ins): 0}
        ins.append(stack)
        in_specs.append(pl.BlockSpec(memory_space=pl.ANY))
        out_specs = [pl.BlockSpec((1, tm * N_HEADS, LANES), lambda i: (layer, i, 0)), flat_spec]
        out_shape = [jax.ShapeDtypeStruct(stack.shape, stack.dtype),
                     jax.ShapeDtypeStruct((m, width), BF16)]
    else:
        out_specs = [flat_spec]
        out_shape = [jax.ShapeDtypeStruct((m, width), BF16 if mode == "q" else F32)]
    n_in = len(ins)

    def body(*refs):
        if stack is not None:
            refs = refs[:n_in - 1] + refs[n_in:]
        _inproj_kernel(*refs, mode=mode)

    out = pl.pallas_call(
        body,
        grid=(m // tm,),
        in_specs=in_specs,
        out_specs=out_specs,
        out_shape=out_shape,
        scratch_shapes=[pltpu.VMEM((d, width), BF16)],
        input_output_aliases=aliases,
        compiler_params=_params(1),
        name="inproj_" + mode,
    )(*ins)
    return out if stack is not None else out[0]


def _sub_norm(o, g_sub, li):
    y = o * lax.rsqrt(jnp.mean(o * o, axis=-1, keepdims=True) + EPS)
    return (y * g_sub) * (1.0 - li)


def _attn_kernel(lam_ref, q_ref, k_ref, v_ref, g_ref, o_ref, vt_ref, s_ref, m_ref, l_ref, acc_ref,
                 *, tq, li):
    n_t = vt_ref.shape[0]
    for jt in range(n_t):
        vt_ref[jt] = v_ref[0, jt * tq:(jt + 1) * tq, :].astype(F32).T.astype(BF16)
    lo = lax.broadcasted_iota(jnp.int32, (1, LANES), 1) < HEAD_DIM_QK
    pairs = [(i, j) for i in range(n_t) for j in range(i + 1)]

    def logits(n):
        i, j = pairs[n]
        q = q_ref[0, i * tq:(i + 1) * tq, :]
        zero = jnp.zeros_like(q)
        kt = k_ref[0, j * tq:(j + 1) * tq, :]
        for mp, qm in enumerate((jnp.where(lo, q, zero), jnp.where(lo, zero, q))):
            s_ref[n % 2, mp] = lax.dot_general(kt, qm, (((1,), (1,)), ((), ())),
                                               preferred_element_type=F32)

    def consume(n):
        i, j = pairs[n]
        vt = vt_ref[j]
        for mp in range(2):
            s = s_ref[n % 2, mp]
            if j == i:
                key = lax.broadcasted_iota(jnp.int32, (tq, tq), 0)
                qry = lax.broadcasted_iota(jnp.int32, (tq, tq), 1)
                s = jnp.where(key <= qry, s, NEG_INF)
            m_old = m_ref[mp]
            m_new = jnp.maximum(m_old, jnp.max(s, axis=0, keepdims=True))
            alpha = jnp.exp2(m_old - m_new)
            p = jnp.exp2(s - m_new)
            l_ref[mp] = alpha * l_ref[mp] + jnp.sum(p, axis=0, keepdims=True)
            acc_ref[mp] = alpha * acc_ref[mp] + jnp.dot(vt, p.astype(BF16),
                                                        preferred_element_type=F32)
            m_ref[mp] = m_new

    logits(0)
    for n, (i, j) in enumerate(pairs):
        if j == 0:
            m_ref[...] = jnp.full(m_ref.shape, NEG_INF, F32)
            l_ref[...] = jnp.zeros(l_ref.shape, F32)
            acc_ref[...] = jnp.zeros(acc_ref.shape, F32)
        if n + 1 < len(pairs):
            logits(n + 1)
        consume(n)
        if j == i:
            o = acc_ref[0] / l_ref[0] - lam_ref[0] * (acc_ref[1] / l_ref[1])
            y = o * lax.rsqrt(jnp.mean(o * o, axis=0, keepdims=True) + EPS)
            o_ref[0, i * tq:(i + 1) * tq, :] = ((y.T * g_ref[...]) * (1.0 - li)).astype(BF16)


def _attn_prompt(q, k, v, layer, lam, g_sub, batch, seq, tq):
    width = N_HEADS * LANES
    q3, k3, v3 = (t.reshape(batch, seq, width) for t in (q, k, v))
    head_spec = pl.BlockSpec((1, seq, LANES), lambda b, h: (b, 0, h))
    out = pl.pallas_call(
        functools.partial(_attn_kernel, tq=tq, li=_lambda_init(layer)),
        grid=(batch, N_HEADS),
        in_specs=[pl.BlockSpec(memory_space=pltpu.SMEM),
                  head_spec, head_spec, head_spec,
                  pl.BlockSpec((1, LANES), lambda b, h: (0, 0))],
        out_specs=head_spec,
        out_shape=jax.ShapeDtypeStruct((batch, seq, width), BF16),
        scratch_shapes=[pltpu.VMEM((seq // tq, LANES, tq), BF16),
                        pltpu.VMEM((2, 2, tq, tq), F32),
                        pltpu.VMEM((2, 1, tq), F32), pltpu.VMEM((2, 1, tq), F32),
                        pltpu.VMEM((2, LANES, tq), F32)],
        compiler_params=_params(2),
        name="attn_prompt",
    )(lam, q3, k3, v3, g_sub)
    return out.reshape(batch * seq, width)


def _decode_kernel(pt_ref, lam_ref, q_ref, *refs, pages, n_new, li):
    k_refs = refs[:pages]
    v_refs = refs[pages:2 * pages]
    kn_ref, vn_ref, g_ref, o_ref, knp_ref, vnp_ref, m_ref, l_ref, acc_ref = refs[2 * pages:]
    b = pl.program_id(0)
    c = pl.program_id(1)
    last = pl.num_programs(1) - 1
    shift_q = (2 * n_new).bit_length() - 1

    @pl.when(jnp.logical_and(b == 0, c == 0))
    def _():
        knp_ref[...] = jnp.zeros(knp_ref.shape, F32)
        vnp_ref[...] = jnp.zeros(vnp_ref.shape, F32)

    @pl.when(c == 0)
    def _():
        m_ref[...] = jnp.full(m_ref.shape, NEG_INF, F32)
        l_ref[...] = jnp.zeros(l_ref.shape, F32)
        acc_ref[...] = jnp.zeros(acc_ref.shape, F32)

    q = q_ref[0]
    n_q = q.shape[0]
    page_rows = PAGE_SIZE * N_HEADS
    def own_head(n_cols):
        row = lax.broadcasted_iota(jnp.int32, (n_q, n_cols), 0)
        col = lax.broadcasted_iota(jnp.int32, (n_q, n_cols), 1)
        return row, col, jnp.bitwise_and(col, N_HEADS - 1) == lax.shift_right_logical(row, shift_q)

    def logits(k2d, keep):
        s = lax.dot_general(q, k2d.astype(BF16), (((1,), (1,)), ((), ())),
                            preferred_element_type=F32)
        return jnp.where(keep, s, NEG_INF)

    def update(s_list, v_list):
        m_old = m_ref[...]
        m_new = m_old
        for s in s_list:
            m_new = jnp.maximum(m_new, jnp.max(s, axis=-1, keepdims=True))
        alpha = jnp.exp2(m_old - m_new)
        l_new = alpha * l_ref[...]
        pv = None
        for s, v in zip(s_list, v_list):
            p = jnp.exp2(s - m_new)
            l_new = l_new + jnp.sum(p, axis=-1, keepdims=True)
            d = jnp.dot(p.astype(BF16), v, preferred_element_type=F32)
            pv = d if pv is None else pv + d
        acc_ref[...] = alpha * acc_ref[...] + pv
        l_ref[...] = l_new
        m_ref[...] = m_new

    s_list, v_list = [], []
    page_keep = own_head(page_rows)[2]
    for j in range(pages):
        s_list.append(logits(k_refs[j][0, 0].reshape(page_rows, LANES), page_keep))
        v_list.append(v_refs[j][0, 0].reshape(page_rows, LANES).astype(BF16))
    update(s_list, v_list)

    @pl.when(c == last)
    def _():
        n_rows = n_new * N_HEADS
        knp_ref[0:n_rows, :] = kn_ref[0, 0]
        vnp_ref[0:n_rows, :] = vn_ref[0, 0]
        row, col, keep = own_head(knp_ref.shape[0])
        tok = jnp.bitwise_and(row, n_new - 1)
        slot = lax.shift_right_logical(col, N_HEADS.bit_length() - 1)
        keep = jnp.logical_and(keep, slot <= tok)
        update([logits(knp_ref[...], keep)], [vnp_ref[...].astype(BF16)])

        o = acc_ref[...] / l_ref[...]
        o = o - lam_ref[0] * pltpu.roll(o, shift=n_q - n_new, axis=0)
        o_ref[0] = _sub_norm(o, g_ref[...], li)


def _attn_decode(q, k_stack, v_stack, cache_k, cache_v, page_table, layer, lam, g_sub,
                 n_new, pages):
    depth = cache_k.shape[0]
    batch, n_pages = page_table.shape
    assert n_new & (n_new - 1) == 0 and N_HEADS & (N_HEADS - 1) == 0
    n_q = N_HEADS * 2 * n_new
    qh = q.reshape(batch, n_new, N_HEADS, 1, LANES).transpose(0, 2, 3, 1, 4)
    in_map = (jnp.arange(LANES) // HEAD_DIM_QK)[None, :] == jnp.arange(2)[:, None]
    qm = jnp.where(in_map[None, None, :, None, :], qh, jnp.zeros_like(qh)).reshape(batch, n_q, LANES)
    kn = k_stack.reshape(depth, batch, n_new * N_HEADS, LANES)
    vn = v_stack.reshape(depth, batch, n_new * N_HEADS, LANES)

    def page_spec(j):
        return pl.BlockSpec((1, 1, PAGE_SIZE, N_HEADS, LANES),
                            lambda b, c, pt: (layer, pt[b, c * pages + j], 0, 0, 0))

    new_spec = pl.BlockSpec((1, 1, n_new * N_HEADS, LANES), lambda b, c, pt: (layer, b, 0, 0))
    grid_spec = pltpu.PrefetchScalarGridSpec(
        num_scalar_prefetch=1,
        grid=(batch, n_pages // pages),
        in_specs=([pl.BlockSpec(memory_space=pltpu.SMEM),
                   pl.BlockSpec((1, n_q, LANES), lambda b, c, pt: (b, 0, 0))]
                  + [page_spec(j) for j in range(pages)]
                  + [page_spec(j) for j in range(pages)]
                  + [new_spec, new_spec,
                     pl.BlockSpec((1, LANES), lambda b, c, pt: (0, 0))]),
        out_specs=pl.BlockSpec((1, n_q, LANES), lambda b, c, pt: (b, 0, 0)),
        scratch_shapes=[pltpu.VMEM((LANES, LANES), F32),
                        pltpu.VMEM((LANES, LANES), F32),
                        pltpu.VMEM((n_q, 1), F32),
                        pltpu.VMEM((n_q, 1), F32),
                        pltpu.VMEM((n_q, LANES), F32)],
    )
    out = pl.pallas_call(
        functools.partial(_decode_kernel, pages=pages, n_new=n_new, li=_lambda_init(layer)),
        grid_spec=grid_spec,
        out_shape=jax.ShapeDtypeStruct((batch, n_q, LANES), F32),
        compiler_params=_params(2),
        name="attn_decode",
    )(page_table, lam, qm, *([cache_k] * pages), *([cache_v] * pages), kn, vn, g_sub)
    out = out.reshape(batch, N_HEADS, 2, n_new, LANES)[:, :, 0]
    return out.transpose(0, 2, 1, 3).reshape(batch * n_new, N_HEADS * LANES).astype(BF16)


def _pool_kernel(u_ref, past_ref, w_ref, sc_ref, o_ref, full_ref, *, start, chunk):
    halo = past_ref.shape[1]
    t = u_ref.shape[1]
    group = u_ref.shape[2] // len(POOL_WINDOWS)
    full_ref[0:halo, :] = past_ref[0]
    full_ref[halo:halo + t, :] = u_ref[0]
    for c0 in range(0, t, chunk):
        pos = start + c0 + lax.broadcasted_iota(jnp.int32, (chunk, 1), 0)
        for g, w in enumerate(POOL_WINDOWS):
            cols = slice(g * group, (g + 1) * group)
            cur = full_ref[halo + c0:halo + c0 + chunk, cols]
            win = cur
            for i in range(1, w):
                win = win + full_ref[halo + c0 - i:halo + c0 - i + chunk, cols]
            count = jnp.minimum(w, pos + 1).astype(F32)
            pooled = (win / count - cur).astype(BF16)
            y = jnp.dot(pooled, w_ref[0, g].astype(BF16), preferred_element_type=F32)
            o_ref[0, c0:c0 + chunk, cols] = (y * sc_ref[0, :, cols]).astype(BF16)


def _pool(u3, past16, w_pool, pool_scale, layer, start):
    batch, t, c = u3.shape
    depth = w_pool.shape[0]
    group = c // len(POOL_WINDOWS)
    chunk = min(t, 256)
    return pl.pallas_call(
        functools.partial(_pool_kernel, start=start, chunk=chunk),
        grid=(batch,),
        in_specs=[pl.BlockSpec((1, t, c), lambda b: (b, 0, 0)),
                  pl.BlockSpec((1, POOL_STATE + 1, c), lambda b: (b, 0, 0)),
                  pl.BlockSpec((1, len(POOL_WINDOWS), group, group), lambda b: (layer, 0, 0, 0)),
                  pl.BlockSpec((1, 1, c), lambda b: (layer, 0, 0))],
        out_specs=pl.BlockSpec((1, t, c), lambda b: (b, 0, 0)),
        out_shape=jax.ShapeDtypeStruct((batch, t, c), BF16),
        scratch_shapes=[pltpu.VMEM((POOL_STATE + 1 + t, c), F32)],
        compiler_params=_params(1),
        name="pool_mix",
    )(u3, past16, w_pool, pool_scale.reshape(depth, 1, c))


def _outproj_kernel(a_ref, p_ref, x_ref, w_ref, gt_ref, g_ref, sc_ref, sh_ref,
                    x1_ref, h2_ref, h2_rows_ref, wbf_ref):
    @pl.when(pl.program_id(0) == 0)
    def _():
        wbf_ref[...] = w_ref[0].astype(BF16)

    half = a_ref.shape[1]
    a = a_ref[...]
    p = p_ref[...]
    for c0 in range(0, x_ref.shape[1], COL_CHUNK):
        cols = slice(c0, c0 + COL_CHUNK)
        y = jnp.dot(a, wbf_ref[0:half, cols], preferred_element_type=F32)
        y = y + jnp.dot(p, wbf_ref[half:, cols], preferred_element_type=F32)
        x1_ref[:, cols] = x_ref[:, cols] + gt_ref[0, :, cols] * y
    h2 = _norm_mod(x1_ref[...], g_ref[0], sc_ref[0], sh_ref[0])
    h2_ref[...] = h2.astype(BF16)
    if h2_rows_ref is not None:
        tm, chunks = h2.shape[0], h2.shape[1] // LANES
        for c in range(chunks):
            h2_rows_ref[pl.ds(c, tm, stride=chunks), :] = h2[:, c * LANES:(c + 1) * LANES]


def _outproj(a, p, x, w_o, layer, gt, g_ffn, sc, sh, tm, tiles_per_group, row_major):
    m, d = x.shape
    half = a.shape[1]
    out_specs = [pl.BlockSpec((tm, d), lambda i: (i, 0)), pl.BlockSpec((tm, d), lambda i: (i, 0))]
    out_shape = [jax.ShapeDtypeStruct((m, d), F32), jax.ShapeDtypeStruct((m, d), BF16)]
    chunks = d // LANES
    if row_major:
        out_specs.append(pl.BlockSpec((tm * chunks, LANES), lambda i: (i, 0)))
        out_shape.append(jax.ShapeDtypeStruct((m * chunks, LANES), F32))

    def body(*refs):
        if row_major:
            _outproj_kernel(*refs)
        else:
            _outproj_kernel(*refs[:-1], None, refs[-1])

    outs = pl.pallas_call(
        body,
        grid=(m // tm,),
        in_specs=[pl.BlockSpec((tm, half), lambda i: (i, 0)),
                  pl.BlockSpec((tm, half), lambda i: (i, 0)),
                  pl.BlockSpec((tm, d), lambda i: (i, 0)),
                  pl.BlockSpec((1, d, d), lambda i: (layer, 0, 0),
                               pipeline_mode=pl.Buffered(1)),
                  _mod_spec(gt, tiles_per_group),
                  pl.BlockSpec((1, 1, d), lambda i: (layer, 0, 0)),
                  _mod_spec(sc, tiles_per_group), _mod_spec(sh, tiles_per_group)],
        out_specs=out_specs,
        out_shape=out_shape,
        scratch_shapes=[pltpu.VMEM((d, d), BF16)],
        compiler_params=_params(1),
        name="outproj",
    )(a, p, x, w_o, gt, g_ffn, sc, sh)
    if row_major:
        return outs[0], outs[1], outs[2].reshape(m, chunks, LANES)
    return outs


def _swiglu_act(h_ref, wg_ref, wu_ref):
    h = h_ref[...]
    g = jnp.dot(h, wg_ref[0].astype(BF16), preferred_element_type=F32)
    u = jnp.dot(h, wu_ref[0].astype(BF16), preferred_element_type=F32)
    return ((g * jax.nn.sigmoid(g)) * u).astype(BF16)


def _ffn_kernel(h_ref, x_ref, gt_ref, wg_ref, wu_ref, wd_ref, o_ref):
    @pl.when(pl.program_id(1) == 0)
    def _():
        o_ref[...] = x_ref[...]

    act = _swiglu_act(h_ref, wg_ref, wu_ref)
    for c0 in range(0, o_ref.shape[1], COL_CHUNK):
        cols = slice(c0, c0 + COL_CHUNK)
        y = jnp.dot(act, wd_ref[0, :, cols].astype(BF16), preferred_element_type=F32)
        o_ref[:, cols] += gt_ref[0, :, cols] * y


def _ffn(h, x, gt, wg, wu, wd, widx, tm, tf, tiles_per_group):
    m, d = x.shape
    f = wg.shape[2]
    return pl.pallas_call(
        _ffn_kernel,
        grid=(m // tm, f // tf),
        in_specs=[pl.BlockSpec((tm, d), lambda i, j: (i, 0)),
                  pl.BlockSpec((tm, d), lambda i, j: (i, 0), pipeline_mode=pl.Buffered(1)),
                  _mod_spec(gt, tiles_per_group),
                  pl.BlockSpec((1, d, tf), lambda i, j: (widx, 0, j)),
                  pl.BlockSpec((1, d, tf), lambda i, j: (widx, 0, j)),
                  pl.BlockSpec((1, tf, d), lambda i, j: (widx, j, 0))],
        out_specs=pl.BlockSpec((tm, d), lambda i, j: (i, 0)),
        out_shape=jax.ShapeDtypeStruct((m, d), F32),
        compiler_params=_params(2),
        name="ffn",
    )(h, x, gt, wg, wu, wd)


def _all_experts_kernel(h_ref, x_ref, gt_ref, r_ref, wg_ref, wu_ref, wd_ref, o_ref):
    e = pl.program_id(1)

    @pl.when(jnp.logical_and(e == 0, pl.program_id(2) == 0))
    def _():
        o_ref[...] = x_ref[...]

    h = h_ref[...]
    g = jnp.dot(h, wg_ref[0].astype(BF16), preferred_element_type=F32)
    u = jnp.dot(h, wu_ref[0].astype(BF16), preferred_element_type=F32)
    act = ((g * jax.nn.sigmoid(g)) * u).astype(BF16)
    route = r_ref[...]
    lane = lax.broadcasted_iota(jnp.int32, route.shape, 1)
    cw = jnp.sum(jnp.where(lane == e, route, 0.0), axis=-1, keepdims=True)
    for c0 in range(0, o_ref.shape[1], COL_CHUNK):
        cols = slice(c0, c0 + COL_CHUNK)
        y = jnp.dot(act, wd_ref[0, :, cols].astype(BF16), preferred_element_type=F32)
        o_ref[:, cols] += (gt_ref[0, :, cols] * cw) * y


def _all_experts(h, x, gt, route, wg, wu, wd, layer_base, tm, tf, tiles_per_group):
    m, d = x.shape
    f = wg.shape[2]
    return pl.pallas_call(
        _all_experts_kernel,
        grid=(m // tm, N_EXPERTS, f // tf),
        in_specs=[pl.BlockSpec((tm, d), lambda i, e, j: (i, 0)),
                  pl.BlockSpec((tm, d), lambda i, e, j: (i, 0)),
                  _mod_spec(gt, tiles_per_group),
                  pl.BlockSpec((tm, LANES), lambda i, e, j: (i, 0)),
                  pl.BlockSpec((1, d, tf), lambda i, e, j: (layer_base + e, 0, j)),
                  pl.BlockSpec((1, d, tf), lambda i, e, j: (layer_base + e, 0, j)),
                  pl.BlockSpec((1, tf, d), lambda i, e, j: (layer_base + e, j, 0))],
        out_specs=pl.BlockSpec((tm, d), lambda i, e, j: (i, 0)),
        out_shape=jax.ShapeDtypeStruct((m, d), F32),
        compiler_params=_params(3),
        name="moe_all_experts",
    )(h, x, gt, route, wg, wu, wd)


ROUTE_E1, ROUTE_E2, ROUTE_POS1, ROUTE_POS2, ROUTE_G1, ROUTE_G2 = range(N_EXPERTS, N_EXPERTS + 6)


def _router_kernel(h_ref, whi_ref, wlo_ref, b_ref, o_ref, cnt_ref, carry_ref):
    @pl.when(pl.program_id(0) == 0)
    def _():
        carry_ref[...] = jnp.zeros(carry_ref.shape, F32)

    h = h_ref[...].astype(BF16)
    logits = (jnp.dot(h, whi_ref[...], preferred_element_type=F32)
              + jnp.dot(h, wlo_ref[...], preferred_element_type=F32) + b_ref[...])
    lane = lax.broadcasted_iota(jnp.int32, logits.shape, 1).astype(F32)
    logits = jnp.where(lane < N_EXPERTS, logits, NEG_INF)
    v1 = jnp.max(logits, axis=-1, keepdims=True)
    i1 = jnp.min(jnp.where(logits == v1, lane, float(LANES)), axis=-1, keepdims=True)
    rest = jnp.where(lane == i1, NEG_INF, logits)
    v2 = jnp.max(rest, axis=-1, keepdims=True)
    i2 = jnp.min(jnp.where(rest == v2, lane, float(LANES)), axis=-1, keepdims=True)
    e2 = jnp.exp(v2 - v1)
    g1 = 1.0 / (1.0 + e2)
    g2 = e2 / (1.0 + e2)

    tm = h.shape[0]
    sel = jnp.where(jnp.logical_or(lane == i1, lane == i2), 1.0, 0.0)
    earlier = (lax.broadcasted_iota(jnp.int32, (tm, tm), 1)
               < lax.broadcasted_iota(jnp.int32, (tm, tm), 0))
    rank = jnp.dot(jnp.where(earlier, 1.0, 0.0).astype(BF16), sel.astype(BF16),
                   preferred_element_type=F32) + carry_ref[...]
    pos1 = jnp.sum(jnp.where(lane == i1, rank, 0.0), axis=-1, keepdims=True)
    pos2 = jnp.sum(jnp.where(lane == i2, rank, 0.0), axis=-1, keepdims=True)
    total = carry_ref[...] + jnp.sum(sel, axis=0, keepdims=True)
    carry_ref[...] = total
    cnt_ref[...] = total

    out = jnp.where(lane == i1, g1, 0.0) + jnp.where(lane == i2, g2, 0.0)
    for idx, val in ((ROUTE_E1, i1), (ROUTE_E2, i2), (ROUTE_POS1, pos1), (ROUTE_POS2, pos2),
                     (ROUTE_G1, g1), (ROUTE_G2, g2)):
        out = out + jnp.where(lane == float(idx), val, 0.0)
    o_ref[...] = out


def _router(h, router_w, router_b, tm):
    m, d = h.shape
    pad = LANES - N_EXPERTS
    w = jnp.pad(router_w, ((0, 0), (0, pad)))
    whi = w.astype(BF16)
    wlo = (w - whi.astype(F32)).astype(BF16)
    b = jnp.pad(router_b, (0, pad)).reshape(1, LANES)
    return pl.pallas_call(
        _router_kernel,
        grid=(m // tm,),
        in_specs=[pl.BlockSpec((tm, d), lambda i: (i, 0)),
                  pl.BlockSpec((d, LANES), lambda i: (0, 0)),
                  pl.BlockSpec((d, LANES), lambda i: (0, 0)),
                  pl.BlockSpec((1, LANES), lambda i: (0, 0))],
        out_specs=[pl.BlockSpec((tm, LANES), lambda i: (i, 0)),
                   pl.BlockSpec((1, LANES), lambda i: (0, 0))],
        out_shape=[jax.ShapeDtypeStruct((m, LANES), F32),
                   jax.ShapeDtypeStruct((1, LANES), F32)],
        scratch_shapes=[pltpu.VMEM((1, LANES), F32)],
        compiler_params=_params(1),
        name="router",
    )(h, whi, wlo, b)


def _route_plan(route, counts, tm):
    m = route.shape[0]
    n_tiles = 2 * m // tm + N_EXPERTS
    col = lambda i: route[:, i].astype(jnp.int32)
    cnt = counts[0, :N_EXPERTS].astype(jnp.int32)
    tiles = (cnt + tm - 1) // tm
    tile_end = jnp.cumsum(tiles)
    row_start = (tile_end - tiles) * tm
    d1 = row_start[col(ROUTE_E1)] + col(ROUTE_POS1)
    d2 = row_start[col(ROUTE_E2)] + col(ROUTE_POS2)
    token = jnp.arange(m, dtype=jnp.int32)
    src = jnp.zeros((n_tiles * tm,), jnp.int32).at[d1].set(token).at[d2].set(token)
    tile_id = jnp.arange(n_tiles, dtype=jnp.int32)
    tile_expert = jnp.minimum(jnp.searchsorted(tile_end, tile_id, side="right"),
                              N_EXPERTS - 1).astype(jnp.int32)
    tile_rows = jnp.clip(cnt[tile_expert] - (tile_id - (tile_end - tiles)[tile_expert]) * tm, 0, tm)
    tile_rows = jnp.where(tile_id < tile_end[-1], tile_rows, 0).astype(jnp.int32)
    return d1, d2, src, tile_expert, tile_rows, tile_end[-1:].astype(jnp.int32)


def _row_copy(src_hbm, row, dst_vmem, slot, sem):
    return pltpu.make_async_copy(src_hbm.at[pl.ds(row, 1), :], dst_vmem.at[pl.ds(slot, 1), :], sem)


def _slab_copy(src_hbm, row, dst_vmem, slot, sem):
    chunks = src_hbm.shape[1]
    return pltpu.make_async_copy(
        src_hbm.at[row], dst_vmem.at[pl.ds(pl.multiple_of(slot * chunks, chunks), chunks), :], sem)


def _gather_kernel(rows_ref, src_ref, h_hbm, o_ref, buf_ref, sems):
    i = pl.program_id(0)
    n_tiles = pl.num_programs(0) - 1
    tm = o_ref.shape[0]
    chunks = h_hbm.shape[1]

    @pl.when(i == 0)
    def _():
        buf_ref[...] = jnp.zeros(buf_ref.shape, F32)

    def copy(slot, r, row):
        return _slab_copy(h_hbm, row, buf_ref.at[slot], r, sems.at[slot])

    def pairs(tile):
        return lax.shift_right_logical(rows_ref[tile] + 1, 1)

    @pl.when(i < n_tiles)
    def _():
        slot = jnp.bitwise_and(i, 1)

        def issue(k, c):
            for prio in range(2):
                r = 2 * k + prio
                copy(slot, r, src_ref[0, 0, r]).start(priority=prio)
            return c

        lax.fori_loop(0, pairs(i), issue, 0)

    @pl.when(i > 0)
    def _():
        slot = jnp.bitwise_and(i - 1, 1)

        def wait(k, c):
            for prio in range(2):
                copy(slot, 2 * k + prio, 0).wait()
            return c

        lax.fori_loop(0, pairs(i - 1), wait, 0)
        for c in range(chunks):
            o_ref[:, c * LANES:(c + 1) * LANES] = (
                buf_ref[slot, pl.ds(c, tm, stride=chunks), :].astype(BF16))


def _gather_rows(h_rows, src, tile_rows, tm):
    _, chunks, _ = h_rows.shape
    n_rows = src.shape[0]
    n_tiles = n_rows // tm
    grid_spec = pltpu.PrefetchScalarGridSpec(
        num_scalar_prefetch=1,
        grid=(n_tiles + 1,),
        in_specs=[pl.BlockSpec((1, 1, tm), lambda i, nr: (jnp.minimum(i, n_tiles - 1), 0, 0),
                               memory_space=pltpu.SMEM),
                  pl.BlockSpec(memory_space=pl.ANY)],
        out_specs=pl.BlockSpec((tm, chunks * LANES), lambda i, nr: (jnp.maximum(i - 1, 0), 0)),
        scratch_shapes=[pltpu.VMEM((2, tm * chunks, LANES), F32), pltpu.SemaphoreType.DMA((2,))],
    )
    return pl.pallas_call(
        _gather_kernel,
        grid_spec=grid_spec,
        out_shape=jax.ShapeDtypeStruct((n_rows, chunks * LANES), BF16),
        compiler_params=_params(1),
        name="moe_gather",
    )(tile_rows, src.reshape(n_rows // tm, 1, tm), h_rows)


def _expert_ffn_kernel(te_ref, nv_ref, h_ref, wg_ref, wu_ref, wd_ref, wgt_ref, wut_ref, wdt_ref,
                       o_ref, *, n_main):
    i = pl.program_id(0)
    j = pl.program_id(1)

    @pl.when(j == 0)
    def _():
        o_ref[...] = jnp.zeros(o_ref.shape, F32)

    def accumulate(wg, wu, wd):
        act = _swiglu_act(h_ref, wg, wu)
        for c0 in range(0, o_ref.shape[1], COL_CHUNK):
            cols = slice(c0, c0 + COL_CHUNK)
            o_ref[:, cols] += jnp.dot(act, wd[0, :, cols].astype(BF16),
                                      preferred_element_type=F32)

    in_use = i < nv_ref[0]

    @pl.when(jnp.logical_and(in_use, j < n_main))
    def _():
        accumulate(wg_ref, wu_ref, wd_ref)

    @pl.when(jnp.logical_and(in_use, j == n_main))
    def _():
        accumulate(wgt_ref, wut_ref, wdt_ref)


def _expert_ffn(xs, wg, wu, wd, layer_base, tile_expert, n_valid, tm):
    n_rows, d = xs.shape
    f = wg.shape[2]
    wide, tail = 2 * LANES, LANES
    n_main = f // wide
    assert f == n_main * wide + tail
    tail_blk = f // tail - 1

    def main_blk(i, j, nv):
        return jnp.where(i < nv[0], jnp.minimum(j, n_main - 1), 0)

    grid_spec = pltpu.PrefetchScalarGridSpec(
        num_scalar_prefetch=2,
        grid=(n_rows // tm, n_main + 1),
        in_specs=[
            pl.BlockSpec((tm, d), lambda i, j, te, nv: (i, 0)),
            pl.BlockSpec((1, d, wide), lambda i, j, te, nv: (layer_base + te[i], 0, main_blk(i, j, nv))),
            pl.BlockSpec((1, d, wide), lambda i, j, te, nv: (layer_base + te[i], 0, main_blk(i, j, nv))),
            pl.BlockSpec((1, wide, d), lambda i, j, te, nv: (layer_base + te[i], main_blk(i, j, nv), 0)),
            pl.BlockSpec((1, d, tail), lambda i, j, te, nv: (layer_base + te[i], 0, tail_blk)),
            pl.BlockSpec((1, d, tail), lambda i, j, te, nv: (layer_base + te[i], 0, tail_blk)),
            pl.BlockSpec((1, tail, d), lambda i, j, te, nv: (layer_base + te[i], tail_blk, 0)),
        ],
        out_specs=pl.BlockSpec((tm, d), lambda i, j, te, nv: (i, 0)),
    )
    return pl.pallas_call(
        functools.partial(_expert_ffn_kernel, n_main=n_main),
        grid_spec=grid_spec,
        out_shape=jax.ShapeDtypeStruct((n_rows, d), F32),
        compiler_params=_params(2),
        name="moe_ffn",
    )(tile_expert, n_valid, xs, wg, wu, wd, wg, wu, wd)


def _combine_kernel(d1_ref, d2_ref, x_ref, r_ref, gt_ref, y_hbm, o_ref, a_ref, b_ref, sems):
    i = pl.program_id(0)
    n_tiles = pl.num_programs(0) - 1
    tc = a_ref.shape[1]

    def copies(slot, r, row1, row2):
        return (_row_copy(y_hbm, row1, a_ref.at[slot], r, sems.at[0, slot]),
                _row_copy(y_hbm, row2, b_ref.at[slot], r, sems.at[1, slot]))

    @pl.when(i < n_tiles)
    def _():
        slot = jnp.bitwise_and(i, 1)

        def issue(r, c):
            for prio, cp in enumerate(copies(slot, r, d1_ref[0, 0, r], d2_ref[0, 0, r])):
                cp.start(priority=prio)
            return c

        lax.fori_loop(0, tc, issue, 0, unroll=4)

    @pl.when(i > 0)
    def _():
        slot = jnp.bitwise_and(i - 1, 1)

        def wait(r, c):
            for cp in copies(slot, r, 0, 0):
                cp.wait()
            return c

        lax.fori_loop(0, tc, wait, 0, unroll=4)
        route = r_ref[...]
        lane = lax.broadcasted_iota(jnp.int32, route.shape, 1)
        g1 = jnp.sum(jnp.where(lane == ROUTE_G1, route, 0.0), axis=-1, keepdims=True)
        g2 = jnp.sum(jnp.where(lane == ROUTE_G2, route, 0.0), axis=-1, keepdims=True)
        o_ref[...] = x_ref[...] + gt_ref[0] * (g1 * a_ref[slot] + g2 * b_ref[slot])


def _combine(x, route, gt, y, d1, d2, tc, tiles_per_group):
    m, d = x.shape
    n_tiles = m // tc
    ahead = lambda i: jnp.minimum(i, n_tiles - 1)
    done = lambda i: jnp.maximum(i - 1, 0)
    idx_spec = pl.BlockSpec((1, 1, tc), lambda i: (ahead(i), 0, 0), memory_space=pltpu.SMEM)
    _, r, _ = gt.shape
    return pl.pallas_call(
        _combine_kernel,
        grid=(n_tiles + 1,),
        in_specs=[idx_spec, idx_spec,
                  pl.BlockSpec((tc, d), lambda i: (done(i), 0)),
                  pl.BlockSpec((tc, LANES), lambda i: (done(i), 0)),
                  pl.BlockSpec((1, r, d), lambda i: (done(i) // tiles_per_group, 0, 0)),
                  pl.BlockSpec(memory_space=pl.ANY)],
        out_specs=pl.BlockSpec((tc, d), lambda i: (done(i), 0)),
        out_shape=jax.ShapeDtypeStruct((m, d), F32),
        scratch_shapes=[pltpu.VMEM((2, tc, d), F32), pltpu.VMEM((2, tc, d), F32),
                        pltpu.SemaphoreType.DMA((2, 2))],
        compiler_params=_params(1),
        name="moe_combine",
    )(d1.reshape(n_tiles, 1, tc), d2.reshape(n_tiles, 1, tc), x, route, gt, y)


def _moe_routed(h2, h2_rows, x, gt, p, j, tm_router, tm, tc, tiles_per_group):
    d = x.shape[1]
    n_exp, _, f_exp = p["moe_w_gate"].shape[1:]
    route, counts = _router(h2, p["router_w"][j], p["router_b"][j], tm_router)
    d1, d2, src, tile_expert, tile_rows, n_valid = _route_plan(route, counts, tm)
    xs = _gather_rows(h2_rows, src, tile_rows, tm)
    y = _expert_ffn(xs, p["moe_w_gate"].reshape(-1, d, f_exp), p["moe_w_up"].reshape(-1, d, f_exp),
                    p["moe_w_down"].reshape(-1, f_exp, d), j * n_exp, tile_expert, n_valid, tm)
    return _combine(x, route, gt, y, d1, d2, tc, tiles_per_group)


def _trunk(x, mods, start, pool_past, attend, p, batch, seq, tm, tm_out, tm_ffn, tm_moe,
           k_stack, v_stack):
    depth = p["w_in"].shape[0]
    m, d = x.shape
    tpg = (m // mods[0][0].shape[0]) // tm
    tpg_out = (m // mods[0][0].shape[0]) // tm_out
    tpg_ffn = (m // mods[0][0].shape[0]) // tm_ffn
    pools = []
    sh_m, sc_m = mods[0][0], mods[0][1]
    h = _norm(x, p["g_norm_mix"], 0, sc_m, sh_m, tm, tpg)
    for l in range(depth):
        sh_m, sc_m, gt_m, sh_f, sc_f, gt_f = mods[l]
        if l > 0:
            h = _norm(x, p["g_norm_mix"], l, sc_m, sh_m, tm, tpg)
        g_q = jnp.tile(p["g_q"][l], 2).reshape(1, LANES)
        g_k = jnp.tile(p["g_k"][l], 2).reshape(1, LANES)
        q = _inproj(h, p["w_in"], l, 0, "q", tm, g128=g_q)
        k_stack, k_bf = _inproj(h, p["w_in"], l, 1, "k", tm, g128=g_k, stack=k_stack)
        v_stack, v_bf = _inproj(h, p["w_in"], l, 2, "v", tm, stack=v_stack)
        u = _inproj(h, p["w_in"], l, 3, "u", tm)
        lam = (jnp.exp(jnp.sum(p["lambda_q1"][l] * p["lambda_k1"][l]))
               - jnp.exp(jnp.sum(p["lambda_q2"][l] * p["lambda_k2"][l]))
               + _lambda_init(l)).astype(F32).reshape(1)
        g_sub = p["g_sub"][l].reshape(1, LANES)
        a = attend(l, q, k_bf, v_bf, k_stack, v_stack, lam, g_sub)
        pool_out, pool_new = pool_past(l, u)
        pools.append(pool_new)
        j = l // 2
        routed = l % 2 == 1 and tm_moe is not None
        x, h2, *h2_rows = _outproj(a, pool_out, x, p["w_o"], l, gt_m, p["g_norm_ffn"], sc_f, sh_f,
                                   tm_out, tpg_out, routed)
        if l % 2 == 0:
            x = _ffn(h2, x, gt_f, p["dense_w_gate"], p["dense_w_up"], p["dense_w_down"], j,
                     tm_ffn, 256, tpg_ffn)
        elif routed:
            x = _moe_routed(h2, h2_rows[0], x, gt_f, p, j, tm, tm_moe, tm_out, tpg_out)
        else:
            route, _ = _router(h2, p["router_w"][j], p["router_b"][j], tm)
            n_exp, _, f_exp = p["moe_w_gate"].shape[1:]
            x = _all_experts(h2, x, gt_f, route, p["moe_w_gate"].reshape(-1, d, f_exp),
                             p["moe_w_up"].reshape(-1, d, f_exp),
                             p["moe_w_down"].reshape(-1, f_exp, d), j * n_exp, tm_ffn, LANES, tpg_ffn)
    return x, k_stack, v_stack, jnp.stack(pools)


def kernel(x_prompt, x_sample, cache_k, cache_v, state_pool, page_table, c_prompt, c_sample,
           g_norm_mix, g_norm_ffn, w_ada, b_ada, w_in, g_q, g_k,
           lambda_q1, lambda_k1, lambda_q2, lambda_k2, g_sub, w_pool, pool_scale, w_o,
           dense_w_gate, dense_w_up, dense_w_down,
           router_w, router_b, moe_w_gate, moe_w_up, moe_w_down):
    p = dict(g_norm_mix=g_norm_mix.reshape(-1, 1, g_norm_mix.shape[-1]),
             g_norm_ffn=g_norm_ffn.reshape(-1, 1, g_norm_ffn.shape[-1]),
             w_in=w_in, g_q=g_q, g_k=g_k,
             lambda_q1=lambda_q1, lambda_k1=lambda_k1, lambda_q2=lambda_q2, lambda_k2=lambda_k2,
             g_sub=g_sub, w_pool=w_pool, pool_scale=pool_scale, w_o=w_o,
             dense_w_gate=dense_w_gate, dense_w_up=dense_w_up, dense_w_down=dense_w_down,
             router_w=router_w, router_b=router_b,
             moe_w_gate=moe_w_gate, moe_w_up=moe_w_up, moe_w_down=moe_w_down)
    depth = w_in.shape[0]
    bp, tp, d = x_prompt.shape
    bs, ts, _ = x_sample.shape
    width = N_HEADS * LANES
    c_pool = pool_scale.shape[-1]

    n_c = bp + bs
    rows = -(-n_c // 16) * 16
    c_all = jnp.concatenate([c_prompt, c_sample, jnp.zeros((rows - n_c, d), F32)], axis=0)
    mod = _ada(c_all, w_ada, b_ada).reshape(depth, rows, 6, d)
    mods_p = [[mod[l, :bp, i][:, None, :] for i in range(6)] for l in range(depth)]
    mods_s = [[jnp.repeat(mod[l, bp:n_c, i], ts, axis=0)[None] for i in range(6)]
              for l in range(depth)]

    zeros_past = jnp.zeros((bp, POOL_STATE + 1, c_pool), F32)

    def attend_p(l, q, k_bf, v_bf, ks, vs, lam, gs):
        return _attn_prompt(q, k_bf, v_bf, l, lam, gs, bp, tp, 512)

    def pool_p(l, u):
        u3 = u.reshape(bp, tp, c_pool)
        out = _pool(u3, zeros_past, w_pool, pool_scale, l, 0)
        return out.reshape(bp * tp, c_pool), u3[:, tp - POOL_STATE:]

    kp0 = jnp.zeros((depth, bp * tp * N_HEADS, LANES), F32)
    vp0 = jnp.zeros((depth, bp * tp * N_HEADS, LANES), F32)
    y_p, k_p, v_p, pool_p_new = _trunk(x_prompt.reshape(bp * tp, d), mods_p, 0, pool_p, attend_p,
                                       p, bp, tp, 512, 256, 1024, 1024, kp0, vp0)

    past_len = page_table.shape[1] * PAGE_SIZE
    t_pad = 16

    def attend_s(l, q, k_bf, v_bf, ks, vs, lam, gs):
        return _attn_decode(q, ks, vs, cache_k, cache_v, page_table, l, lam, gs, ts, 8)

    def pool_s(l, u):
        u3 = u.reshape(bs, ts, c_pool)
        past16 = jnp.concatenate([jnp.zeros((bs, 1, c_pool), F32), state_pool[l]], axis=1)
        u_pad = jnp.concatenate([u3, jnp.zeros((bs, t_pad - ts, c_pool), F32)], axis=1)
        out = _pool(u_pad, past16, w_pool, pool_scale, l, past_len)
        new = jnp.concatenate([state_pool[l], u3], axis=1)[:, -POOL_STATE:]
        return out[:, :ts].reshape(bs * ts, c_pool), new

    ks0 = jnp.zeros((depth, bs * ts * N_HEADS, LANES), F32)
    vs0 = jnp.zeros((depth, bs * ts * N_HEADS, LANES), F32)
    y_s, k_s, v_s, pool_s_new = _trunk(x_sample.reshape(bs * ts, d), mods_s, past_len, pool_s,
                                       attend_s, p, bs, ts, bs * ts, bs * ts, bs * ts, None,
                                       ks0, vs0)

    return (y_p.reshape(bp, tp, d), y_s.reshape(bs, ts, d),
            k_p.reshape(depth, bp, tp, N_HEADS, LANES), v_p.reshape(depth, bp, tp, N_HEADS, LANES),
            pool_p_new,
            k_s.reshape(depth, bs, ts, N_HEADS, LANES), v_s.reshape(depth, bs, ts, N_HEADS, LANES),
            pool_s_new)
```

```python
import functools
import math

import jax
import jax.numpy as jnp
from jax import lax
from jax.experimental import pallas as pl
from jax.experimental.pallas import tpu as pltpu

F32 = jnp.float32
BF16 = jnp.bfloat16

N_HEADS = 8
HEAD_DIM_V = 128
HEAD_DIM_QK = 64
POOL_WINDOWS = (2, 4, 8, 16)
POOL_STATE = 15
PAGE_SIZE = 128
N_EXPERTS = 8
EPS = 1e-6
NEG_INF = -1e30
LANES = 128
VMEM_LIMIT = 56 * 1024 * 1024
COL_CHUNK = 512
Q_SCALE = HEAD_DIM_QK ** -0.5 * math.log2(math.e)


def _params(n_axes, vmem=VMEM_LIMIT):
    return pltpu.CompilerParams(dimension_semantics=("arbitrary",) * n_axes,
                                vmem_limit_bytes=vmem)


def _lambda_init(layer):
    return 0.8 - 0.6 * math.exp(-0.3 * layer)


def _mod_spec(arr, tiles_per_group):
    _, r, d = arr.shape
    return pl.BlockSpec((1, r, d), lambda m, *_: (m // tiles_per_group, 0, 0))


def _norm_mod(x, g, sc, sh):
    y = x * lax.rsqrt(jnp.mean(x * x, axis=-1, keepdims=True) + EPS)
    return (y * g) * (1.0 + sc) + sh


def _ada_kernel(c_ref, w_ref, b_ref, o_ref):
    c = c_ref[...]
    s = (c * jax.nn.sigmoid(c)).astype(BF16)
    o_ref[0] = jnp.dot(s, w_ref[0].astype(BF16), preferred_element_type=F32) + b_ref[0]


def _ada(c_all, w_ada, b_ada):
    depth, d, n = w_ada.shape
    rows = c_all.shape[0]
    tn = 1024
    return pl.pallas_call(
        _ada_kernel,
        grid=(depth, n // tn),
        in_specs=[pl.BlockSpec((rows, d), lambda l, j: (0, 0)),
                  pl.BlockSpec((1, d, tn), lambda l, j: (l, 0, j)),
                  pl.BlockSpec((1, 1, tn), lambda l, j: (l, 0, j))],
        out_specs=pl.BlockSpec((1, rows, tn), lambda l, j: (l, 0, j)),
        out_shape=jax.ShapeDtypeStruct((depth, rows, n), F32),
        compiler_params=_params(2),
        name="ada_mod",
    )(c_all, w_ada, b_ada.reshape(depth, 1, n))


def _norm_kernel(x_ref, g_ref, sc_ref, sh_ref, h_ref):
    h_ref[...] = _norm_mod(x_ref[...], g_ref[0], sc_ref[0], sh_ref[0]).astype(BF16)


def _norm(x, g_all, layer, sc, sh, tm, tiles_per_group):
    m, d = x.shape
    return pl.pallas_call(
        _norm_kernel,
        grid=(m // tm,),
        in_specs=[pl.BlockSpec((tm, d), lambda i: (i, 0)),
                  pl.BlockSpec((1, 1, d), lambda i: (layer, 0, 0)),
                  _mod_spec(sc, tiles_per_group), _mod_spec(sh, tiles_per_group)],
        out_specs=pl.BlockSpec((tm, d), lambda i: (i, 0)),
        out_shape=jax.ShapeDtypeStruct((m, d), BF16),
        compiler_params=_params(1),
        name="norm_mod",
    )(x, g_all, sc, sh)


def _qk_norm(p, g128, scale):
    lo = lax.broadcasted_iota(jnp.int32, (1, LANES), 1) < HEAD_DIM_QK
    sq = p * p
    s0 = jnp.sum(jnp.where(lo, sq, 0.0), axis=-1, keepdims=True)
    s1 = jnp.sum(jnp.where(lo, 0.0, sq), axis=-1, keepdims=True)
    inv0 = lax.rsqrt(s0 * (1.0 / HEAD_DIM_QK) + EPS)
    inv1 = lax.rsqrt(s1 * (1.0 / HEAD_DIM_QK) + EPS)
    y = (p * jnp.where(lo, inv0, inv1)) * g128
    return y if scale is None else y * scale


def _inproj_kernel(*refs, mode):
    normed = mode in ("q", "k")
    stacked = mode in ("k", "v")
    h_ref, w_ref = refs[:2]
    g_ref = refs[2] if normed else None
    wbf_ref = refs[-1]
    outs = refs[-3:-1] if stacked else refs[-2:-1]

    @pl.when(pl.program_id(0) == 0)
    def _():
        wbf_ref[...] = w_ref[0].astype(BF16)

    h = h_ref[...]
    scale = Q_SCALE if mode == "q" else None
    for c0 in range(0, wbf_ref.shape[1], COL_CHUNK):
        acc = jnp.dot(h, wbf_ref[:, c0:c0 + COL_CHUNK], preferred_element_type=F32)
        if mode == "u":
            outs[0][:, c0:c0 + COL_CHUNK] = acc
            continue
        for hh in range(COL_CHUNK // LANES):
            head = c0 // LANES + hh
            y = acc[:, hh * LANES:(hh + 1) * LANES]
            if normed:
                y = _qk_norm(y, g_ref[...], scale)
            if stacked:
                outs[0][0, pl.ds(head, h.shape[0], stride=N_HEADS), :] = y
                outs[1][:, head * LANES:(head + 1) * LANES] = y.astype(BF16)
            else:
                outs[0][:, head * LANES:(head + 1) * LANES] = y.astype(BF16)


def _inproj(h, w_in, layer, section, mode, tm, g128=None, stack=None):
    m, d = h.shape
    width = N_HEADS * LANES
    ins = [h, w_in]
    in_specs = [pl.BlockSpec((tm, d), lambda i: (i, 0)),
                pl.BlockSpec((1, d, width), lambda i: (layer, 0, section))]
    if g128 is not None:
        ins.append(g128)
        in_specs.append(pl.BlockSpec((1, LANES), lambda i: (0, 0)))
    flat_spec = pl.BlockSpec((tm, width), lambda i: (i, 0))
    aliases = {}
    if stack is not None:
        aliases = {len(ins): 0}
        ins.append(stack)
        in_specs.append(pl.BlockSpec(memory_space=pl.ANY))
        out_specs = [pl.BlockSpec((1, tm * N_HEADS, LANES), lambda i: (layer, i, 0)), flat_spec]
        out_shape = [jax.ShapeDtypeStruct(stack.shape, stack.dtype),
                     jax.ShapeDtypeStruct((m, width), BF16)]
    else:
        out_specs = [flat_spec]
        out_shape = [jax.ShapeDtypeStruct((m, width), BF16 if mode == "q" else F32)]
    n_in = len(ins)

    def body(*refs):
        if stack is not None:
            refs = refs[:n_in - 1] + refs[n_in:]
        _inproj_kernel(*refs, mode=mode)

    out = pl.pallas_call(
        body,
        grid=(m // tm,),
        in_specs=in_specs,
        out_specs=out_specs,
        out_shape=out_shape,
        scratch_shapes=[pltpu.VMEM((d, width), BF16)],
        input_output_aliases=aliases,
        compiler_params=_params(1),
        name="inproj_" + mode,
    )(*ins)
    return out if stack is not None else out[0]


def _sub_norm(o, g_sub, li):
    y = o * lax.rsqrt(jnp.mean(o * o, axis=-1, keepdims=True) + EPS)
    return (y * g_sub) * (1.0 - li)


def _attn_kernel(lam_ref, q_ref, k_ref, v_ref, g_ref, o_ref, vt_ref, s_ref, m_ref, l_ref, acc_ref,
                 *, tq, li):
    n_t = vt_ref.shape[0]
    for jt in range(n_t):
        vt_ref[jt] = v_ref[0, jt * tq:(jt + 1) * tq, :].astype(F32).T.astype(BF16)
    lo = lax.broadcasted_iota(jnp.int32, (1, LANES), 1) < HEAD_DIM_QK
    pairs = [(i, j) for i in range(n_t) for j in range(i + 1)]

    def logits(n):
        i, j = pairs[n]
        q = q_ref[0, i * tq:(i + 1) * tq, :]
        zero = jnp.zeros_like(q)
        kt = k_ref[0, j * tq:(j + 1) * tq, :]
        for mp, qm in enumerate((jnp.where(lo, q, zero), jnp.where(lo, zero, q))):
            s_ref[n % 2, mp] = lax.dot_general(kt, qm, (((1,), (1,)), ((), ())),
                                               preferred_element_type=F32)

    def consume(n):
        i, j = pairs[n]
        vt = vt_ref[j]
        for mp in range(2):
            s = s_ref[n % 2, mp]
            if j == i:
                key = lax.broadcasted_iota(jnp.int32, (tq, tq), 0)
                qry = lax.broadcasted_iota(jnp.int32, (tq, tq), 1)
                s = jnp.where(key <= qry, s, NEG_INF)
            m_old = m_ref[mp]
            m_new = jnp.maximum(m_old, jnp.max(s, axis=0, keepdims=True))
            alpha = jnp.exp2(m_old - m_new)
            p = jnp.exp2(s - m_new)
            l_ref[mp] = alpha * l_ref[mp] + jnp.sum(p, axis=0, keepdims=True)
            acc_ref[mp] = alpha * acc_ref[mp] + jnp.dot(vt, p.astype(BF16),
                                                        preferred_element_type=F32)
            m_ref[mp] = m_new

    logits(0)
    for n, (i, j) in enumerate(pairs):
        if j == 0:
            m_ref[...] = jnp.full(m_ref.shape, NEG_INF, F32)
            l_ref[...] = jnp.zeros(l_ref.shape, F32)
            acc_ref[...] = jnp.zeros(acc_ref.shape, F32)
        if n + 1 < len(pairs):
            logits(n + 1)
        consume(n)
        if j == i:
            o = acc_ref[0] / l_ref[0] - lam_ref[0] * (acc_ref[1] / l_ref[1])
            y = o * lax.rsqrt(jnp.mean(o * o, axis=0, keepdims=True) + EPS)
            o_ref[0, i * tq:(i + 1) * tq, :] = ((y.T * g_ref[...]) * (1.0 - li)).astype(BF16)


def _attn_prompt(q, k, v, layer, lam, g_sub, batch, seq, tq):
    width = N_HEADS * LANES
    q3, k3, v3 = (t.reshape(batch, seq, width) for t in (q, k, v))
    head_spec = pl.BlockSpec((1, seq, LANES), lambda b, h: (b, 0, h))
    out = pl.pallas_call(
        functools.partial(_attn_kernel, tq=tq, li=_lambda_init(layer)),
        grid=(batch, N_HEADS),
        in_specs=[pl.BlockSpec(memory_space=pltpu.SMEM),
                  head_spec, head_spec, head_spec,
                  pl.BlockSpec((1, LANES), lambda b, h: (0, 0))],
        out_specs=head_spec,
        out_shape=jax.ShapeDtypeStruct((batch, seq, width), BF16),
        scratch_shapes=[pltpu.VMEM((seq // tq, LANES, tq), BF16),
                        pltpu.VMEM((2, 2, tq, tq), F32),
                        pltpu.VMEM((2, 1, tq), F32), pltpu.VMEM((2, 1, tq), F32),
                        pltpu.VMEM((2, LANES, tq), F32)],
        compiler_params=_params(2),
        name="attn_prompt",
    )(lam, q3, k3, v3, g_sub)
    return out.reshape(batch * seq, width)


def _decode_kernel(pt_ref, lam_ref, q_ref, *refs, pages, n_new, li):
    k_refs = refs[:pages]
    v_refs = refs[pages:2 * pages]
    kn_ref, vn_ref, g_ref, o_ref, knp_ref, vnp_ref, m_ref, l_ref, acc_ref = refs[2 * pages:]
    b = pl.program_id(0)
    c = pl.program_id(1)
    last = pl.num_programs(1) - 1
    shift_q = (2 * n_new).bit_length() - 1

    @pl.when(jnp.logical_and(b == 0, c == 0))
    def _():
        knp_ref[...] = jnp.zeros(knp_ref.shape, F32)
        vnp_ref[...] = jnp.zeros(vnp_ref.shape, F32)

    @pl.when(c == 0)
    def _():
        m_ref[...] = jnp.full(m_ref.shape, NEG_INF, F32)
        l_ref[...] = jnp.zeros(l_ref.shape, F32)
        acc_ref[...] = jnp.zeros(acc_ref.shape, F32)

    q = q_ref[0]
    n_q = q.shape[0]
    page_rows = PAGE_SIZE * N_HEADS
    def own_head(n_cols):
        row = lax.broadcasted_iota(jnp.int32, (n_q, n_cols), 0)
        col = lax.broadcasted_iota(jnp.int32, (n_q, n_cols), 1)
        return row, col, jnp.bitwise_and(col, N_HEADS - 1) == lax.shift_right_logical(row, shift_q)

    def logits(k2d, keep):
        s = lax.dot_general(q, k2d.astype(BF16), (((1,), (1,)), ((), ())),
                            preferred_element_type=F32)
        return jnp.where(keep, s, NEG_INF)

    def update(s_list, v_list):
        m_old = m_ref[...]
        m_new = m_old
        for s in s_list:
            m_new = jnp.maximum(m_new, jnp.max(s, axis=-1, keepdims=True))
        alpha = jnp.exp2(m_old - m_new)
        l_new = alpha * l_ref[...]
        pv = None
        for s, v in zip(s_list, v_list):
            p = jnp.exp2(s - m_new)
            l_new = l_new + jnp.sum(p, axis=-1, keepdims=True)
            d = jnp.dot(p.astype(BF16), v, preferred_element_type=F32)
            pv = d if pv is None else pv + d
        acc_ref[...] = alpha * acc_ref[...] + pv
        l_ref[...] = l_new
        m_ref[...] = m_new

    s_list, v_list = [], []
    page_keep = own_head(page_rows)[2]
    for j in range(pages):
        s_list.append(logits(k_refs[j][0, 0].reshape(page_rows, LANES), page_keep))
        v_list.append(v_refs[j][0, 0].reshape(page_rows, LANES).astype(BF16))
    update(s_list, v_list)

    @pl.when(c == last)
    def _():
        n_rows = n_new * N_HEADS
        knp_ref[0:n_rows, :] = kn_ref[0, 0]
        vnp_ref[0:n_rows, :] = vn_ref[0, 0]
        row, col, keep = own_head(knp_ref.shape[0])
        tok = jnp.bitwise_and(row, n_new - 1)
        slot = lax.shift_right_logical(col, N_HEADS.bit_length() - 1)
        keep = jnp.logical_and(keep, slot <= tok)
        update([logits(knp_ref[...], keep)], [vnp_ref[...].astype(BF16)])

        o = acc_ref[...] / l_ref[...]
        o = o - lam_ref[0] * pltpu.roll(o, shift=n_q - n_new, axis=0)
        o_ref[0] = _sub_norm(o, g_ref[...], li)


def _attn_decode(q, k_stack, v_stack, cache_k, cache_v, page_table, layer, lam, g_sub,
                 n_new, pages):
    depth = cache_k.shape[0]
    batch, n_pages = page_table.shape
    assert n_new & (n_new - 1) == 0 and N_HEADS & (N_HEADS - 1) == 0
    n_q = N_HEADS * 2 * n_new
    qh = q.reshape(batch, n_new, N_HEADS, 1, LANES).transpose(0, 2, 3, 1, 4)
    in_map = (jnp.arange(LANES) // HEAD_DIM_QK)[None, :] == jnp.arange(2)[:, None]
    qm = jnp.where(in_map[None, None, :, None, :], qh, jnp.zeros_like(qh)).reshape(batch, n_q, LANES)
    kn = k_stack.reshape(depth, batch, n_new * N_HEADS, LANES)
    vn = v_stack.reshape(depth, batch, n_new * N_HEADS, LANES)

    def page_spec(j):
        return pl.BlockSpec((1, 1, PAGE_SIZE, N_HEADS, LANES),
                            lambda b, c, pt: (layer, pt[b, c * pages + j], 0, 0, 0))

    new_spec = pl.BlockSpec((1, 1, n_new * N_HEADS, LANES), lambda b, c, pt: (layer, b, 0, 0))
    grid_spec = pltpu.PrefetchScalarGridSpec(
        num_scalar_prefetch=1,
        grid=(batch, n_pages // pages),
        in_specs=([pl.BlockSpec(memory_space=pltpu.SMEM),
                   pl.BlockSpec((1, n_q, LANES), lambda b, c, pt: (b, 0, 0))]
                  + [page_spec(j) for j in range(pages)]
                  + [page_spec(j) for j in range(pages)]
                  + [new_spec, new_spec,
                     pl.BlockSpec((1, LANES), lambda b, c, pt: (0, 0))]),
        out_specs=pl.BlockSpec((1, n_q, LANES), lambda b, c, pt: (b, 0, 0)),
        scratch_shapes=[pltpu.VMEM((LANES, LANES), F32),
                        pltpu.VMEM((LANES, LANES), F32),
                        pltpu.VMEM((n_q, 1), F32),
                        pltpu.VMEM((n_q, 1), F32),
                        pltpu.VMEM((n_q, LANES), F32)],
    )
    out = pl.pallas_call(
        functools.partial(_decode_kernel, pages=pages, n_new=n_new, li=_lambda_init(layer)),
        grid_spec=grid_spec,
        out_shape=jax.ShapeDtypeStruct((batch, n_q, LANES), F32),
        compiler_params=_params(2),
        name="attn_decode",
    )(page_table, lam, qm, *([cache_k] * pages), *([cache_v] * pages), kn, vn, g_sub)
    out = out.reshape(batch, N_HEADS, 2, n_new, LANES)[:, :, 0]
    return out.transpose(0, 2, 1, 3).reshape(batch * n_new, N_HEADS * LANES).astype(BF16)


def _pool_kernel(u_ref, past_ref, w_ref, sc_ref, o_ref, full_ref, *, start, chunk):
    halo = past_ref.shape[1]
    t = u_ref.shape[1]
    group = u_ref.shape[2] // len(POOL_WINDOWS)
    full_ref[0:halo, :] = past_ref[0]
    full_ref[halo:halo + t, :] = u_ref[0]
    for c0 in range(0, t, chunk):
        pos = start + c0 + lax.broadcasted_iota(jnp.int32, (chunk, 1), 0)
        for g, w in enumerate(POOL_WINDOWS):
            cols = slice(g * group, (g + 1) * group)
            cur = full_ref[halo + c0:halo + c0 + chunk, cols]
            win = cur
            for i in range(1, w):
                win = win + full_ref[halo + c0 - i:halo + c0 - i + chunk, cols]
            count = jnp.minimum(w, pos + 1).astype(F32)
            pooled = (win / count - cur).astype(BF16)
            y = jnp.dot(pooled, w_ref[0, g].astype(BF16), preferred_element_type=F32)
            o_ref[0, c0:c0 + chunk, cols] = (y * sc_ref[0, :, cols]).astype(BF16)


def _pool(u3, past16, w_pool, pool_scale, layer, start):
    batch, t, c = u3.shape
    depth = w_pool.shape[0]
    group = c // len(POOL_WINDOWS)
    chunk = min(t, 256)
    return pl.pallas_call(
        functools.partial(_pool_kernel, start=start, chunk=chunk),
        grid=(batch,),
        in_specs=[pl.BlockSpec((1, t, c), lambda b: (b, 0, 0)),
                  pl.BlockSpec((1, POOL_STATE + 1, c), lambda b: (b, 0, 0)),
                  pl.BlockSpec((1, len(POOL_WINDOWS), group, group), lambda b: (layer, 0, 0, 0)),
                  pl.BlockSpec((1, 1, c), lambda b: (layer, 0, 0))],
        out_specs=pl.BlockSpec((1, t, c), lambda b: (b, 0, 0)),
        out_shape=jax.ShapeDtypeStruct((batch, t, c), BF16),
        scratch_shapes=[pltpu.VMEM((POOL_STATE + 1 + t, c), F32)],
        compiler_params=_params(1),
        name="pool_mix",
    )(u3, past16, w_pool, pool_scale.reshape(depth, 1, c))


def _outproj_kernel(a_ref, p_ref, x_ref, w_ref, gt_ref, g_ref, sc_ref, sh_ref,
                    x1_ref, h2_ref, h2_rows_ref, wbf_ref):
    @pl.when(pl.program_id(0) == 0)
    def _():
        wbf_ref[...] = w_ref[0].astype(BF16)

    half = a_ref.shape[1]
    a = a_ref[...]
    p = p_ref[...]
    for c0 in range(0, x_ref.shape[1], COL_CHUNK):
        cols = slice(c0, c0 + COL_CHUNK)
        y = jnp.dot(a, wbf_ref[0:half, cols], preferred_element_type=F32)
        y = y + jnp.dot(p, wbf_ref[half:, cols], preferred_element_type=F32)
        x1_ref[:, cols] = x_ref[:, cols] + gt_ref[0, :, cols] * y
    h2 = _norm_mod(x1_ref[...], g_ref[0], sc_ref[0], sh_ref[0])
    h2_ref[...] = h2.astype(BF16)
    if h2_rows_ref is not None:
        tm, chunks = h2.shape[0], h2.shape[1] // LANES
        for c in range(chunks):
            h2_rows_ref[pl.ds(c, tm, stride=chunks), :] = h2[:, c * LANES:(c + 1) * LANES]


def _outproj(a, p, x, w_o, layer, gt, g_ffn, sc, sh, tm, tiles_per_group, row_major):
    m, d = x.shape
    half = a.shape[1]
    out_specs = [pl.BlockSpec((tm, d), lambda i: (i, 0)), pl.BlockSpec((tm, d), lambda i: (i, 0))]
    out_shape = [jax.ShapeDtypeStruct((m, d), F32), jax.ShapeDtypeStruct((m, d), BF16)]
    chunks = d // LANES
    if row_major:
        out_specs.append(pl.BlockSpec((tm * chunks, LANES), lambda i: (i, 0)))
        out_shape.append(jax.ShapeDtypeStruct((m * chunks, LANES), F32))

    def body(*refs):
        if row_major:
            _outproj_kernel(*refs)
        else:
            _outproj_kernel(*refs[:-1], None, refs[-1])

    outs = pl.pallas_call(
        body,
        grid=(m // tm,),
        in_specs=[pl.BlockSpec((tm, half), lambda i: (i, 0)),
                  pl.BlockSpec((tm, half), lambda i: (i, 0)),
                  pl.BlockSpec((tm, d), lambda i: (i, 0)),
                  pl.BlockSpec((1, d, d), lambda i: (layer, 0, 0),
                               pipeline_mode=pl.Buffered(1)),
                  _mod_spec(gt, tiles_per_group),
                  pl.BlockSpec((1, 1, d), lambda i: (layer, 0, 0)),
                  _mod_spec(sc, tiles_per_group), _mod_spec(sh, tiles_per_group)],
        out_specs=out_specs,
        out_shape=out_shape,
        scratch_shapes=[pltpu.VMEM((d, d), BF16)],
        compiler_params=_params(1),
        name="outproj",
    )(a, p, x, w_o, gt, g_ffn, sc, sh)
    if row_major:
        return outs[0], outs[1], outs[2].reshape(m, chunks, LANES)
    return outs


def _swiglu_act(h_ref, wg_ref, wu_ref):
    h = h_ref[...]
    g = jnp.dot(h, wg_ref[0].astype(BF16), preferred_element_type=F32)
    u = jnp.dot(h, wu_ref[0].astype(BF16), preferred_element_type=F32)
    return ((g * jax.nn.sigmoid(g)) * u).astype(BF16)


def _ffn_kernel(h_ref, x_ref, gt_ref, wg_ref, wu_ref, wd_ref, o_ref):
    @pl.when(pl.program_id(1) == 0)
    def _():
        o_ref[...] = x_ref[...]

    act = _swiglu_act(h_ref, wg_ref, wu_ref)
    for c0 in range(0, o_ref.shape[1], COL_CHUNK):
        cols = slice(c0, c0 + COL_CHUNK)
        y = jnp.dot(act, wd_ref[0, :, cols].astype(BF16), preferred_element_type=F32)
        o_ref[:, cols] += gt_ref[0, :, cols] * y


def _ffn(h, x, gt, wg, wu, wd, widx, tm, tf, tiles_per_group):
    m, d = x.shape
    f = wg.shape[2]
    return pl.pallas_call(
        _ffn_kernel,
        grid=(m // tm, f // tf),
        in_specs=[pl.BlockSpec((tm, d), lambda i, j: (i, 0)),
                  pl.BlockSpec((tm, d), lambda i, j: (i, 0), pipeline_mode=pl.Buffered(1)),
                  _mod_spec(gt, tiles_per_group),
                  pl.BlockSpec((1, d, tf), lambda i, j: (widx, 0, j)),
                  pl.BlockSpec((1, d, tf), lambda i, j: (widx, 0, j)),
                  pl.BlockSpec((1, tf, d), lambda i, j: (widx, j, 0))],
        out_specs=pl.BlockSpec((tm, d), lambda i, j: (i, 0)),
        out_shape=jax.ShapeDtypeStruct((m, d), F32),
        compiler_params=_params(2),
        name="ffn",
    )(h, x, gt, wg, wu, wd)


def _all_experts_kernel(h_ref, x_ref, gt_ref, r_ref, wg_ref, wu_ref, wd_ref, o_ref):
    e = pl.program_id(1)

    @pl.when(jnp.logical_and(e == 0, pl.program_id(2) == 0))
    def _():
        o_ref[...] = x_ref[...]

    h = h_ref[...]
    g = jnp.dot(h, wg_ref[0].astype(BF16), preferred_element_type=F32)
    u = jnp.dot(h, wu_ref[0].astype(BF16), preferred_element_type=F32)
    act = ((g * jax.nn.sigmoid(g)) * u).astype(BF16)
    route = r_ref[...]
    lane = lax.broadcasted_iota(jnp.int32, route.shape, 1)
    cw = jnp.sum(jnp.where(lane == e, route, 0.0), axis=-1, keepdims=True)
    for c0 in range(0, o_ref.shape[1], COL_CHUNK):
        cols = slice(c0, c0 + COL_CHUNK)
        y = jnp.dot(act, wd_ref[0, :, cols].astype(BF16), preferred_element_type=F32)
        o_ref[:, cols] += (gt_ref[0, :, cols] * cw) * y


def _all_experts(h, x, gt, route, wg, wu, wd, layer_base, tm, tf, tiles_per_group):
    m, d = x.shape
    f = wg.shape[2]
    return pl.pallas_call(
        _all_experts_kernel,
        grid=(m // tm, N_EXPERTS, f // tf),
        in_specs=[pl.BlockSpec((tm, d), lambda i, e, j: (i, 0)),
                  pl.BlockSpec((tm, d), lambda i, e, j: (i, 0)),
                  _mod_spec(gt, tiles_per_group),
                  pl.BlockSpec((tm, LANES), lambda i, e, j: (i, 0)),
                  pl.BlockSpec((1, d, tf), lambda i, e, j: (layer_base + e, 0, j)),
                  pl.BlockSpec((1, d, tf), lambda i, e, j: (layer_base + e, 0, j)),
                  pl.BlockSpec((1, tf, d), lambda i, e, j: (layer_base + e, j, 0))],
        out_specs=pl.BlockSpec((tm, d), lambda i, e, j: (i, 0)),
        out_shape=jax.ShapeDtypeStruct((m, d), F32),
        compiler_params=_params(3),
        name="moe_all_experts",
    )(h, x, gt, route, wg, wu, wd)


ROUTE_E1, ROUTE_E2, ROUTE_POS1, ROUTE_POS2, ROUTE_G1, ROUTE_G2 = range(N_EXPERTS, N_EXPERTS + 6)


def _router_kernel(h_ref, whi_ref, wlo_ref, b_ref, o_ref, cnt_ref, carry_ref):
    @pl.when(pl.program_id(0) == 0)
    def _():
        carry_ref[...] = jnp.zeros(carry_ref.shape, F32)

    h = h_ref[...].astype(BF16)
    logits = (jnp.dot(h, whi_ref[...], preferred_element_type=F32)
              + jnp.dot(h, wlo_ref[...], preferred_element_type=F32) + b_ref[...])
    lane = lax.broadcasted_iota(jnp.int32, logits.shape, 1).astype(F32)
    logits = jnp.where(lane < N_EXPERTS, logits, NEG_INF)
    v1 = jnp.max(logits, axis=-1, keepdims=True)
    i1 = jnp.min(jnp.where(logits == v1, lane, float(LANES)), axis=-1, keepdims=True)
    rest = jnp.where(lane == i1, NEG_INF, logits)
    v2 = jnp.max(rest, axis=-1, keepdims=True)
    i2 = jnp.min(jnp.where(rest == v2, lane, float(LANES)), axis=-1, keepdims=True)
    e2 = jnp.exp(v2 - v1)
    g1 = 1.0 / (1.0 + e2)
    g2 = e2 / (1.0 + e2)

    tm = h.shape[0]
    sel = jnp.where(jnp.logical_or(lane == i1, lane == i2), 1.0, 0.0)
    earlier = (lax.broadcasted_iota(jnp.int32, (tm, tm), 1)
               < lax.broadcasted_iota(jnp.int32, (tm, tm), 0))
    rank = jnp.dot(jnp.where(earlier, 1.0, 0.0).astype(BF16), sel.astype(BF16),
                   preferred_element_type=F32) + carry_ref[...]
    pos1 = jnp.sum(jnp.where(lane == i1, rank, 0.0), axis=-1, keepdims=True)
    pos2 = jnp.sum(jnp.where(lane == i2, rank, 0.0), axis=-1, keepdims=True)
    total = carry_ref[...] + jnp.sum(sel, axis=0, keepdims=True)
    carry_ref[...] = total
    cnt_ref[...] = total

    out = jnp.where(lane == i1, g1, 0.0) + jnp.where(lane == i2, g2, 0.0)
    for idx, val in ((ROUTE_E1, i1), (ROUTE_E2, i2), (ROUTE_POS1, pos1), (ROUTE_POS2, pos2),
                     (ROUTE_G1, g1), (ROUTE_G2, g2)):
        out = out + jnp.where(lane == float(idx), val, 0.0)
    o_ref[...] = out


def _router(h, router_w, router_b, tm):
    m, d = h.shape
    pad = LANES - N_EXPERTS
    w = jnp.pad(router_w, ((0, 0), (0, pad)))
    whi = w.astype(BF16)
    wlo = (w - whi.astype(F32)).astype(BF16)
    b = jnp.pad(router_b, (0, pad)).reshape(1, LANES)
    return pl.pallas_call(
        _router_kernel,
        grid=(m // tm,),
        in_specs=[pl.BlockSpec((tm, d), lambda i: (i, 0)),
                  pl.BlockSpec((d, LANES), lambda i: (0, 0)),
                  pl.BlockSpec((d, LANES), lambda i: (0, 0)),
                  pl.BlockSpec((1, LANES), lambda i: (0, 0))],
        out_specs=[pl.BlockSpec((tm, LANES), lambda i: (i, 0)),
                   pl.BlockSpec((1, LANES), lambda i: (0, 0))],
        out_shape=[jax.ShapeDtypeStruct((m, LANES), F32),
                   jax.ShapeDtypeStruct((1, LANES), F32)],
        scratch_shapes=[pltpu.VMEM((1, LANES), F32)],
        compiler_params=_params(1),
        name="router",
    )(h, whi, wlo, b)


def _route_plan(route, counts, tm):
    m = route.shape[0]
    n_tiles = 2 * m // tm + N_EXPERTS
    col = lambda i: route[:, i].astype(jnp.int32)
    cnt = counts[0, :N_EXPERTS].astype(jnp.int32)
    tiles = (cnt + tm - 1) // tm
    tile_end = jnp.cumsum(tiles)
    row_start = (tile_end - tiles) * tm
    d1 = row_start[col(ROUTE_E1)] + col(ROUTE_POS1)
    d2 = row_start[col(ROUTE_E2)] + col(ROUTE_POS2)
    token = jnp.arange(m, dtype=jnp.int32)
    src = jnp.zeros((n_tiles * tm,), jnp.int32).at[jnp.concatenate([d1, d2])].set(
        jnp.concatenate([token, token]), unique_indices=True)
    tile_id = jnp.arange(n_tiles, dtype=jnp.int32)
    tile_expert = jnp.minimum(jnp.searchsorted(tile_end, tile_id, side="right"),
                              N_EXPERTS - 1).astype(jnp.int32)
    tile_rows = jnp.clip(cnt[tile_expert] - (tile_id - (tile_end - tiles)[tile_expert]) * tm, 0, tm)
    tile_rows = jnp.where(tile_id < tile_end[-1], tile_rows, 0).astype(jnp.int32)
    return d1, d2, src, tile_expert, tile_rows, tile_end[-1:].astype(jnp.int32)


def _row_copy(src_hbm, row, dst_vmem, slot, sem):
    return pltpu.make_async_copy(src_hbm.at[pl.ds(row, 1), :], dst_vmem.at[pl.ds(slot, 1), :], sem)


def _slab_copy(src_hbm, row, dst_vmem, slot, sem):
    chunks = src_hbm.shape[1]
    return pltpu.make_async_copy(
        src_hbm.at[row], dst_vmem.at[pl.ds(pl.multiple_of(slot * chunks, chunks), chunks), :], sem)


def _gather_kernel(rows_ref, src_ref, h_hbm, o_ref, buf_ref, sems):
    i = pl.program_id(0)
    n_tiles = pl.num_programs(0) - 1
    tm = o_ref.shape[0]
    chunks = h_hbm.shape[1]

    @pl.when(i == 0)
    def _():
        buf_ref[...] = jnp.zeros(buf_ref.shape, F32)

    def copy(slot, r, row):
        return _slab_copy(h_hbm, row, buf_ref.at[slot], r, sems.at[slot])

    def pairs(tile):
        return lax.shift_right_logical(rows_ref[tile] + 1, 1)

    @pl.when(i < n_tiles)
    def _():
        slot = jnp.bitwise_and(i, 1)

        def issue(k, c):
            for prio in range(2):
                r = 2 * k + prio
                copy(slot, r, src_ref[0, 0, r]).start(priority=prio)
            return c

        lax.fori_loop(0, pairs(i), issue, 0)

    @pl.when(i > 0)
    def _():
        slot = jnp.bitwise_and(i - 1, 1)

        def wait(k, c):
            for prio in range(2):
                copy(slot, 2 * k + prio, 0).wait()
            return c

        lax.fori_loop(0, pairs(i - 1), wait, 0)
        for c in range(chunks):
            o_ref[:, c * LANES:(c + 1) * LANES] = (
                buf_ref[slot, pl.ds(c, tm, stride=chunks), :].astype(BF16))


def _gather_rows(h_rows, src, tile_rows, tm):
    _, chunks, _ = h_rows.shape
    n_rows = src.shape[0]
    n_tiles = n_rows // tm
    grid_spec = pltpu.PrefetchScalarGridSpec(
        num_scalar_prefetch=1,
        grid=(n_tiles + 1,),
        in_specs=[pl.BlockSpec((1, 1, tm), lambda i, nr: (jnp.minimum(i, n_tiles - 1), 0, 0),
                               memory_space=pltpu.SMEM),
                  pl.BlockSpec(memory_space=pl.ANY)],
        out_specs=pl.BlockSpec((tm, chunks * LANES), lambda i, nr: (jnp.maximum(i - 1, 0), 0)),
        scratch_shapes=[pltpu.VMEM((2, tm * chunks, LANES), F32), pltpu.SemaphoreType.DMA((2,))],
    )
    return pl.pallas_call(
        _gather_kernel,
        grid_spec=grid_spec,
        out_shape=jax.ShapeDtypeStruct((n_rows, chunks * LANES), BF16),
        compiler_params=_params(1),
        name="moe_gather",
    )(tile_rows, src.reshape(n_rows // tm, 1, tm), h_rows)


def _expert_ffn_kernel(te_ref, nv_ref, h_ref, wg_ref, wu_ref, wd_ref, wgt_ref, wut_ref, wdt_ref,
                       o_ref, *, n_main):
    i = pl.program_id(0)
    j = pl.program_id(1)

    @pl.when(j == 0)
    def _():
        o_ref[...] = jnp.zeros(o_ref.shape, F32)

    def accumulate(wg, wu, wd):
        act = _swiglu_act(h_ref, wg, wu)
        for c0 in range(0, o_ref.shape[1], COL_CHUNK):
            cols = slice(c0, c0 + COL_CHUNK)
            o_ref[:, cols] += jnp.dot(act, wd[0, :, cols].astype(BF16),
                                      preferred_element_type=F32)

    in_use = i < nv_ref[0]

    @pl.when(jnp.logical_and(in_use, j < n_main))
    def _():
        accumulate(wg_ref, wu_ref, wd_ref)

    @pl.when(jnp.logical_and(in_use, j == n_main))
    def _():
        accumulate(wgt_ref, wut_ref, wdt_ref)


def _expert_ffn(xs, wg, wu, wd, layer_base, tile_expert, n_valid, tm):
    n_rows, d = xs.shape
    f = wg.shape[2]
    wide, tail = 2 * LANES, LANES
    n_main = f // wide
    assert f == n_main * wide + tail
    tail_blk = f // tail - 1

    def main_blk(i, j, nv):
        return jnp.where(i < nv[0], jnp.minimum(j, n_main - 1), 0)

    grid_spec = pltpu.PrefetchScalarGridSpec(
        num_scalar_prefetch=2,
        grid=(n_rows // tm, n_main + 1),
        in_specs=[
            pl.BlockSpec((tm, d), lambda i, j, te, nv: (i, 0)),
            pl.BlockSpec((1, d, wide), lambda i, j, te, nv: (layer_base + te[i], 0, main_blk(i, j, nv))),
            pl.BlockSpec((1, d, wide), lambda i, j, te, nv: (layer_base + te[i], 0, main_blk(i, j, nv))),
            pl.BlockSpec((1, wide, d), lambda i, j, te, nv: (layer_base + te[i], main_blk(i, j, nv), 0)),
            pl.BlockSpec((1, d, tail), lambda i, j, te, nv: (layer_base + te[i], 0, tail_blk)),
            pl.BlockSpec((1, d, tail), lambda i, j, te, nv: (layer_base + te[i], 0, tail_blk)),
            pl.BlockSpec((1, tail, d), lambda i, j, te, nv: (layer_base + te[i], tail_blk, 0)),
        ],
        out_specs=pl.BlockSpec((tm, d), lambda i, j, te, nv: (i, 0)),
    )
    return pl.pallas_call(
        functools.partial(_expert_ffn_kernel, n_main=n_main),
        grid_spec=grid_spec,
        out_shape=jax.ShapeDtypeStruct((n_rows, d), F32),
        compiler_params=_params(2),
        name="moe_ffn",
    )(tile_expert, n_valid, xs, wg, wu, wd, wg, wu, wd)


def _combine_kernel(d1_ref, d2_ref, x_ref, r_ref, gt_ref, y_hbm, o_ref, a_ref, b_ref, sems):
    i = pl.program_id(0)
    n_tiles = pl.num_programs(0) - 1
    tc = a_ref.shape[1]

    def copies(slot, r, row1, row2):
        return (_row_copy(y_hbm, row1, a_ref.at[slot], r, sems.at[0, slot]),
                _row_copy(y_hbm, row2, b_ref.at[slot], r, sems.at[1, slot]))

    @pl.when(i < n_tiles)
    def _():
        slot = jnp.bitwise_and(i, 1)

        def issue(r, c):
            for prio, cp in enumerate(copies(slot, r, d1_ref[0, 0, r], d2_ref[0, 0, r])):
                cp.start(priority=prio)
            return c

        lax.fori_loop(0, tc, issue, 0, unroll=4)

    @pl.when(i > 0)
    def _():
        slot = jnp.bitwise_and(i - 1, 1)

        def wait(r, c):
            for cp in copies(slot, r, 0, 0):
                cp.wait()
            return c

        lax.fori_loop(0, tc, wait, 0, unroll=4)
        route = r_ref[...]
        lane = lax.broadcasted_iota(jnp.int32, route.shape, 1)
        g1 = jnp.sum(jnp.where(lane == ROUTE_G1, route, 0.0), axis=-1, keepdims=True)
        g2 = jnp.sum(jnp.where(lane == ROUTE_G2, route, 0.0), axis=-1, keepdims=True)
        o_ref[...] = x_ref[...] + gt_ref[0] * (g1 * a_ref[slot] + g2 * b_ref[slot])


def _combine(x, route, gt, y, d1, d2, tc, tiles_per_group):
    m, d = x.shape
    n_tiles = m // tc
    ahead = lambda i: jnp.minimum(i, n_tiles - 1)
    done = lambda i: jnp.maximum(i - 1, 0)
    idx_spec = pl.BlockSpec((1, 1, tc), lambda i: (ahead(i), 0, 0), memory_space=pltpu.SMEM)
    _, r, _ = gt.shape
    return pl.pallas_call(
        _combine_kernel,
        grid=(n_tiles + 1,),
        in_specs=[idx_spec, idx_spec,
                  pl.BlockSpec((tc, d), lambda i: (done(i), 0)),
                  pl.BlockSpec((tc, LANES), lambda i: (done(i), 0)),
                  pl.BlockSpec((1, r, d), lambda i: (done(i) // tiles_per_group, 0, 0)),
                  pl.BlockSpec(memory_space=pl.ANY)],
        out_specs=pl.BlockSpec((tc, d), lambda i: (done(i), 0)),
        out_shape=jax.ShapeDtypeStruct((m, d), F32),
        scratch_shapes=[pltpu.VMEM((2, tc, d), F32), pltpu.VMEM((2, tc, d), F32),
                        pltpu.SemaphoreType.DMA((2, 2))],
        compiler_params=_params(1),
        name="moe_combine",
    )(d1.reshape(n_tiles, 1, tc), d2.reshape(n_tiles, 1, tc), x, route, gt, y)


def _moe_routed(h2, h2_rows, x, gt, p, j, tm_router, tm, tc, tiles_per_group):
    d = x.shape[1]
    n_exp, _, f_exp = p["moe_w_gate"].shape[1:]
    route, counts = _router(h2, p["router_w"][j], p["router_b"][j], tm_router)
    d1, d2, src, tile_expert, tile_rows, n_valid = _route_plan(route, counts, tm)
    xs = _gather_rows(h2_rows, src, tile_rows, tm)
    y = _expert_ffn(xs, p["moe_w_gate"].reshape(-1, d, f_exp), p["moe_w_up"].reshape(-1, d, f_exp),
                    p["moe_w_down"].reshape(-1, f_exp, d), j * n_exp, tile_expert, n_valid, tm)
    return _combine(x, route, gt, y, d1, d2, tc, tiles_per_group)


def _trunk(x, mods, start, pool_past, attend, p, batch, seq, tm, tm_out, tm_ffn, tm_moe,
           k_stack, v_stack):
    depth = p["w_in"].shape[0]
    m, d = x.shape
    tpg = (m // mods[0][0].shape[0]) // tm
    tpg_out = (m // mods[0][0].shape[0]) // tm_out
    tpg_ffn = (m // mods[0][0].shape[0]) // tm_ffn
    pools = []
    sh_m, sc_m = mods[0][0], mods[0][1]
    h = _norm(x, p["g_norm_mix"], 0, sc_m, sh_m, tm, tpg)
    for l in range(depth):
        sh_m, sc_m, gt_m, sh_f, sc_f, gt_f = mods[l]
        if l > 0:
            h = _norm(x, p["g_norm_mix"], l, sc_m, sh_m, tm, tpg)
        g_q = jnp.tile(p["g_q"][l], 2).reshape(1, LANES)
        g_k = jnp.tile(p["g_k"][l], 2).reshape(1, LANES)
        q = _inproj(h, p["w_in"], l, 0, "q", tm, g128=g_q)
        k_stack, k_bf = _inproj(h, p["w_in"], l, 1, "k", tm, g128=g_k, stack=k_stack)
        v_stack, v_bf = _inproj(h, p["w_in"], l, 2, "v", tm, stack=v_stack)
        u = _inproj(h, p["w_in"], l, 3, "u", tm)
        lam = (jnp.exp(jnp.sum(p["lambda_q1"][l] * p["lambda_k1"][l]))
               - jnp.exp(jnp.sum(p["lambda_q2"][l] * p["lambda_k2"][l]))
               + _lambda_init(l)).astype(F32).reshape(1)
        g_sub = p["g_sub"][l].reshape(1, LANES)
        a = attend(l, q, k_bf, v_bf, k_stack, v_stack, lam, g_sub)
        pool_out, pool_new = pool_past(l, u)
        pools.append(pool_new)
        j = l // 2
        routed = l % 2 == 1 and tm_moe is not None
        x, h2, *h2_rows = _outproj(a, pool_out, x, p["w_o"], l, gt_m, p["g_norm_ffn"], sc_f, sh_f,
                                   tm_out, tpg_out, routed)
        if l % 2 == 0:
            x = _ffn(h2, x, gt_f, p["dense_w_gate"], p["dense_w_up"], p["dense_w_down"], j,
                     tm_ffn, 256, tpg_ffn)
        elif routed:
            x = _moe_routed(h2, h2_rows[0], x, gt_f, p, j, tm, tm_moe, tm_out, tpg_out)
        else:
            route, _ = _router(h2, p["router_w"][j], p["router_b"][j], tm)
            n_exp, _, f_exp = p["moe_w_gate"].shape[1:]
            x = _all_experts(h2, x, gt_f, route, p["moe_w_gate"].reshape(-1, d, f_exp),
                             p["moe_w_up"].reshape(-1, d, f_exp),
                             p["moe_w_down"].reshape(-1, f_exp, d), j * n_exp, tm_ffn, LANES, tpg_ffn)
    return x, k_stack, v_stack, jnp.stack(pools)


def kernel(x_prompt, x_sample, cache_k, cache_v, state_pool, page_table, c_prompt, c_sample,
           g_norm_mix, g_norm_ffn, w_ada, b_ada, w_in, g_q, g_k,
           lambda_q1, lambda_k1, lambda_q2, lambda_k2, g_sub, w_pool, pool_scale, w_o,
           dense_w_gate, dense_w_up, dense_w_down,
           router_w, router_b, moe_w_gate, moe_w_up, moe_w_down):
    p = dict(g_norm_mix=g_norm_mix.reshape(-1, 1, g_norm_mix.shape[-1]),
             g_norm_ffn=g_norm_ffn.reshape(-1, 1, g_norm_ffn.shape[-1]),
             w_in=w_in, g_q=g_q, g_k=g_k,
             lambda_q1=lambda_q1, lambda_k1=lambda_k1, lambda_q2=lambda_q2, lambda_k2=lambda_k2,
             g_sub=g_sub, w_pool=w_pool, pool_scale=pool_scale, w_o=w_o,
             dense_w_gate=dense_w_gate, dense_w_up=dense_w_up, dense_w_down=dense_w_down,
             router_w=router_w, router_b=router_b,
             moe_w_gate=moe_w_gate, moe_w_up=moe_w_up, moe_w_down=moe_w_down)
    depth = w_in.shape[0]
    bp, tp, d = x_prompt.shape
    bs, ts, _ = x_sample.shape
    width = N_HEADS * LANES
    c_pool = pool_scale.shape[-1]

    n_c = bp + bs
    rows = -(-n_c // 16) * 16
    c_all = jnp.concatenate([c_prompt, c_sample, jnp.zeros((rows - n_c, d), F32)], axis=0)
    mod = _ada(c_all, w_ada, b_ada).reshape(depth, rows, 6, d)
    mods_p = [[mod[l, :bp, i][:, None, :] for i in range(6)] for l in range(depth)]
    mods_s = [[jnp.repeat(mod[l, bp:n_c, i], ts, axis=0)[None] for i in range(6)]
              for l in range(depth)]

    zeros_past = jnp.zeros((bp, POOL_STATE + 1, c_pool), F32)

    def attend_p(l, q, k_bf, v_bf, ks, vs, lam, gs):
        return _attn_prompt(q, k_bf, v_bf, l, lam, gs, bp, tp, 512)

    def pool_p(l, u):
        u3 = u.reshape(bp, tp, c_pool)
        out = _pool(u3, zeros_past, w_pool, pool_scale, l, 0)
        return out.reshape(bp * tp, c_pool), u3[:, tp - POOL_STATE:]

    kp0 = jnp.zeros((depth, bp * tp * N_HEADS, LANES), F32)
    vp0 = jnp.zeros((depth, bp * tp * N_HEADS, LANES), F32)
    y_p, k_p, v_p, pool_p_new = _trunk(x_prompt.reshape(bp * tp, d), mods_p, 0, pool_p, attend_p,
                                       p, bp, tp, 512, 256, 1024, 1024, kp0, vp0)

    past_len = page_table.shape[1] * PAGE_SIZE
    t_pad = 16

    def attend_s(l, q, k_bf, v_bf, ks, vs, lam, gs):
        return _attn_decode(q, ks, vs, cache_k, cache_v, page_table, l, lam, gs, ts, 16)

    def pool_s(l, u):
        u3 = u.reshape(bs, ts, c_pool)
        past16 = jnp.concatenate([jnp.zeros((bs, 1, c_pool), F32), state_pool[l]], axis=1)
        u_pad = jnp.concatenate([u3, jnp.zeros((bs, t_pad - ts, c_pool), F32)], axis=1)
        out = _pool(u_pad, past16, w_pool, pool_scale, l, past_len)
        new = jnp.concatenate([state_pool[l], u3], axis=1)[:, -POOL_STATE:]
        return out[:, :ts].reshape(bs * ts, c_pool), new

    ks0 = jnp.zeros((depth, bs * ts * N_HEADS, LANES), F32)
    vs0 = jnp.zeros((depth, bs * ts * N_HEADS, LANES), F32)
    y_s, k_s, v_s, pool_s_new = _trunk(x_sample.reshape(bs * ts, d), mods_s, past_len, pool_s,
                                       attend_s, p, bs, ts, bs * ts, bs * ts, bs * ts, None,
                                       ks0, vs0)

    return (y_p.reshape(bp, tp, d), y_s.reshape(bs, ts, d),
            k_p.reshape(depth, bp, tp, N_HEADS, LANES), v_p.reshape(depth, bp, tp, N_HEADS, LANES),
            pool_p_new,
            k_s.reshape(depth, bs, ts, N_HEADS, LANES), v_s.reshape(depth, bs, ts, N_HEADS, LANES),
            pool_s_new)
```

```python
import functools
import math

import jax
import jax.numpy as jnp
from jax import lax
from jax.experimental import pallas as pl
from jax.experimental.pallas import tpu as pltpu

F32 = jnp.float32
BF16 = jnp.bfloat16

N_HEADS = 8
HEAD_DIM_V = 128
HEAD_DIM_QK = 64
POOL_WINDOWS = (2, 4, 8, 16)
POOL_STATE = 15
PAGE_SIZE = 128
N_EXPERTS = 8
EPS = 1e-6
NEG_INF = -1e30
LANES = 128
VMEM_LIMIT = 56 * 1024 * 1024
COL_CHUNK = 512
Q_SCALE = HEAD_DIM_QK ** -0.5 * math.log2(math.e)


def _params(n_axes, vmem=VMEM_LIMIT):
    return pltpu.CompilerParams(dimension_semantics=("arbitrary",) * n_axes,
                                vmem_limit_bytes=vmem)


def _lambda_init(layer):
    return 0.8 - 0.6 * math.exp(-0.3 * layer)


def _mod_spec(arr, tiles_per_group):
    _, r, d = arr.shape
    return pl.BlockSpec((1, r, d), lambda m, *_: (m // tiles_per_group, 0, 0))


def _norm_mod(x, g, sc, sh):
    y = x * lax.rsqrt(jnp.mean(x * x, axis=-1, keepdims=True) + EPS)
    return (y * g) * (1.0 + sc) + sh


def _ada_kernel(c_ref, w_ref, b_ref, o_ref):
    c = c_ref[...]
    s = (c * jax.nn.sigmoid(c)).astype(BF16)
    o_ref[0] = jnp.dot(s, w_ref[0].astype(BF16), preferred_element_type=F32) + b_ref[0]


def _ada(c_all, w_ada, b_ada):
    depth, d, n = w_ada.shape
    rows = c_all.shape[0]
    tn = 1024
    return pl.pallas_call(
        _ada_kernel,
        grid=(depth, n // tn),
        in_specs=[pl.BlockSpec((rows, d), lambda l, j: (0, 0)),
                  pl.BlockSpec((1, d, tn), lambda l, j: (l, 0, j)),
                  pl.BlockSpec((1, 1, tn), lambda l, j: (l, 0, j))],
        out_specs=pl.BlockSpec((1, rows, tn), lambda l, j: (l, 0, j)),
        out_shape=jax.ShapeDtypeStruct((depth, rows, n), F32),
        compiler_params=_params(2),
        name="ada_mod",
    )(c_all, w_ada, b_ada.reshape(depth, 1, n))


def _norm_kernel(x_ref, g_ref, sc_ref, sh_ref, h_ref):
    h_ref[...] = _norm_mod(x_ref[...], g_ref[0], sc_ref[0], sh_ref[0]).astype(BF16)


def _norm(x, g_all, layer, sc, sh, tm, tiles_per_group):
    m, d = x.shape
    return pl.pallas_call(
        _norm_kernel,
        grid=(m // tm,),
        in_specs=[pl.BlockSpec((tm, d), lambda i: (i, 0)),
                  pl.BlockSpec((1, 1, d), lambda i: (layer, 0, 0)),
                  _mod_spec(sc, tiles_per_group), _mod_spec(sh, tiles_per_group)],
        out_specs=pl.BlockSpec((tm, d), lambda i: (i, 0)),
        out_shape=jax.ShapeDtypeStruct((m, d), BF16),
        compiler_params=_params(1),
        name="norm_mod",
    )(x, g_all, sc, sh)


def _qk_norm(p, g128, scale):
    lo = lax.broadcasted_iota(jnp.int32, (1, LANES), 1) < HEAD_DIM_QK
    sq = p * p
    s0 = jnp.sum(jnp.where(lo, sq, 0.0), axis=-1, keepdims=True)
    s1 = jnp.sum(jnp.where(lo, 0.0, sq), axis=-1, keepdims=True)
    inv0 = lax.rsqrt(s0 * (1.0 / HEAD_DIM_QK) + EPS)
    inv1 = lax.rsqrt(s1 * (1.0 / HEAD_DIM_QK) + EPS)
    y = (p * jnp.where(lo, inv0, inv1)) * g128
    return y if scale is None else y * scale


def _inproj_kernel(*refs, mode):
    normed = mode in ("q", "k")
    stacked = mode in ("k", "v")
    h_ref, w_ref = refs[:2]
    g_ref = refs[2] if normed else None
    wbf_ref = refs[-1]
    outs = refs[-3:-1] if stacked else refs[-2:-1]

    @pl.when(pl.program_id(0) == 0)
    def _():
        wbf_ref[...] = w_ref[0].astype(BF16)

    h = h_ref[...]
    scale = Q_SCALE if mode == "q" else None
    for c0 in range(0, wbf_ref.shape[1], COL_CHUNK):
        acc = jnp.dot(h, wbf_ref[:, c0:c0 + COL_CHUNK], preferred_element_type=F32)
        if mode == "u":
            outs[0][:, c0:c0 + COL_CHUNK] = acc
            continue
        for hh in range(COL_CHUNK // LANES):
            head = c0 // LANES + hh
            y = acc[:, hh * LANES:(hh + 1) * LANES]
            if normed:
                y = _qk_norm(y, g_ref[...], scale)
            if stacked:
                outs[0][0, pl.ds(head, h.shape[0], stride=N_HEADS), :] = y
                outs[1][:, head * LANES:(head + 1) * LANES] = y.astype(BF16)
            else:
                outs[0][:, head * LANES:(head + 1) * LANES] = y.astype(BF16)


def _inproj(h, w_in, layer, section, mode, tm, g128=None, stack=None):
    m, d = h.shape
    width = N_HEADS * LANES
    ins = [h, w_in]
    in_specs = [pl.BlockSpec((tm, d), lambda i: (i, 0)),
                pl.BlockSpec((1, d, width), lambda i: (layer, 0, section))]
    if g128 is not None:
        ins.append(g128)
        in_specs.append(pl.BlockSpec((1, LANES), lambda i: (0, 0)))
    flat_spec = pl.BlockSpec((tm, width), lambda i: (i, 0))
    aliases = {}
    if stack is not None:
        aliases = {len(ins): 0}
        ins.append(stack)
        in_specs.append(pl.BlockSpec(memory_space=pl.ANY))
        out_specs = [pl.BlockSpec((1, tm * N_HEADS, LANES), lambda i: (layer, i, 0)), flat_spec]
        out_shape = [jax.ShapeDtypeStruct(stack.shape, stack.dtype),
                     jax.ShapeDtypeStruct((m, width), BF16)]
    else:
        out_specs = [flat_spec]
        out_shape = [jax.ShapeDtypeStruct((m, width), BF16 if mode == "q" else F32)]
    n_in = len(ins)

    def body(*refs):
        if stack is not None:
            refs = refs[:n_in - 1] + refs[n_in:]
        _inproj_kernel(*refs, mode=mode)

    out = pl.pallas_call(
        body,
        grid=(m // tm,),
        in_specs=in_specs,
        out_specs=out_specs,
        out_shape=out_shape,
        scratch_shapes=[pltpu.VMEM((d, width), BF16)],
        input_output_aliases=aliases,
        compiler_params=_params(1),
        name="inproj_" + mode,
    )(*ins)
    return out if stack is not None else out[0]


def _sub_norm(o, g_sub, li):
    y = o * lax.rsqrt(jnp.mean(o * o, axis=-1, keepdims=True) + EPS)
    return (y * g_sub) * (1.0 - li)


def _attn_kernel(lam_ref, q_ref, k_ref, v_ref, g_ref, o_ref, vt_ref, s_ref, m_ref, l_ref, acc_ref,
                 *, tq, li):
    n_t = vt_ref.shape[0]
    for jt in range(n_t):
        vt_ref[jt] = v_ref[0, jt * tq:(jt + 1) * tq, :].astype(F32).T.astype(BF16)
    lo = lax.broadcasted_iota(jnp.int32, (1, LANES), 1) < HEAD_DIM_QK
    pairs = [(i, j) for i in range(n_t) for j in range(i + 1)]

    def logits(n):
        i, j = pairs[n]
        q = q_ref[0, i * tq:(i + 1) * tq, :]
        zero = jnp.zeros_like(q)
        kt = k_ref[0, j * tq:(j + 1) * tq, :]
        for mp, qm in enumerate((jnp.where(lo, q, zero), jnp.where(lo, zero, q))):
            s_ref[n % 2, mp] = lax.dot_general(kt, qm, (((1,), (1,)), ((), ())),
                                               preferred_element_type=F32)

    def consume(n):
        i, j = pairs[n]
        vt = vt_ref[j]
        for mp in range(2):
            s = s_ref[n % 2, mp]
            if j == i:
                key = lax.broadcasted_iota(jnp.int32, (tq, tq), 0)
                qry = lax.broadcasted_iota(jnp.int32, (tq, tq), 1)
                s = jnp.where(key <= qry, s, NEG_INF)
            m_old = m_ref[mp]
            m_new = jnp.maximum(m_old, jnp.max(s, axis=0, keepdims=True))
            alpha = jnp.exp2(m_old - m_new)
            p = jnp.exp2(s - m_new)
            l_ref[mp] = alpha * l_ref[mp] + jnp.sum(p, axis=0, keepdims=True)
            acc_ref[mp] = alpha * acc_ref[mp] + jnp.dot(vt, p.astype(BF16),
                                                        preferred_element_type=F32)
            m_ref[mp] = m_new

    logits(0)
    for n, (i, j) in enumerate(pairs):
        if j == 0:
            m_ref[...] = jnp.full(m_ref.shape, NEG_INF, F32)
            l_ref[...] = jnp.zeros(l_ref.shape, F32)
            acc_ref[...] = jnp.zeros(acc_ref.shape, F32)
        if n + 1 < len(pairs):
            logits(n + 1)
        consume(n)
        if j == i:
            o = acc_ref[0] / l_ref[0] - lam_ref[0] * (acc_ref[1] / l_ref[1])
            y = o * lax.rsqrt(jnp.mean(o * o, axis=0, keepdims=True) + EPS)
            o_ref[0, i * tq:(i + 1) * tq, :] = ((y.T * g_ref[...]) * (1.0 - li)).astype(BF16)


def _attn_prompt(q, k, v, layer, lam, g_sub, batch, seq, tq):
    width = N_HEADS * LANES
    q3, k3, v3 = (t.reshape(batch, seq, width) for t in (q, k, v))
    head_spec = pl.BlockSpec((1, seq, LANES), lambda b, h: (b, 0, h))
    out = pl.pallas_call(
        functools.partial(_attn_kernel, tq=tq, li=_lambda_init(layer)),
        grid=(batch, N_HEADS),
        in_specs=[pl.BlockSpec(memory_space=pltpu.SMEM),
                  head_spec, head_spec, head_spec,
                  pl.BlockSpec((1, LANES), lambda b, h: (0, 0))],
        out_specs=head_spec,
        out_shape=jax.ShapeDtypeStruct((batch, seq, width), BF16),
        scratch_shapes=[pltpu.VMEM((seq // tq, LANES, tq), BF16),
                        pltpu.VMEM((2, 2, tq, tq), F32),
                        pltpu.VMEM((2, 1, tq), F32), pltpu.VMEM((2, 1, tq), F32),
                        pltpu.VMEM((2, LANES, tq), F32)],
        compiler_params=_params(2),
        name="attn_prompt",
    )(lam, q3, k3, v3, g_sub)
    return out.reshape(batch * seq, width)


def _decode_kernel(pt_ref, lam_ref, q_ref, *refs, pages, n_new, li):
    k_refs = refs[:pages]
    v_refs = refs[pages:2 * pages]
    kn_ref, vn_ref, g_ref, o_ref, knp_ref, vnp_ref, m_ref, l_ref, acc_ref = refs[2 * pages:]
    b = pl.program_id(0)
    c = pl.program_id(1)
    last = pl.num_programs(1) - 1
    shift_q = (2 * n_new).bit_length() - 1

    @pl.when(jnp.logical_and(b == 0, c == 0))
    def _():
        knp_ref[...] = jnp.zeros(knp_ref.shape, F32)
        vnp_ref[...] = jnp.zeros(vnp_ref.shape, F32)

    @pl.when(c == 0)
    def _():
        m_ref[...] = jnp.full(m_ref.shape, NEG_INF, F32)
        l_ref[...] = jnp.zeros(l_ref.shape, F32)
        acc_ref[...] = jnp.zeros(acc_ref.shape, F32)

    q = q_ref[0]
    n_q = q.shape[0]
    page_rows = PAGE_SIZE * N_HEADS
    def own_head(n_cols):
        row = lax.broadcasted_iota(jnp.int32, (n_q, n_cols), 0)
        col = lax.broadcasted_iota(jnp.int32, (n_q, n_cols), 1)
        return row, col, jnp.bitwise_and(col, N_HEADS - 1) == lax.shift_right_logical(row, shift_q)

    def logits(k2d, keep):
        s = lax.dot_general(q, k2d.astype(BF16), (((1,), (1,)), ((), ())),
                            preferred_element_type=F32)
        return jnp.where(keep, s, NEG_INF)

    def update(s_list, v_list):
        m_old = m_ref[...]
        m_new = m_old
        for s in s_list:
            m_new = jnp.maximum(m_new, jnp.max(s, axis=-1, keepdims=True))
        alpha = jnp.exp2(m_old - m_new)
        l_new = alpha * l_ref[...]
        pv = None
        for s, v in zip(s_list, v_list):
            p = jnp.exp2(s - m_new)
            l_new = l_new + jnp.sum(p, axis=-1, keepdims=True)
            d = jnp.dot(p.astype(BF16), v, preferred_element_type=F32)
            pv = d if pv is None else pv + d
        acc_ref[...] = alpha * acc_ref[...] + pv
        l_ref[...] = l_new
        m_ref[...] = m_new

    s_list, v_list = [], []
    page_keep = own_head(page_rows)[2]
    for j in range(pages):
        s_list.append(logits(k_refs[j][0, 0].reshape(page_rows, LANES), page_keep))
        v_list.append(v_refs[j][0, 0].reshape(page_rows, LANES).astype(BF16))
    update(s_list, v_list)

    @pl.when(c == last)
    def _():
        n_rows = n_new * N_HEADS
        knp_ref[0:n_rows, :] = kn_ref[0, 0]
        vnp_ref[0:n_rows, :] = vn_ref[0, 0]
        row, col, keep = own_head(knp_ref.shape[0])
        tok = jnp.bitwise_and(row, n_new - 1)
        slot = lax.shift_right_logical(col, N_HEADS.bit_length() - 1)
        keep = jnp.logical_and(keep, slot <= tok)
        update([logits(knp_ref[...], keep)], [vnp_ref[...].astype(BF16)])

        o = acc_ref[...] / l_ref[...]
        o = o - lam_ref[0] * pltpu.roll(o, shift=n_q - n_new, axis=0)
        o_ref[0] = _sub_norm(o, g_ref[...], li)


def _attn_decode(q, k_stack, v_stack, cache_k, cache_v, page_table, layer, lam, g_sub,
                 n_new, pages):
    depth = cache_k.shape[0]
    batch, n_pages = page_table.shape
    assert n_new & (n_new - 1) == 0 and N_HEADS & (N_HEADS - 1) == 0
    n_q = N_HEADS * 2 * n_new
    qh = q.reshape(batch, n_new, N_HEADS, 1, LANES).transpose(0, 2, 3, 1, 4)
    in_map = (jnp.arange(LANES) // HEAD_DIM_QK)[None, :] == jnp.arange(2)[:, None]
    qm = jnp.where(in_map[None, None, :, None, :], qh, jnp.zeros_like(qh)).reshape(batch, n_q, LANES)
    kn = k_stack.reshape(depth, batch, n_new * N_HEADS, LANES)
    vn = v_stack.reshape(depth, batch, n_new * N_HEADS, LANES)

    def page_spec(j):
        return pl.BlockSpec((1, 1, PAGE_SIZE, N_HEADS, LANES),
                            lambda b, c, pt: (layer, pt[b, c * pages + j], 0, 0, 0))

    new_spec = pl.BlockSpec((1, 1, n_new * N_HEADS, LANES), lambda b, c, pt: (layer, b, 0, 0))
    grid_spec = pltpu.PrefetchScalarGridSpec(
        num_scalar_prefetch=1,
        grid=(batch, n_pages // pages),
        in_specs=([pl.BlockSpec(memory_space=pltpu.SMEM),
                   pl.BlockSpec((1, n_q, LANES), lambda b, c, pt: (b, 0, 0))]
                  + [page_spec(j) for j in range(pages)]
                  + [page_spec(j) for j in range(pages)]
                  + [new_spec, new_spec,
                     pl.BlockSpec((1, LANES), lambda b, c, pt: (0, 0))]),
        out_specs=pl.BlockSpec((1, n_q, LANES), lambda b, c, pt: (b, 0, 0)),
        scratch_shapes=[pltpu.VMEM((LANES, LANES), F32),
                        pltpu.VMEM((LANES, LANES), F32),
                        pltpu.VMEM((n_q, 1), F32),
                        pltpu.VMEM((n_q, 1), F32),
                        pltpu.VMEM((n_q, LANES), F32)],
    )
    out = pl.pallas_call(
        functools.partial(_decode_kernel, pages=pages, n_new=n_new, li=_lambda_init(layer)),
        grid_spec=grid_spec,
        out_shape=jax.ShapeDtypeStruct((batch, n_q, LANES), F32),
        compiler_params=_params(2),
        name="attn_decode",
    )(page_table, lam, qm, *([cache_k] * pages), *([cache_v] * pages), kn, vn, g_sub)
    out = out.reshape(batch, N_HEADS, 2, n_new, LANES)[:, :, 0]
    return out.transpose(0, 2, 1, 3).reshape(batch * n_new, N_HEADS * LANES).astype(BF16)


def _pool_kernel(u_ref, past_ref, w_ref, sc_ref, o_ref, full_ref, *, start, chunk):
    halo = past_ref.shape[1]
    t = u_ref.shape[1]
    group = u_ref.shape[2] // len(POOL_WINDOWS)
    full_ref[0:halo, :] = past_ref[0]
    full_ref[halo:halo + t, :] = u_ref[0]
    for c0 in range(0, t, chunk):
        pos = start + c0 + lax.broadcasted_iota(jnp.int32, (chunk, 1), 0)
        for g, w in enumerate(POOL_WINDOWS):
            cols = slice(g * group, (g + 1) * group)
            cur = full_ref[halo + c0:halo + c0 + chunk, cols]
            win = cur
            for i in range(1, w):
                win = win + full_ref[halo + c0 - i:halo + c0 - i + chunk, cols]
            count = jnp.minimum(w, pos + 1).astype(F32)
            pooled = (win / count - cur).astype(BF16)
            y = jnp.dot(pooled, w_ref[0, g].astype(BF16), preferred_element_type=F32)
            o_ref[0, c0:c0 + chunk, cols] = (y * sc_ref[0, :, cols]).astype(BF16)


def _pool(u3, past16, w_pool, pool_scale, layer, start):
    batch, t, c = u3.shape
    depth = w_pool.shape[0]
    group = c // len(POOL_WINDOWS)
    chunk = min(t, 256)
    return pl.pallas_call(
        functools.partial(_pool_kernel, start=start, chunk=chunk),
        grid=(batch,),
        in_specs=[pl.BlockSpec((1, t, c), lambda b: (b, 0, 0)),
                  pl.BlockSpec((1, POOL_STATE + 1, c), lambda b: (b, 0, 0)),
                  pl.BlockSpec((1, len(POOL_WINDOWS), group, group), lambda b: (layer, 0, 0, 0)),
                  pl.BlockSpec((1, 1, c), lambda b: (layer, 0, 0))],
        out_specs=pl.BlockSpec((1, t, c), lambda b: (b, 0, 0)),
        out_shape=jax.ShapeDtypeStruct((batch, t, c), BF16),
        scratch_shapes=[pltpu.VMEM((POOL_STATE + 1 + t, c), F32)],
        compiler_params=_params(1),
        name="pool_mix",
    )(u3, past16, w_pool, pool_scale.reshape(depth, 1, c))


def _outproj_kernel(a_ref, p_ref, x_ref, w_ref, gt_ref, g_ref, sc_ref, sh_ref,
                    x1_ref, h2_ref, h2_rows_ref, wbf_ref):
    @pl.when(pl.program_id(0) == 0)
    def _():
        wbf_ref[...] = w_ref[0].astype(BF16)

    half = a_ref.shape[1]
    a = a_ref[...]
    p = p_ref[...]
    for c0 in range(0, x_ref.shape[1], COL_CHUNK):
        cols = slice(c0, c0 + COL_CHUNK)
        y = jnp.dot(a, wbf_ref[0:half, cols], preferred_element_type=F32)
        y = y + jnp.dot(p, wbf_ref[half:, cols], preferred_element_type=F32)
        x1_ref[:, cols] = x_ref[:, cols] + gt_ref[0, :, cols] * y
    h2 = _norm_mod(x1_ref[...], g_ref[0], sc_ref[0], sh_ref[0])
    h2_ref[...] = h2.astype(BF16)
    if h2_rows_ref is not None:
        tm, chunks = h2.shape[0], h2.shape[1] // LANES
        for c in range(chunks):
            h2_rows_ref[pl.ds(c, tm, stride=chunks), :] = h2[:, c * LANES:(c + 1) * LANES]


def _outproj(a, p, x, w_o, layer, gt, g_ffn, sc, sh, tm, tiles_per_group, row_major):
    m, d = x.shape
    half = a.shape[1]
    out_specs = [pl.BlockSpec((tm, d), lambda i: (i, 0)), pl.BlockSpec((tm, d), lambda i: (i, 0))]
    out_shape = [jax.ShapeDtypeStruct((m, d), F32), jax.ShapeDtypeStruct((m, d), BF16)]
    chunks = d // LANES
    if row_major:
        out_specs.append(pl.BlockSpec((tm * chunks, LANES), lambda i: (i, 0)))
        out_shape.append(jax.ShapeDtypeStruct((m * chunks, LANES), F32))

    def body(*refs):
        if row_major:
            _outproj_kernel(*refs)
        else:
            _outproj_kernel(*refs[:-1], None, refs[-1])

    outs = pl.pallas_call(
        body,
        grid=(m // tm,),
        in_specs=[pl.BlockSpec((tm, half), lambda i: (i, 0)),
                  pl.BlockSpec((tm, half), lambda i: (i, 0)),
                  pl.BlockSpec((tm, d), lambda i: (i, 0)),
                  pl.BlockSpec((1, d, d), lambda i: (layer, 0, 0),
                               pipeline_mode=pl.Buffered(1)),
                  _mod_spec(gt, tiles_per_group),
                  pl.BlockSpec((1, 1, d), lambda i: (layer, 0, 0)),
                  _mod_spec(sc, tiles_per_group), _mod_spec(sh, tiles_per_group)],
        out_specs=out_specs,
        out_shape=out_shape,
        scratch_shapes=[pltpu.VMEM((d, d), BF16)],
        compiler_params=_params(1),
        name="outproj",
    )(a, p, x, w_o, gt, g_ffn, sc, sh)
    if row_major:
        return outs[0], outs[1], outs[2].reshape(m, chunks, LANES)
    return outs


def _swiglu_act(h_ref, wg_ref, wu_ref):
    h = h_ref[...]
    g = jnp.dot(h, wg_ref[0].astype(BF16), preferred_element_type=F32)
    u = jnp.dot(h, wu_ref[0].astype(BF16), preferred_element_type=F32)
    return ((g * jax.nn.sigmoid(g)) * u).astype(BF16)


def _ffn_kernel(h_ref, x_ref, gt_ref, wg_ref, wu_ref, wd_ref, o_ref):
    @pl.when(pl.program_id(1) == 0)
    def _():
        o_ref[...] = x_ref[...]

    act = _swiglu_act(h_ref, wg_ref, wu_ref)
    for c0 in range(0, o_ref.shape[1], COL_CHUNK):
        cols = slice(c0, c0 + COL_CHUNK)
        y = jnp.dot(act, wd_ref[0, :, cols].astype(BF16), preferred_element_type=F32)
        o_ref[:, cols] += gt_ref[0, :, cols] * y


def _ffn(h, x, gt, wg, wu, wd, widx, tm, tf, tiles_per_group):
    m, d = x.shape
    f = wg.shape[2]
    return pl.pallas_call(
        _ffn_kernel,
        grid=(m // tm, f // tf),
        in_specs=[pl.BlockSpec((tm, d), lambda i, j: (i, 0)),
                  pl.BlockSpec((tm, d), lambda i, j: (i, 0), pipeline_mode=pl.Buffered(1)),
                  _mod_spec(gt, tiles_per_group),
                  pl.BlockSpec((1, d, tf), lambda i, j: (widx, 0, j)),
                  pl.BlockSpec((1, d, tf), lambda i, j: (widx, 0, j)),
                  pl.BlockSpec((1, tf, d), lambda i, j: (widx, j, 0))],
        out_specs=pl.BlockSpec((tm, d), lambda i, j: (i, 0)),
        out_shape=jax.ShapeDtypeStruct((m, d), F32),
        compiler_params=_params(2),
        name="ffn",
    )(h, x, gt, wg, wu, wd)


def _all_experts_kernel(h_ref, x_ref, gt_ref, r_ref, wg_ref, wu_ref, wd_ref, o_ref):
    e = pl.program_id(1)

    @pl.when(jnp.logical_and(e == 0, pl.program_id(2) == 0))
    def _():
        o_ref[...] = x_ref[...]

    h = h_ref[...]
    g = jnp.dot(h, wg_ref[0].astype(BF16), preferred_element_type=F32)
    u = jnp.dot(h, wu_ref[0].astype(BF16), preferred_element_type=F32)
    act = ((g * jax.nn.sigmoid(g)) * u).astype(BF16)
    route = r_ref[...]
    lane = lax.broadcasted_iota(jnp.int32, route.shape, 1)
    cw = jnp.sum(jnp.where(lane == e, route, 0.0), axis=-1, keepdims=True)
    for c0 in range(0, o_ref.shape[1], COL_CHUNK):
        cols = slice(c0, c0 + COL_CHUNK)
        y = jnp.dot(act, wd_ref[0, :, cols].astype(BF16), preferred_element_type=F32)
        o_ref[:, cols] += (gt_ref[0, :, cols] * cw) * y


def _all_experts(h, x, gt, route, wg, wu, wd, layer_base, tm, tf, tiles_per_group):
    m, d = x.shape
    f = wg.shape[2]
    return pl.pallas_call(
        _all_experts_kernel,
        grid=(m // tm, N_EXPERTS, f // tf),
        in_specs=[pl.BlockSpec((tm, d), lambda i, e, j: (i, 0)),
                  pl.BlockSpec((tm, d), lambda i, e, j: (i, 0)),
                  _mod_spec(gt, tiles_per_group),
                  pl.BlockSpec((tm, LANES), lambda i, e, j: (i, 0)),
                  pl.BlockSpec((1, d, tf), lambda i, e, j: (layer_base + e, 0, j)),
                  pl.BlockSpec((1, d, tf), lambda i, e, j: (layer_base + e, 0, j)),
                  pl.BlockSpec((1, tf, d), lambda i, e, j: (layer_base + e, j, 0))],
        out_specs=pl.BlockSpec((tm, d), lambda i, e, j: (i, 0)),
        out_shape=jax.ShapeDtypeStruct((m, d), F32),
        compiler_params=_params(3),
        name="moe_all_experts",
    )(h, x, gt, route, wg, wu, wd)


ROUTE_E1, ROUTE_E2, ROUTE_POS1, ROUTE_POS2, ROUTE_G1, ROUTE_G2 = range(N_EXPERTS, N_EXPERTS + 6)


def _router_kernel(h_ref, whi_ref, wlo_ref, b_ref, o_ref, cnt_ref, carry_ref):
    @pl.when(pl.program_id(0) == 0)
    def _():
        carry_ref[...] = jnp.zeros(carry_ref.shape, F32)

    h = h_ref[...].astype(BF16)
    logits = (jnp.dot(h, whi_ref[...], preferred_element_type=F32)
              + jnp.dot(h, wlo_ref[...], preferred_element_type=F32) + b_ref[...])
    lane = lax.broadcasted_iota(jnp.int32, logits.shape, 1).astype(F32)
    logits = jnp.where(lane < N_EXPERTS, logits, NEG_INF)
    v1 = jnp.max(logits, axis=-1, keepdims=True)
    i1 = jnp.min(jnp.where(logits == v1, lane, float(LANES)), axis=-1, keepdims=True)
    rest = jnp.where(lane == i1, NEG_INF, logits)
    v2 = jnp.max(rest, axis=-1, keepdims=True)
    i2 = jnp.min(jnp.where(rest == v2, lane, float(LANES)), axis=-1, keepdims=True)
    e2 = jnp.exp(v2 - v1)
    g1 = 1.0 / (1.0 + e2)
    g2 = e2 / (1.0 + e2)

    tm = h.shape[0]
    sel = jnp.where(jnp.logical_or(lane == i1, lane == i2), 1.0, 0.0)
    earlier = (lax.broadcasted_iota(jnp.int32, (tm, tm), 1)
               < lax.broadcasted_iota(jnp.int32, (tm, tm), 0))
    rank = jnp.dot(jnp.where(earlier, 1.0, 0.0).astype(BF16), sel.astype(BF16),
                   preferred_element_type=F32) + carry_ref[...]
    pos1 = jnp.sum(jnp.where(lane == i1, rank, 0.0), axis=-1, keepdims=True)
    pos2 = jnp.sum(jnp.where(lane == i2, rank, 0.0), axis=-1, keepdims=True)
    total = carry_ref[...] + jnp.sum(sel, axis=0, keepdims=True)
    carry_ref[...] = total
    cnt_ref[...] = total

    out = jnp.where(lane == i1, g1, 0.0) + jnp.where(lane == i2, g2, 0.0)
    for idx, val in ((ROUTE_E1, i1), (ROUTE_E2, i2), (ROUTE_POS1, pos1), (ROUTE_POS2, pos2),
                     (ROUTE_G1, g1), (ROUTE_G2, g2)):
        out = out + jnp.where(lane == float(idx), val, 0.0)
    o_ref[...] = out


def _router(h, router_w, router_b, tm):
    m, d = h.shape
    pad = LANES - N_EXPERTS
    w = jnp.pad(router_w, ((0, 0), (0, pad)))
    whi = w.astype(BF16)
    wlo = (w - whi.astype(F32)).astype(BF16)
    b = jnp.pad(router_b, (0, pad)).reshape(1, LANES)
    return pl.pallas_call(
        _router_kernel,
        grid=(m // tm,),
        in_specs=[pl.BlockSpec((tm, d), lambda i: (i, 0)),
                  pl.BlockSpec((d, LANES), lambda i: (0, 0)),
                  pl.BlockSpec((d, LANES), lambda i: (0, 0)),
                  pl.BlockSpec((1, LANES), lambda i: (0, 0))],
        out_specs=[pl.BlockSpec((tm, LANES), lambda i: (i, 0)),
                   pl.BlockSpec((1, LANES), lambda i: (0, 0))],
        out_shape=[jax.ShapeDtypeStruct((m, LANES), F32),
                   jax.ShapeDtypeStruct((1, LANES), F32)],
        scratch_shapes=[pltpu.VMEM((1, LANES), F32)],
        compiler_params=_params(1),
        name="router",
    )(h, whi, wlo, b)


def _route_plan(route, counts, tm):
    m = route.shape[0]
    n_tiles = 2 * m // tm + N_EXPERTS
    col = lambda i: route[:, i].astype(jnp.int32)
    cnt = counts[0, :N_EXPERTS].astype(jnp.int32)
    tiles = (cnt + tm - 1) // tm
    tile_end = jnp.cumsum(tiles)
    row_start = (tile_end - tiles) * tm
    d1 = row_start[col(ROUTE_E1)] + col(ROUTE_POS1)
    d2 = row_start[col(ROUTE_E2)] + col(ROUTE_POS2)
    token = jnp.arange(m, dtype=jnp.int32)
    src = jnp.zeros((n_tiles * tm,), jnp.int32).at[jnp.concatenate([d1, d2])].set(
        jnp.concatenate([token, token]), unique_indices=True)
    tile_id = jnp.arange(n_tiles, dtype=jnp.int32)
    tile_expert = jnp.minimum(jnp.searchsorted(tile_end, tile_id, side="right"),
                              N_EXPERTS - 1).astype(jnp.int32)
    tile_rows = jnp.clip(cnt[tile_expert] - (tile_id - (tile_end - tiles)[tile_expert]) * tm, 0, tm)
    tile_rows = jnp.where(tile_id < tile_end[-1], tile_rows, 0).astype(jnp.int32)
    return d1, d2, src, tile_expert, tile_rows, tile_end[-1:].astype(jnp.int32)


def _row_copy(src_hbm, row, dst_vmem, slot, sem):
    return pltpu.make_async_copy(src_hbm.at[pl.ds(row, 1), :], dst_vmem.at[pl.ds(slot, 1), :], sem)


def _slab_copy(src_hbm, row, dst_vmem, slot, sem):
    chunks = src_hbm.shape[1]
    return pltpu.make_async_copy(
        src_hbm.at[row], dst_vmem.at[pl.ds(pl.multiple_of(slot * chunks, chunks), chunks), :], sem)


def _gather_kernel(rows_ref, src_ref, h_hbm, o_ref, buf_ref, sems):
    i = pl.program_id(0)
    n_tiles = pl.num_programs(0) - 1
    tm = o_ref.shape[0]
    chunks = h_hbm.shape[1]

    @pl.when(i == 0)
    def _():
        buf_ref[...] = jnp.zeros(buf_ref.shape, F32)

    def copy(slot, r, row):
        return _slab_copy(h_hbm, row, buf_ref.at[slot], r, sems.at[slot])

    def pairs(tile):
        return lax.shift_right_logical(rows_ref[tile] + 1, 1)

    @pl.when(i < n_tiles)
    def _():
        slot = jnp.bitwise_and(i, 1)

        def issue(k, c):
            for prio in range(2):
                r = 2 * k + prio
                copy(slot, r, src_ref[0, 0, r]).start(priority=prio)
            return c

        lax.fori_loop(0, pairs(i), issue, 0)

    @pl.when(i > 0)
    def _():
        slot = jnp.bitwise_and(i - 1, 1)

        def wait(k, c):
            for prio in range(2):
                copy(slot, 2 * k + prio, 0).wait()
            return c

        lax.fori_loop(0, pairs(i - 1), wait, 0)
        for c in range(chunks):
            o_ref[:, c * LANES:(c + 1) * LANES] = (
                buf_ref[slot, pl.ds(c, tm, stride=chunks), :].astype(BF16))


def _gather_rows(h_rows, src, tile_rows, tm):
    _, chunks, _ = h_rows.shape
    n_rows = src.shape[0]
    n_tiles = n_rows // tm
    grid_spec = pltpu.PrefetchScalarGridSpec(
        num_scalar_prefetch=1,
        grid=(n_tiles + 1,),
        in_specs=[pl.BlockSpec((1, 1, tm), lambda i, nr: (jnp.minimum(i, n_tiles - 1), 0, 0),
                               memory_space=pltpu.SMEM),
                  pl.BlockSpec(memory_space=pl.ANY)],
        out_specs=pl.BlockSpec((tm, chunks * LANES), lambda i, nr: (jnp.maximum(i - 1, 0), 0)),
        scratch_shapes=[pltpu.VMEM((2, tm * chunks, LANES), F32), pltpu.SemaphoreType.DMA((2,))],
    )
    return pl.pallas_call(
        _gather_kernel,
        grid_spec=grid_spec,
        out_shape=jax.ShapeDtypeStruct((n_rows, chunks * LANES), BF16),
        compiler_params=_params(1),
        name="moe_gather",
    )(tile_rows, src.reshape(n_rows // tm, 1, tm), h_rows)


def _expert_ffn_kernel(te_ref, nv_ref, h_ref, wg_ref, wu_ref, wd_ref, wgt_ref, wut_ref, wdt_ref,
                       o_ref, *, n_main):
    i = pl.program_id(0)
    j = pl.program_id(1)

    @pl.when(j == 0)
    def _():
        o_ref[...] = jnp.zeros(o_ref.shape, F32)

    def accumulate(wg, wu, wd):
        act = _swiglu_act(h_ref, wg, wu)
        for c0 in range(0, o_ref.shape[1], COL_CHUNK):
            cols = slice(c0, c0 + COL_CHUNK)
            o_ref[:, cols] += jnp.dot(act, wd[0, :, cols].astype(BF16),
                                      preferred_element_type=F32)

    in_use = i < nv_ref[0]

    @pl.when(jnp.logical_and(in_use, j < n_main))
    def _():
        accumulate(wg_ref, wu_ref, wd_ref)

    @pl.when(jnp.logical_and(in_use, j == n_main))
    def _():
        accumulate(wgt_ref, wut_ref, wdt_ref)


def _expert_ffn(xs, wg, wu, wd, layer_base, tile_expert, n_valid, tm):
    n_rows, d = xs.shape
    f = wg.shape[2]
    wide, tail = 2 * LANES, LANES
    n_main = f // wide
    assert f == n_main * wide + tail
    tail_blk = f // tail - 1

    def main_blk(i, j, nv):
        return jnp.where(i < nv[0], jnp.minimum(j, n_main - 1), 0)

    grid_spec = pltpu.PrefetchScalarGridSpec(
        num_scalar_prefetch=2,
        grid=(n_rows // tm, n_main + 1),
        in_specs=[
            pl.BlockSpec((tm, d), lambda i, j, te, nv: (i, 0)),
            pl.BlockSpec((1, d, wide), lambda i, j, te, nv: (layer_base + te[i], 0, main_blk(i, j, nv))),
            pl.BlockSpec((1, d, wide), lambda i, j, te, nv: (layer_base + te[i], 0, main_blk(i, j, nv))),
            pl.BlockSpec((1, wide, d), lambda i, j, te, nv: (layer_base + te[i], main_blk(i, j, nv), 0)),
            pl.BlockSpec((1, d, tail), lambda i, j, te, nv: (layer_base + te[i], 0, tail_blk)),
            pl.BlockSpec((1, d, tail), lambda i, j, te, nv: (layer_base + te[i], 0, tail_blk)),
            pl.BlockSpec((1, tail, d), lambda i, j, te, nv: (layer_base + te[i], tail_blk, 0)),
        ],
        out_specs=pl.BlockSpec((tm, d), lambda i, j, te, nv: (i, 0)),
    )
    return pl.pallas_call(
        functools.partial(_expert_ffn_kernel, n_main=n_main),
        grid_spec=grid_spec,
        out_shape=jax.ShapeDtypeStruct((n_rows, d), F32),
        compiler_params=_params(2),
        name="moe_ffn",
    )(tile_expert, n_valid, xs, wg, wu, wd, wg, wu, wd)


def _combine_kernel(d1_ref, d2_ref, x_ref, r_ref, gt_ref, y_hbm, o_ref, a_ref, b_ref, sems):
    i = pl.program_id(0)
    n_tiles = pl.num_programs(0) - 1
    tc = a_ref.shape[1]

    def copies(slot, r, row1, row2):
        return (_row_copy(y_hbm, row1, a_ref.at[slot], r, sems.at[0, slot]),
                _row_copy(y_hbm, row2, b_ref.at[slot], r, sems.at[1, slot]))

    @pl.when(i < n_tiles)
    def _():
        slot = jnp.bitwise_and(i, 1)

        def issue(r, c):
            for prio, cp in enumerate(copies(slot, r, d1_ref[0, 0, r], d2_ref[0, 0, r])):
                cp.start(priority=prio)
            return c

        lax.fori_loop(0, tc, issue, 0, unroll=4)

    @pl.when(i > 0)
    def _():
        slot = jnp.bitwise_and(i - 1, 1)

        def wait(r, c):
            for cp in copies(slot, r, 0, 0):
                cp.wait()
            return c

        lax.fori_loop(0, tc, wait, 0, unroll=4)
        route = r_ref[...]
        lane = lax.broadcasted_iota(jnp.int32, route.shape, 1)
        g1 = jnp.sum(jnp.where(lane == ROUTE_G1, route, 0.0), axis=-1, keepdims=True)
        g2 = jnp.sum(jnp.where(lane == ROUTE_G2, route, 0.0), axis=-1, keepdims=True)
        o_ref[...] = x_ref[...] + gt_ref[0] * (g1 * a_ref[slot] + g2 * b_ref[slot])


def _combine(x, route, gt, y, d1, d2, tc, tiles_per_group):
    m, d = x.shape
    n_tiles = m // tc
    ahead = lambda i: jnp.minimum(i, n_tiles - 1)
    done = lambda i: jnp.maximum(i - 1, 0)
    idx_spec = pl.BlockSpec((1, 1, tc), lambda i: (ahead(i), 0, 0), memory_space=pltpu.SMEM)
    _, r, _ = gt.shape
    return pl.pallas_call(
        _combine_kernel,
        grid=(n_tiles + 1,),
        in_specs=[idx_spec, idx_spec,
                  pl.BlockSpec((tc, d), lambda i: (done(i), 0)),
                  pl.BlockSpec((tc, LANES), lambda i: (done(i), 0)),
                  pl.BlockSpec((1, r, d), lambda i: (done(i) // tiles_per_group, 0, 0)),
                  pl.BlockSpec(memory_space=pl.ANY)],
        out_specs=pl.BlockSpec((tc, d), lambda i: (done(i), 0)),
        out_shape=jax.ShapeDtypeStruct((m, d), F32),
        scratch_shapes=[pltpu.VMEM((2, tc, d), F32), pltpu.VMEM((2, tc, d), F32),
                        pltpu.SemaphoreType.DMA((2, 2))],
        compiler_params=_params(1),
        name="moe_combine",
    )(d1.reshape(n_tiles, 1, tc), d2.reshape(n_tiles, 1, tc), x, route, gt, y)


def _moe_routed(h2, h2_rows, x, gt, p, j, tm_router, tm, tc, tiles_per_group):
    d = x.shape[1]
    n_exp, _, f_exp = p["moe_w_gate"].shape[1:]
    route, counts = _router(h2, p["router_w"][j], p["router_b"][j], tm_router)
    d1, d2, src, tile_expert, tile_rows, n_valid = _route_plan(route, counts, tm)
    xs = _gather_rows(h2_rows, src, tile_rows, tm)
    y = _expert_ffn(xs, p["moe_w_gate"].reshape(-1, d, f_exp), p["moe_w_up"].reshape(-1, d, f_exp),
                    p["moe_w_down"].reshape(-1, f_exp, d), j * n_exp, tile_expert, n_valid, tm)
    return _combine(x, route, gt, y, d1, d2, tc, tiles_per_group)


def _trunk(x, mods, start, pool_past, attend, p, batch, seq, tm, tm_out, tm_ffn, tm_moe,
           k_stack, v_stack):
    depth = p["w_in"].shape[0]
    m, d = x.shape
    tpg = (m // mods[0][0].shape[0]) // tm
    tpg_out = (m // mods[0][0].shape[0]) // tm_out
    tpg_ffn = (m // mods[0][0].shape[0]) // tm_ffn
    pools = []
    sh_m, sc_m = mods[0][0], mods[0][1]
    h = _norm(x, p["g_norm_mix"], 0, sc_m, sh_m, tm, tpg)
    for l in range(depth):
        sh_m, sc_m, gt_m, sh_f, sc_f, gt_f = mods[l]
        if l > 0:
            h = _norm(x, p["g_norm_mix"], l, sc_m, sh_m, tm, tpg)
        g_q = jnp.tile(p["g_q"][l], 2).reshape(1, LANES)
        g_k = jnp.tile(p["g_k"][l], 2).reshape(1, LANES)
        q = _inproj(h, p["w_in"], l, 0, "q", tm, g128=g_q)
        k_stack, k_bf = _inproj(h, p["w_in"], l, 1, "k", tm, g128=g_k, stack=k_stack)
        v_stack, v_bf = _inproj(h, p["w_in"], l, 2, "v", tm, stack=v_stack)
        u = _inproj(h, p["w_in"], l, 3, "u", tm)
        lam = (jnp.exp(jnp.sum(p["lambda_q1"][l] * p["lambda_k1"][l]))
               - jnp.exp(jnp.sum(p["lambda_q2"][l] * p["lambda_k2"][l]))
               + _lambda_init(l)).astype(F32).reshape(1)
        g_sub = p["g_sub"][l].reshape(1, LANES)
        a = attend(l, q, k_bf, v_bf, k_stack, v_stack, lam, g_sub)
        pool_out, pool_new = pool_past(l, u)
        pools.append(pool_new)
        j = l // 2
        routed = l % 2 == 1 and tm_moe is not None
        x, h2, *h2_rows = _outproj(a, pool_out, x, p["w_o"], l, gt_m, p["g_norm_ffn"], sc_f, sh_f,
                                   tm_out, tpg_out, routed)
        if l % 2 == 0:
            x = _ffn(h2, x, gt_f, p["dense_w_gate"], p["dense_w_up"], p["dense_w_down"], j,
                     tm_ffn, 256, tpg_ffn)
        elif routed:
            x = _moe_routed(h2, h2_rows[0], x, gt_f, p, j, tm, tm_moe, tm_out, tpg_out)
        else:
            route, _ = _router(h2, p["router_w"][j], p["router_b"][j], tm)
            n_exp, _, f_exp = p["moe_w_gate"].shape[1:]
            x = _all_experts(h2, x, gt_f, route, p["moe_w_gate"].reshape(-1, d, f_exp),
                             p["moe_w_up"].reshape(-1, d, f_exp),
                             p["moe_w_down"].reshape(-1, f_exp, d), j * n_exp, tm_ffn, LANES, tpg_ffn)
    return x, k_stack, v_stack, jnp.stack(pools)


def kernel(x_prompt, x_sample, cache_k, cache_v, state_pool, page_table, c_prompt, c_sample,
           g_norm_mix, g_norm_ffn, w_ada, b_ada, w_in, g_q, g_k,
           lambda_q1, lambda_k1, lambda_q2, lambda_k2, g_sub, w_pool, pool_scale, w_o,
           dense_w_gate, dense_w_up, dense_w_down,
           router_w, router_b, moe_w_gate, moe_w_up, moe_w_down):
    p = dict(g_norm_mix=g_norm_mix.reshape(-1, 1, g_norm_mix.shape[-1]),
             g_norm_ffn=g_norm_ffn.reshape(-1, 1, g_norm_ffn.shape[-1]),
             w_in=w_in, g_q=g_q, g_k=g_k,
             lambda_q1=lambda_q1, lambda_k1=lambda_k1, lambda_q2=lambda_q2, lambda_k2=lambda_k2,
             g_sub=g_sub, w_pool=w_pool, pool_scale=pool_scale, w_o=w_o,
             dense_w_gate=dense_w_gate, dense_w_up=dense_w_up, dense_w_down=dense_w_down,
             router_w=router_w, router_b=router_b,
             moe_w_gate=moe_w_gate, moe_w_up=moe_w_up, moe_w_down=moe_w_down)
    depth = w_in.shape[0]
    bp, tp, d = x_prompt.shape
    bs, ts, _ = x_sample.shape
    width = N_HEADS * LANES
    c_pool = pool_scale.shape[-1]

    n_c = bp + bs
    rows = -(-n_c // 16) * 16
    c_all = jnp.concatenate([c_prompt, c_sample, jnp.zeros((rows - n_c, d), F32)], axis=0)
    mod = _ada(c_all, w_ada, b_ada).reshape(depth, rows, 6, d)
    mods_p = [[mod[l, :bp, i][:, None, :] for i in range(6)] for l in range(depth)]
    mods_s = [[jnp.repeat(mod[l, bp:n_c, i], ts, axis=0)[None] for i in range(6)]
              for l in range(depth)]

    zeros_past = jnp.zeros((bp, POOL_STATE + 1, c_pool), F32)

    def attend_p(l, q, k_bf, v_bf, ks, vs, lam, gs):
        return _attn_prompt(q, k_bf, v_bf, l, lam, gs, bp, tp, 512)

    def pool_p(l, u):
        u3 = u.reshape(bp, tp, c_pool)
        out = _pool(u3, zeros_past, w_pool, pool_scale, l, 0)
        return out.reshape(bp * tp, c_pool), u3[:, tp - POOL_STATE:]

    kp0 = jnp.zeros((depth, bp * tp * N_HEADS, LANES), F32)
    vp0 = jnp.zeros((depth, bp * tp * N_HEADS, LANES), F32)
    y_p, k_p, v_p, pool_p_new = _trunk(x_prompt.reshape(bp * tp, d), mods_p, 0, pool_p, attend_p,
                                       p, bp, tp, 1024, 256, 1024, 1024, kp0, vp0)

    past_len = page_table.shape[1] * PAGE_SIZE
    t_pad = 16

    def attend_s(l, q, k_bf, v_bf, ks, vs, lam, gs):
        return _attn_decode(q, ks, vs, cache_k, cache_v, page_table, l, lam, gs, ts, 16)

    def pool_s(l, u):
        u3 = u.reshape(bs, ts, c_pool)
        past16 = jnp.concatenate([jnp.zeros((bs, 1, c_pool), F32), state_pool[l]], axis=1)
        u_pad = jnp.concatenate([u3, jnp.zeros((bs, t_pad - ts, c_pool), F32)], axis=1)
        out = _pool(u_pad, past16, w_pool, pool_scale, l, past_len)
        new = jnp.concatenate([state_pool[l], u3], axis=1)[:, -POOL_STATE:]
        return out[:, :ts].reshape(bs * ts, c_pool), new

    ks0 = jnp.zeros((depth, bs * ts * N_HEADS, LANES), F32)
    vs0 = jnp.zeros((depth, bs * ts * N_HEADS, LANES), F32)
    y_s, k_s, v_s, pool_s_new = _trunk(x_sample.reshape(bs * ts, d), mods_s, past_len, pool_s,
                                       attend_s, p, bs, ts, bs * ts, bs * ts, bs * ts, None,
                                       ks0, vs0)

    return (y_p.reshape(bp, tp, d), y_s.reshape(bs, ts, d),
            k_p.reshape(depth, bp, tp, N_HEADS, LANES), v_p.reshape(depth, bp, tp, N_HEADS, LANES),
            pool_p_new,
            k_s.reshape(depth, bs, ts, N_HEADS, LANES), v_s.reshape(depth, bs, ts, N_HEADS, LANES),
            pool_s_new)
```

```python
import functools
import math

import jax
import jax.numpy as jnp
from jax import lax
from jax.experimental import pallas as pl
from jax.experimental.pallas import tpu as pltpu

F32 = jnp.float32
BF16 = jnp.bfloat16

N_HEADS = 8
HEAD_DIM_V = 128
HEAD_DIM_QK = 64
POOL_WINDOWS = (2, 4, 8, 16)
POOL_STATE = 15
PAGE_SIZE = 128
N_EXPERTS = 8
EPS = 1e-6
NEG_INF = -1e30
LANES = 128
VMEM_LIMIT = 56 * 1024 * 1024
COL_CHUNK = 512
Q_SCALE = HEAD_DIM_QK ** -0.5 * math.log2(math.e)


def _params(n_axes, vmem=VMEM_LIMIT):
    return pltpu.CompilerParams(dimension_semantics=("arbitrary",) * n_axes,
                                vmem_limit_bytes=vmem)


def _lambda_init(layer):
    return 0.8 - 0.6 * math.exp(-0.3 * layer)


def _mod_spec(arr, tiles_per_group):
    _, r, d = arr.shape
    return pl.BlockSpec((1, r, d), lambda m, *_: (m // tiles_per_group, 0, 0))


def _norm_mod(x, g, sc, sh):
    y = x * lax.rsqrt(jnp.mean(x * x, axis=-1, keepdims=True) + EPS)
    return (y * g) * (1.0 + sc) + sh


def _ada_kernel(c_ref, w_ref, b_ref, o_ref):
    c = c_ref[...]
    s = (c * jax.nn.sigmoid(c)).astype(BF16)
    o_ref[0] = jnp.dot(s, w_ref[0].astype(BF16), preferred_element_type=F32) + b_ref[0]


def _ada(c_all, w_ada, b_ada):
    depth, d, n = w_ada.shape
    rows = c_all.shape[0]
    tn = 1024
    return pl.pallas_call(
        _ada_kernel,
        grid=(depth, n // tn),
        in_specs=[pl.BlockSpec((rows, d), lambda l, j: (0, 0)),
                  pl.BlockSpec((1, d, tn), lambda l, j: (l, 0, j)),
                  pl.BlockSpec((1, 1, tn), lambda l, j: (l, 0, j))],
        out_specs=pl.BlockSpec((1, rows, tn), lambda l, j: (l, 0, j)),
        out_shape=jax.ShapeDtypeStruct((depth, rows, n), F32),
        compiler_params=_params(2),
        name="ada_mod",
    )(c_all, w_ada, b_ada.reshape(depth, 1, n))


def _norm_kernel(x_ref, g_ref, sc_ref, sh_ref, h_ref):
    h_ref[...] = _norm_mod(x_ref[...], g_ref[0], sc_ref[0], sh_ref[0]).astype(BF16)


def _norm(x, g_all, layer, sc, sh, tm, tiles_per_group):
    m, d = x.shape
    return pl.pallas_call(
        _norm_kernel,
        grid=(m // tm,),
        in_specs=[pl.BlockSpec((tm, d), lambda i: (i, 0)),
                  pl.BlockSpec((1, 1, d), lambda i: (layer, 0, 0)),
                  _mod_spec(sc, tiles_per_group), _mod_spec(sh, tiles_per_group)],
        out_specs=pl.BlockSpec((tm, d), lambda i: (i, 0)),
        out_shape=jax.ShapeDtypeStruct((m, d), BF16),
        compiler_params=_params(1),
        name="norm_mod",
    )(x, g_all, sc, sh)


def _qk_norm(p, g128, scale):
    lo = lax.broadcasted_iota(jnp.int32, (1, LANES), 1) < HEAD_DIM_QK
    sq = p * p
    s0 = jnp.sum(jnp.where(lo, sq, 0.0), axis=-1, keepdims=True)
    s1 = jnp.sum(jnp.where(lo, 0.0, sq), axis=-1, keepdims=True)
    inv0 = lax.rsqrt(s0 * (1.0 / HEAD_DIM_QK) + EPS)
    inv1 = lax.rsqrt(s1 * (1.0 / HEAD_DIM_QK) + EPS)
    y = (p * jnp.where(lo, inv0, inv1)) * g128
    return y if scale is None else y * scale


def _inproj_kernel(*refs, mode):
    normed = mode in ("q", "k")
    stacked = mode in ("k", "v")
    h_ref, w_ref = refs[:2]
    g_ref = refs[2] if normed else None
    wbf_ref = refs[-1]
    outs = refs[-3:-1] if stacked else refs[-2:-1]

    @pl.when(pl.program_id(0) == 0)
    def _():
        wbf_ref[...] = w_ref[0].astype(BF16)

    h = h_ref[...]
    scale = Q_SCALE if mode == "q" else None
    for c0 in range(0, wbf_ref.shape[1], COL_CHUNK):
        acc = jnp.dot(h, wbf_ref[:, c0:c0 + COL_CHUNK], preferred_element_type=F32)
        if mode == "u":
            outs[0][:, c0:c0 + COL_CHUNK] = acc
            continue
        for hh in range(COL_CHUNK // LANES):
            head = c0 // LANES + hh
            y = acc[:, hh * LANES:(hh + 1) * LANES]
            if normed:
                y = _qk_norm(y, g_ref[...], scale)
            if stacked:
                outs[0][0, pl.ds(head, h.shape[0], stride=N_HEADS), :] = y
                outs[1][:, head * LANES:(head + 1) * LANES] = y.astype(BF16)
            else:
                outs[0][:, head * LANES:(head + 1) * LANES] = y.astype(BF16)


def _inproj(h, w_in, layer, section, mode, tm, g128=None, stack=None):
    m, d = h.shape
    width = N_HEADS * LANES
    ins = [h, w_in]
    in_specs = [pl.BlockSpec((tm, d), lambda i: (i, 0)),
                pl.BlockSpec((1, d, width), lambda i: (layer, 0, section))]
    if g128 is not None:
        ins.append(g128)
        in_specs.append(pl.BlockSpec((1, LANES), lambda i: (0, 0)))
    flat_spec = pl.BlockSpec((tm, width), lambda i: (i, 0))
    aliases = {}
    if stack is not None:
        aliases = {len(ins): 0}
        ins.append(stack)
        in_specs.append(pl.BlockSpec(memory_space=pl.ANY))
        out_specs = [pl.BlockSpec((1, tm * N_HEADS, LANES), lambda i: (layer, i, 0)), flat_spec]
        out_shape = [jax.ShapeDtypeStruct(stack.shape, stack.dtype),
                     jax.ShapeDtypeStruct((m, width), BF16)]
    else:
        out_specs = [flat_spec]
        out_shape = [jax.ShapeDtypeStruct((m, width), BF16 if mode == "q" else F32)]
    n_in = len(ins)

    def body(*refs):
        if stack is not None:
            refs = refs[:n_in - 1] + refs[n_in:]
        _inproj_kernel(*refs, mode=mode)

    out = pl.pallas_call(
        body,
        grid=(m // tm,),
        in_specs=in_specs,
        out_specs=out_specs,
        out_shape=out_shape,
        scratch_shapes=[pltpu.VMEM((d, width), BF16)],
        input_output_aliases=aliases,
        compiler_params=_params(1),
        name="inproj_" + mode,
    )(*ins)
    return out if stack is not None else out[0]


def _sub_norm(o, g_sub, li):
    y = o * lax.rsqrt(jnp.mean(o * o, axis=-1, keepdims=True) + EPS)
    return (y * g_sub) * (1.0 - li)


def _attn_kernel(lam_ref, q_ref, k_ref, v_ref, g_ref, o_ref, vt_ref, s_ref, m_ref, l_ref, acc_ref,
                 *, tq, li):
    n_t = vt_ref.shape[0]
    for jt in range(n_t):
        vt_ref[jt] = v_ref[0, jt * tq:(jt + 1) * tq, :].astype(F32).T.astype(BF16)
    lo = lax.broadcasted_iota(jnp.int32, (1, LANES), 1) < HEAD_DIM_QK
    pairs = [(i, j) for i in range(n_t) for j in range(i + 1)]

    def logits(n):
        i, j = pairs[n]
        q = q_ref[0, i * tq:(i + 1) * tq, :]
        zero = jnp.zeros_like(q)
        kt = k_ref[0, j * tq:(j + 1) * tq, :]
        for mp, qm in enumerate((jnp.where(lo, q, zero), jnp.where(lo, zero, q))):
            s_ref[n % 2, mp] = lax.dot_general(kt, qm, (((1,), (1,)), ((), ())),
                                               preferred_element_type=F32)

    def consume(n):
        i, j = pairs[n]
        vt = vt_ref[j]
        for mp in range(2):
            s = s_ref[n % 2, mp]
            if j == i:
                key = lax.broadcasted_iota(jnp.int32, (tq, tq), 0)
                qry = lax.broadcasted_iota(jnp.int32, (tq, tq), 1)
                s = jnp.where(key <= qry, s, NEG_INF)
            m_old = m_ref[mp]
            m_new = jnp.maximum(m_old, jnp.max(s, axis=0, keepdims=True))
            alpha = jnp.exp2(m_old - m_new)
            p = jnp.exp2(s - m_new)
            l_ref[mp] = alpha * l_ref[mp] + jnp.sum(p, axis=0, keepdims=True)
            acc_ref[mp] = alpha * acc_ref[mp] + jnp.dot(vt, p.astype(BF16),
                                                        preferred_element_type=F32)
            m_ref[mp] = m_new

    logits(0)
    for n, (i, j) in enumerate(pairs):
        if j == 0:
            m_ref[...] = jnp.full(m_ref.shape, NEG_INF, F32)
            l_ref[...] = jnp.zeros(l_ref.shape, F32)
            acc_ref[...] = jnp.zeros(acc_ref.shape, F32)
        if n + 1 < len(pairs):
            logits(n + 1)
        consume(n)
        if j == i:
            o = acc_ref[0] / l_ref[0] - lam_ref[0] * (acc_ref[1] / l_ref[1])
            y = o * lax.rsqrt(jnp.mean(o * o, axis=0, keepdims=True) + EPS)
            o_ref[0, i * tq:(i + 1) * tq, :] = ((y.T * g_ref[...]) * (1.0 - li)).astype(BF16)


def _attn_prompt(q, k, v, layer, lam, g_sub, batch, seq, tq):
    width = N_HEADS * LANES
    q3, k3, v3 = (t.reshape(batch, seq, width) for t in (q, k, v))
    head_spec = pl.BlockSpec((1, seq, LANES), lambda b, h: (b, 0, h))
    out = pl.pallas_call(
        functools.partial(_attn_kernel, tq=tq, li=_lambda_init(layer)),
        grid=(batch, N_HEADS),
        in_specs=[pl.BlockSpec(memory_space=pltpu.SMEM),
                  head_spec, head_spec, head_spec,
                  pl.BlockSpec((1, LANES), lambda b, h: (0, 0))],
        out_specs=head_spec,
        out_shape=jax.ShapeDtypeStruct((batch, seq, width), BF16),
        scratch_shapes=[pltpu.VMEM((seq // tq, LANES, tq), BF16),
                        pltpu.VMEM((2, 2, tq, tq), F32),
                        pltpu.VMEM((2, 1, tq), F32), pltpu.VMEM((2, 1, tq), F32),
                        pltpu.VMEM((2, LANES, tq), F32)],
        compiler_params=_params(2),
        name="attn_prompt",
    )(lam, q3, k3, v3, g_sub)
    return out.reshape(batch * seq, width)


def _decode_kernel(pt_ref, lam_ref, q_ref, *refs, pages, n_new, li):
    k_refs = refs[:pages]
    v_refs = refs[pages:2 * pages]
    kn_ref, vn_ref, g_ref, o_ref, knp_ref, vnp_ref, m_ref, l_ref, acc_ref = refs[2 * pages:]
    b = pl.program_id(0)
    c = pl.program_id(1)
    last = pl.num_programs(1) - 1
    shift_q = (2 * n_new).bit_length() - 1

    @pl.when(jnp.logical_and(b == 0, c == 0))
    def _():
        knp_ref[...] = jnp.zeros(knp_ref.shape, F32)
        vnp_ref[...] = jnp.zeros(vnp_ref.shape, F32)

    @pl.when(c == 0)
    def _():
        m_ref[...] = jnp.full(m_ref.shape, NEG_INF, F32)
        l_ref[...] = jnp.zeros(l_ref.shape, F32)
        acc_ref[...] = jnp.zeros(acc_ref.shape, F32)

    q = q_ref[0]
    n_q = q.shape[0]
    page_rows = PAGE_SIZE * N_HEADS
    def own_head(n_cols):
        row = lax.broadcasted_iota(jnp.int32, (n_q, n_cols), 0)
        col = lax.broadcasted_iota(jnp.int32, (n_q, n_cols), 1)
        return row, col, jnp.bitwise_and(col, N_HEADS - 1) == lax.shift_right_logical(row, shift_q)

    def logits(k2d, keep):
        s = lax.dot_general(q, k2d.astype(BF16), (((1,), (1,)), ((), ())),
                            preferred_element_type=F32)
        return jnp.where(keep, s, NEG_INF)

    def update(s_list, v_list):
        m_old = m_ref[...]
        m_new = m_old
        for s in s_list:
            m_new = jnp.maximum(m_new, jnp.max(s, axis=-1, keepdims=True))
        alpha = jnp.exp2(m_old - m_new)
        l_new = alpha * l_ref[...]
        pv = None
        for s, v in zip(s_list, v_list):
            p = jnp.exp2(s - m_new)
            l_new = l_new + jnp.sum(p, axis=-1, keepdims=True)
            d = jnp.dot(p.astype(BF16), v, preferred_element_type=F32)
            pv = d if pv is None else pv + d
        acc_ref[...] = alpha * acc_ref[...] + pv
        l_ref[...] = l_new
        m_ref[...] = m_new

    s_list, v_list = [], []
    page_keep = own_head(page_rows)[2]
    for j in range(pages):
        s_list.append(logits(k_refs[j][0, 0].reshape(page_rows, LANES), page_keep))
        v_list.append(v_refs[j][0, 0].reshape(page_rows, LANES).astype(BF16))
    update(s_list, v_list)

    @pl.when(c == last)
    def _():
        n_rows = n_new * N_HEADS
        knp_ref[0:n_rows, :] = kn_ref[0, 0]
        vnp_ref[0:n_rows, :] = vn_ref[0, 0]
        row, col, keep = own_head(knp_ref.shape[0])
        tok = jnp.bitwise_and(row, n_new - 1)
        slot = lax.shift_right_logical(col, N_HEADS.bit_length() - 1)
        keep = jnp.logical_and(keep, slot <= tok)
        update([logits(knp_ref[...], keep)], [vnp_ref[...].astype(BF16)])

        o = acc_ref[...] / l_ref[...]
        o = o - lam_ref[0] * pltpu.roll(o, shift=n_q - n_new, axis=0)
        o_ref[0] = _sub_norm(o, g_ref[...], li)


def _attn_decode(q, k_stack, v_stack, cache_k, cache_v, page_table, layer, lam, g_sub,
                 n_new, pages):
    depth = cache_k.shape[0]
    batch, n_pages = page_table.shape
    assert n_new & (n_new - 1) == 0 and N_HEADS & (N_HEADS - 1) == 0
    n_q = N_HEADS * 2 * n_new
    qh = q.reshape(batch, n_new, N_HEADS, 1, LANES).transpose(0, 2, 3, 1, 4)
    in_map = (jnp.arange(LANES) // HEAD_DIM_QK)[None, :] == jnp.arange(2)[:, None]
    qm = jnp.where(in_map[None, None, :, None, :], qh, jnp.zeros_like(qh)).reshape(batch, n_q, LANES)
    kn = k_stack.reshape(depth, batch, n_new * N_HEADS, LANES)
    vn = v_stack.reshape(depth, batch, n_new * N_HEADS, LANES)

    def page_spec(j):
        return pl.BlockSpec((1, 1, PAGE_SIZE, N_HEADS, LANES),
                            lambda b, c, pt: (layer, pt[b, c * pages + j], 0, 0, 0))

    new_spec = pl.BlockSpec((1, 1, n_new * N_HEADS, LANES), lambda b, c, pt: (layer, b, 0, 0))
    grid_spec = pltpu.PrefetchScalarGridSpec(
        num_scalar_prefetch=1,
        grid=(batch, n_pages // pages),
        in_specs=([pl.BlockSpec(memory_space=pltpu.SMEM),
                   pl.BlockSpec((1, n_q, LANES), lambda b, c, pt: (b, 0, 0))]
                  + [page_spec(j) for j in range(pages)]
                  + [page_spec(j) for j in range(pages)]
                  + [new_spec, new_spec,
                     pl.BlockSpec((1, LANES), lambda b, c, pt: (0, 0))]),
        out_specs=pl.BlockSpec((1, n_q, LANES), lambda b, c, pt: (b, 0, 0)),
        scratch_shapes=[pltpu.VMEM((LANES, LANES), F32),
                        pltpu.VMEM((LANES, LANES), F32),
                        pltpu.VMEM((n_q, 1), F32),
                        pltpu.VMEM((n_q, 1), F32),
                        pltpu.VMEM((n_q, LANES), F32)],
    )
    out = pl.pallas_call(
        functools.partial(_decode_kernel, pages=pages, n_new=n_new, li=_lambda_init(layer)),
        grid_spec=grid_spec,
        out_shape=jax.ShapeDtypeStruct((batch, n_q, LANES), F32),
        compiler_params=_params(2),
        name="attn_decode",
    )(page_table, lam, qm, *([cache_k] * pages), *([cache_v] * pages), kn, vn, g_sub)
    out = out.reshape(batch, N_HEADS, 2, n_new, LANES)[:, :, 0]
    return out.transpose(0, 2, 1, 3).reshape(batch * n_new, N_HEADS * LANES).astype(BF16)


def _pool_kernel(u_ref, past_ref, w_ref, sc_ref, o_ref, full_ref, *, start, chunk):
    halo = past_ref.shape[1]
    t = u_ref.shape[1]
    group = u_ref.shape[2] // len(POOL_WINDOWS)
    full_ref[0:halo, :] = past_ref[0]
    full_ref[halo:halo + t, :] = u_ref[0]
    for c0 in range(0, t, chunk):
        pos = start + c0 + lax.broadcasted_iota(jnp.int32, (chunk, 1), 0)
        for g, w in enumerate(POOL_WINDOWS):
            cols = slice(g * group, (g + 1) * group)
            cur = full_ref[halo + c0:halo + c0 + chunk, cols]
            win = cur
            for i in range(1, w):
                win = win + full_ref[halo + c0 - i:halo + c0 - i + chunk, cols]
            count = jnp.minimum(w, pos + 1).astype(F32)
            pooled = (win / count - cur).astype(BF16)
            y = jnp.dot(pooled, w_ref[0, g].astype(BF16), preferred_element_type=F32)
            o_ref[0, c0:c0 + chunk, cols] = (y * sc_ref[0, :, cols]).astype(BF16)


def _pool(u3, past16, w_pool, pool_scale, layer, start):
    batch, t, c = u3.shape
    depth = w_pool.shape[0]
    group = c // len(POOL_WINDOWS)
    chunk = min(t, 256)
    return pl.pallas_call(
        functools.partial(_pool_kernel, start=start, chunk=chunk),
        grid=(batch,),
        in_specs=[pl.BlockSpec((1, t, c), lambda b: (b, 0, 0)),
                  pl.BlockSpec((1, POOL_STATE + 1, c), lambda b: (b, 0, 0)),
                  pl.BlockSpec((1, len(POOL_WINDOWS), group, group), lambda b: (layer, 0, 0, 0)),
                  pl.BlockSpec((1, 1, c), lambda b: (layer, 0, 0))],
        out_specs=pl.BlockSpec((1, t, c), lambda b: (b, 0, 0)),
        out_shape=jax.ShapeDtypeStruct((batch, t, c), BF16),
        scratch_shapes=[pltpu.VMEM((POOL_STATE + 1 + t, c), F32)],
        compiler_params=_params(1),
        name="pool_mix",
    )(u3, past16, w_pool, pool_scale.reshape(depth, 1, c))


def _outproj_kernel(a_ref, p_ref, x_ref, w_ref, gt_ref, g_ref, sc_ref, sh_ref,
                    x1_ref, h2_ref, h2_rows_ref, wbf_ref):
    @pl.when(pl.program_id(0) == 0)
    def _():
        wbf_ref[...] = w_ref[0].astype(BF16)

    half = a_ref.shape[1]
    a = a_ref[...]
    p = p_ref[...]
    for c0 in range(0, x_ref.shape[1], COL_CHUNK):
        cols = slice(c0, c0 + COL_CHUNK)
        y = jnp.dot(a, wbf_ref[0:half, cols], preferred_element_type=F32)
        y = y + jnp.dot(p, wbf_ref[half:, cols], preferred_element_type=F32)
        x1_ref[:, cols] = x_ref[:, cols] + gt_ref[0, :, cols] * y
    h2 = _norm_mod(x1_ref[...], g_ref[0], sc_ref[0], sh_ref[0])
    h2_ref[...] = h2.astype(BF16)
    if h2_rows_ref is not None:
        tm, chunks = h2.shape[0], h2.shape[1] // LANES
        for c in range(chunks):
            h2_rows_ref[pl.ds(c, tm, stride=chunks), :] = h2[:, c * LANES:(c + 1) * LANES]


def _outproj(a, p, x, w_o, layer, gt, g_ffn, sc, sh, tm, tiles_per_group, row_major):
    m, d = x.shape
    half = a.shape[1]
    out_specs = [pl.BlockSpec((tm, d), lambda i: (i, 0)), pl.BlockSpec((tm, d), lambda i: (i, 0))]
    out_shape = [jax.ShapeDtypeStruct((m, d), F32), jax.ShapeDtypeStruct((m, d), BF16)]
    chunks = d // LANES
    if row_major:
        out_specs.append(pl.BlockSpec((tm * chunks, LANES), lambda i: (i, 0)))
        out_shape.append(jax.ShapeDtypeStruct((m * chunks, LANES), F32))

    def body(*refs):
        if row_major:
            _outproj_kernel(*refs)
        else:
            _outproj_kernel(*refs[:-1], None, refs[-1])

    outs = pl.pallas_call(
        body,
        grid=(m // tm,),
        in_specs=[pl.BlockSpec((tm, half), lambda i: (i, 0)),
                  pl.BlockSpec((tm, half), lambda i: (i, 0)),
                  pl.BlockSpec((tm, d), lambda i: (i, 0)),
                  pl.BlockSpec((1, d, d), lambda i: (layer, 0, 0),
                               pipeline_mode=pl.Buffered(1)),
                  _mod_spec(gt, tiles_per_group),
                  pl.BlockSpec((1, 1, d), lambda i: (layer, 0, 0)),
                  _mod_spec(sc, tiles_per_group), _mod_spec(sh, tiles_per_group)],
        out_specs=out_specs,
        out_shape=out_shape,
        scratch_shapes=[pltpu.VMEM((d, d), BF16)],
        compiler_params=_params(1),
        name="outproj",
    )(a, p, x, w_o, gt, g_ffn, sc, sh)
    if row_major:
        return outs[0], outs[1], outs[2].reshape(m, chunks, LANES)
    return outs


def _swiglu_act(h_ref, wg_ref, wu_ref):
    h = h_ref[...]
    g = jnp.dot(h, wg_ref[0].astype(BF16), preferred_element_type=F32)
    u = jnp.dot(h, wu_ref[0].astype(BF16), preferred_element_type=F32)
    return ((g * jax.nn.sigmoid(g)) * u).astype(BF16)


def _ffn_kernel(h_ref, x_ref, gt_ref, wg_ref, wu_ref, wd_ref, o_ref):
    @pl.when(pl.program_id(1) == 0)
    def _():
        o_ref[...] = x_ref[...]

    act = _swiglu_act(h_ref, wg_ref, wu_ref)
    for c0 in range(0, o_ref.shape[1], COL_CHUNK):
        cols = slice(c0, c0 + COL_CHUNK)
        y = jnp.dot(act, wd_ref[0, :, cols].astype(BF16), preferred_element_type=F32)
        o_ref[:, cols] += gt_ref[0, :, cols] * y


def _ffn(h, x, gt, wg, wu, wd, widx, tm, tf, tiles_per_group):
    m, d = x.shape
    f = wg.shape[2]
    return pl.pallas_call(
        _ffn_kernel,
        grid=(m // tm, f // tf),
        in_specs=[pl.BlockSpec((tm, d), lambda i, j: (i, 0)),
                  pl.BlockSpec((tm, d), lambda i, j: (i, 0), pipeline_mode=pl.Buffered(1)),
                  _mod_spec(gt, tiles_per_group),
                  pl.BlockSpec((1, d, tf), lambda i, j: (widx, 0, j)),
                  pl.BlockSpec((1, d, tf), lambda i, j: (widx, 0, j)),
                  pl.BlockSpec((1, tf, d), lambda i, j: (widx, j, 0))],
        out_specs=pl.BlockSpec((tm, d), lambda i, j: (i, 0)),
        out_shape=jax.ShapeDtypeStruct((m, d), F32),
        compiler_params=_params(2),
        name="ffn",
    )(h, x, gt, wg, wu, wd)


def _all_experts_kernel(h_ref, x_ref, gt_ref, r_ref, wg_ref, wu_ref, wd_ref, o_ref):
    e = pl.program_id(1)

    @pl.when(jnp.logical_and(e == 0, pl.program_id(2) == 0))
    def _():
        o_ref[...] = x_ref[...]

    h = h_ref[...]
    g = jnp.dot(h, wg_ref[0].astype(BF16), preferred_element_type=F32)
    u = jnp.dot(h, wu_ref[0].astype(BF16), preferred_element_type=F32)
    act = ((g * jax.nn.sigmoid(g)) * u).astype(BF16)
    route = r_ref[...]
    lane = lax.broadcasted_iota(jnp.int32, route.shape, 1)
    cw = jnp.sum(jnp.where(lane == e, route, 0.0), axis=-1, keepdims=True)
    for c0 in range(0, o_ref.shape[1], COL_CHUNK):
        cols = slice(c0, c0 + COL_CHUNK)
        y = jnp.dot(act, wd_ref[0, :, cols].astype(BF16), preferred_element_type=F32)
        o_ref[:, cols] += (gt_ref[0, :, cols] * cw) * y


def _all_experts(h, x, gt, route, wg, wu, wd, layer_base, tm, tf, tiles_per_group):
    m, d = x.shape
    f = wg.shape[2]
    return pl.pallas_call(
        _all_experts_kernel,
        grid=(m // tm, N_EXPERTS, f // tf),
        in_specs=[pl.BlockSpec((tm, d), lambda i, e, j: (i, 0)),
                  pl.BlockSpec((tm, d), lambda i, e, j: (i, 0)),
                  _mod_spec(gt, tiles_per_group),
                  pl.BlockSpec((tm, LANES), lambda i, e, j: (i, 0)),
                  pl.BlockSpec((1, d, tf), lambda i, e, j: (layer_base + e, 0, j)),
                  pl.BlockSpec((1, d, tf), lambda i, e, j: (layer_base + e, 0, j)),
                  pl.BlockSpec((1, tf, d), lambda i, e, j: (layer_base + e, j, 0))],
        out_specs=pl.BlockSpec((tm, d), lambda i, e, j: (i, 0)),
        out_shape=jax.ShapeDtypeStruct((m, d), F32),
        compiler_params=_params(3),
        name="moe_all_experts",
    )(h, x, gt, route, wg, wu, wd)


ROUTE_E1, ROUTE_E2, ROUTE_POS1, ROUTE_POS2, ROUTE_G1, ROUTE_G2 = range(N_EXPERTS, N_EXPERTS + 6)


def _router_kernel(h_ref, whi_ref, wlo_ref, b_ref, o_ref, cnt_ref, carry_ref):
    @pl.when(pl.program_id(0) == 0)
    def _():
        carry_ref[...] = jnp.zeros(carry_ref.shape, F32)

    h = h_ref[...].astype(BF16)
    logits = (jnp.dot(h, whi_ref[...], preferred_element_type=F32)
              + jnp.dot(h, wlo_ref[...], preferred_element_type=F32) + b_ref[...])
    lane = lax.broadcasted_iota(jnp.int32, logits.shape, 1).astype(F32)
    logits = jnp.where(lane < N_EXPERTS, logits, NEG_INF)
    v1 = jnp.max(logits, axis=-1, keepdims=True)
    i1 = jnp.min(jnp.where(logits == v1, lane, float(LANES)), axis=-1, keepdims=True)
    rest = jnp.where(lane == i1, NEG_INF, logits)
    v2 = jnp.max(rest, axis=-1, keepdims=True)
    i2 = jnp.min(jnp.where(rest == v2, lane, float(LANES)), axis=-1, keepdims=True)
    e2 = jnp.exp(v2 - v1)
    g1 = 1.0 / (1.0 + e2)
    g2 = e2 / (1.0 + e2)

    tm = h.shape[0]
    sel = jnp.where(jnp.logical_or(lane == i1, lane == i2), 1.0, 0.0)
    earlier = (lax.broadcasted_iota(jnp.int32, (tm, tm), 1)
               < lax.broadcasted_iota(jnp.int32, (tm, tm), 0))
    rank = jnp.dot(jnp.where(earlier, 1.0, 0.0).astype(BF16), sel.astype(BF16),
                   preferred_element_type=F32) + carry_ref[...]
    pos1 = jnp.sum(jnp.where(lane == i1, rank, 0.0), axis=-1, keepdims=True)
    pos2 = jnp.sum(jnp.where(lane == i2, rank, 0.0), axis=-1, keepdims=True)
    total = carry_ref[...] + jnp.sum(sel, axis=0, keepdims=True)
    carry_ref[...] = total
    cnt_ref[...] = total

    out = jnp.where(lane == i1, g1, 0.0) + jnp.where(lane == i2, g2, 0.0)
    for idx, val in ((ROUTE_E1, i1), (ROUTE_E2, i2), (ROUTE_POS1, pos1), (ROUTE_POS2, pos2),
                     (ROUTE_G1, g1), (ROUTE_G2, g2)):
        out = out + jnp.where(lane == float(idx), val, 0.0)
    o_ref[...] = out


def _router(h, router_w, router_b, tm):
    m, d = h.shape
    pad = LANES - N_EXPERTS
    w = jnp.pad(router_w, ((0, 0), (0, pad)))
    whi = w.astype(BF16)
    wlo = (w - whi.astype(F32)).astype(BF16)
    b = jnp.pad(router_b, (0, pad)).reshape(1, LANES)
    return pl.pallas_call(
        _router_kernel,
        grid=(m // tm,),
        in_specs=[pl.BlockSpec((tm, d), lambda i: (i, 0)),
                  pl.BlockSpec((d, LANES), lambda i: (0, 0)),
                  pl.BlockSpec((d, LANES), lambda i: (0, 0)),
                  pl.BlockSpec((1, LANES), lambda i: (0, 0))],
        out_specs=[pl.BlockSpec((tm, LANES), lambda i: (i, 0)),
                   pl.BlockSpec((1, LANES), lambda i: (0, 0))],
        out_shape=[jax.ShapeDtypeStruct((m, LANES), F32),
                   jax.ShapeDtypeStruct((1, LANES), F32)],
        scratch_shapes=[pltpu.VMEM((1, LANES), F32)],
        compiler_params=_params(1),
        name="router",
    )(h, whi, wlo, b)


def _route_plan(route, counts, tm):
    m = route.shape[0]
    n_tiles = 2 * m // tm + N_EXPERTS
    col = lambda i: route[:, i].astype(jnp.int32)
    cnt = counts[0, :N_EXPERTS].astype(jnp.int32)
    tiles = (cnt + tm - 1) // tm
    tile_end = jnp.cumsum(tiles)
    row_start = (tile_end - tiles) * tm
    d1 = row_start[col(ROUTE_E1)] + col(ROUTE_POS1)
    d2 = row_start[col(ROUTE_E2)] + col(ROUTE_POS2)
    token = jnp.arange(m, dtype=jnp.int32)
    src = jnp.zeros((n_tiles * tm,), jnp.int32).at[jnp.concatenate([d1, d2])].set(
        jnp.concatenate([token, token]), unique_indices=True)
    tile_id = jnp.arange(n_tiles, dtype=jnp.int32)
    tile_expert = jnp.minimum(jnp.searchsorted(tile_end, tile_id, side="right"),
                              N_EXPERTS - 1).astype(jnp.int32)
    tile_rows = jnp.clip(cnt[tile_expert] - (tile_id - (tile_end - tiles)[tile_expert]) * tm, 0, tm)
    tile_rows = jnp.where(tile_id < tile_end[-1], tile_rows, 0).astype(jnp.int32)
    return d1, d2, src, tile_expert, tile_rows, tile_end[-1:].astype(jnp.int32)


def _row_copy(src_hbm, row, dst_vmem, slot, sem):
    return pltpu.make_async_copy(src_hbm.at[pl.ds(row, 1), :], dst_vmem.at[pl.ds(slot, 1), :], sem)


def _slab_copy(src_hbm, row, dst_vmem, slot, sem):
    chunks = src_hbm.shape[1]
    return pltpu.make_async_copy(
        src_hbm.at[row], dst_vmem.at[pl.ds(pl.multiple_of(slot * chunks, chunks), chunks), :], sem)


def _gather_kernel(rows_ref, src_ref, h_hbm, o_ref, buf_ref, sems):
    i = pl.program_id(0)
    n_tiles = pl.num_programs(0) - 1
    tm = o_ref.shape[0]
    chunks = h_hbm.shape[1]

    @pl.when(i == 0)
    def _():
        buf_ref[...] = jnp.zeros(buf_ref.shape, F32)

    def copy(slot, r, row):
        return _slab_copy(h_hbm, row, buf_ref.at[slot], r, sems.at[slot])

    def pairs(tile):
        return lax.shift_right_logical(rows_ref[tile] + 1, 1)

    @pl.when(i < n_tiles)
    def _():
        slot = jnp.bitwise_and(i, 1)

        def issue(k, c):
            for prio in range(2):
                r = 2 * k + prio
                copy(slot, r, src_ref[0, 0, r]).start(priority=prio)
            return c

        lax.fori_loop(0, pairs(i), issue, 0)

    @pl.when(i > 0)
    def _():
        slot = jnp.bitwise_and(i - 1, 1)

        def wait(k, c):
            for prio in range(2):
                copy(slot, 2 * k + prio, 0).wait()
            return c

        lax.fori_loop(0, pairs(i - 1), wait, 0)
        for c in range(chunks):
            o_ref[:, c * LANES:(c + 1) * LANES] = (
                buf_ref[slot, pl.ds(c, tm, stride=chunks), :].astype(BF16))


def _gather_rows(h_rows, src, tile_rows, tm):
    _, chunks, _ = h_rows.shape
    n_rows = src.shape[0]
    n_tiles = n_rows // tm
    grid_spec = pltpu.PrefetchScalarGridSpec(
        num_scalar_prefetch=1,
        grid=(n_tiles + 1,),
        in_specs=[pl.BlockSpec((1, 1, tm), lambda i, nr: (jnp.minimum(i, n_tiles - 1), 0, 0),
                               memory_space=pltpu.SMEM),
                  pl.BlockSpec(memory_space=pl.ANY)],
        out_specs=pl.BlockSpec((tm, chunks * LANES), lambda i, nr: (jnp.maximum(i - 1, 0), 0)),
        scratch_shapes=[pltpu.VMEM((2, tm * chunks, LANES), F32), pltpu.SemaphoreType.DMA((2,))],
    )
    return pl.pallas_call(
        _gather_kernel,
        grid_spec=grid_spec,
        out_shape=jax.ShapeDtypeStruct((n_rows, chunks * LANES), BF16),
        compiler_params=_params(1),
        name="moe_gather",
    )(tile_rows, src.reshape(n_rows // tm, 1, tm), h_rows)


def _expert_ffn_kernel(te_ref, nv_ref, h_ref, wg_ref, wu_ref, wd_ref, wgt_ref, wut_ref, wdt_ref,
                       o_ref, *, n_main):
    i = pl.program_id(0)
    j = pl.program_id(1)

    @pl.when(j == 0)
    def _():
        o_ref[...] = jnp.zeros(o_ref.shape, F32)

    def accumulate(wg, wu, wd):
        act = _swiglu_act(h_ref, wg, wu)
        for c0 in range(0, o_ref.shape[1], COL_CHUNK):
            cols = slice(c0, c0 + COL_CHUNK)
            o_ref[:, cols] += jnp.dot(act, wd[0, :, cols].astype(BF16),
                                      preferred_element_type=F32)

    in_use = i < nv_ref[0]

    @pl.when(jnp.logical_and(in_use, j < n_main))
    def _():
        accumulate(wg_ref, wu_ref, wd_ref)

    @pl.when(jnp.logical_and(in_use, j == n_main))
    def _():
        accumulate(wgt_ref, wut_ref, wdt_ref)


def _expert_ffn(xs, wg, wu, wd, layer_base, tile_expert, n_valid, tm):
    n_rows, d = xs.shape
    f = wg.shape[2]
    wide, tail = 2 * LANES, LANES
    n_main = f // wide
    assert f == n_main * wide + tail
    tail_blk = f // tail - 1

    def main_blk(i, j, nv):
        return jnp.where(i < nv[0], jnp.minimum(j, n_main - 1), 0)

    grid_spec = pltpu.PrefetchScalarGridSpec(
        num_scalar_prefetch=2,
        grid=(n_rows // tm, n_main + 1),
        in_specs=[
            pl.BlockSpec((tm, d), lambda i, j, te, nv: (i, 0)),
            pl.BlockSpec((1, d, wide), lambda i, j, te, nv: (layer_base + te[i], 0, main_blk(i, j, nv))),
            pl.BlockSpec((1, d, wide), lambda i, j, te, nv: (layer_base + te[i], 0, main_blk(i, j, nv))),
            pl.BlockSpec((1, wide, d), lambda i, j, te, nv: (layer_base + te[i], main_blk(i, j, nv), 0)),
            pl.BlockSpec((1, d, tail), lambda i, j, te, nv: (layer_base + te[i], 0, tail_blk)),
            pl.BlockSpec((1, d, tail), lambda i, j, te, nv: (layer_base + te[i], 0, tail_blk)),
            pl.BlockSpec((1, tail, d), lambda i, j, te, nv: (layer_base + te[i], tail_blk, 0)),
        ],
        out_specs=pl.BlockSpec((tm, d), lambda i, j, te, nv: (i, 0)),
    )
    return pl.pallas_call(
        functools.partial(_expert_ffn_kernel, n_main=n_main),
        grid_spec=grid_spec,
        out_shape=jax.ShapeDtypeStruct((n_rows, d), F32),
        compiler_params=_params(2),
        name="moe_ffn",
    )(tile_expert, n_valid, xs, wg, wu, wd, wg, wu, wd)


def _combine_kernel(d1_ref, d2_ref, x_ref, r_ref, gt_ref, y_hbm, o_ref, a_ref, b_ref, sems):
    i = pl.program_id(0)
    n_tiles = pl.num_programs(0) - 1
    tc = a_ref.shape[1]

    def copies(slot, r, row1, row2):
        return (_row_copy(y_hbm, row1, a_ref.at[slot], r, sems.at[0, slot]),
                _row_copy(y_hbm, row2, b_ref.at[slot], r, sems.at[1, slot]))

    @pl.when(i < n_tiles)
    def _():
        slot = jnp.bitwise_and(i, 1)

        def issue(r, c):
            for prio, cp in enumerate(copies(slot, r, d1_ref[0, 0, r], d2_ref[0, 0, r])):
                cp.start(priority=prio)
            return c

        lax.fori_loop(0, tc, issue, 0, unroll=4)

    @pl.when(i > 0)
    def _():
        slot = jnp.bitwise_and(i - 1, 1)

        def wait(r, c):
            for cp in copies(slot, r, 0, 0):
                cp.wait()
            return c

        lax.fori_loop(0, tc, wait, 0, unroll=4)
        route = r_ref[...]
        lane = lax.broadcasted_iota(jnp.int32, route.shape, 1)
        g1 = jnp.sum(jnp.where(lane == ROUTE_G1, route, 0.0), axis=-1, keepdims=True)
        g2 = jnp.sum(jnp.where(lane == ROUTE_G2, route, 0.0), axis=-1, keepdims=True)
        o_ref[...] = x_ref[...] + gt_ref[0] * (g1 * a_ref[slot] + g2 * b_ref[slot])


def _combine(x, route, gt, y, d1, d2, tc, tiles_per_group):
    m, d = x.shape
    n_tiles = m // tc
    ahead = lambda i: jnp.minimum(i, n_tiles - 1)
    done = lambda i: jnp.maximum(i - 1, 0)
    idx_spec = pl.BlockSpec((1, 1, tc), lambda i: (ahead(i), 0, 0), memory_space=pltpu.SMEM)
    _, r, _ = gt.shape
    return pl.pallas_call(
        _combine_kernel,
        grid=(n_tiles + 1,),
        in_specs=[idx_spec, idx_spec,
                  pl.BlockSpec((tc, d), lambda i: (done(i), 0)),
                  pl.BlockSpec((tc, LANES), lambda i: (done(i), 0)),
                  pl.BlockSpec((1, r, d), lambda i: (done(i) // tiles_per_group, 0, 0)),
                  pl.BlockSpec(memory_space=pl.ANY)],
        out_specs=pl.BlockSpec((tc, d), lambda i: (done(i), 0)),
        out_shape=jax.ShapeDtypeStruct((m, d), F32),
        scratch_shapes=[pltpu.VMEM((2, tc, d), F32), pltpu.VMEM((2, tc, d), F32),
                        pltpu.SemaphoreType.DMA((2, 2))],
        compiler_params=_params(1),
        name="moe_combine",
    )(d1.reshape(n_tiles, 1, tc), d2.reshape(n_tiles, 1, tc), x, route, gt, y)


def _moe_routed(h2, h2_rows, x, gt, p, j, tm_router, tm, tc, tiles_per_group):
    d = x.shape[1]
    n_exp, _, f_exp = p["moe_w_gate"].shape[1:]
    route, counts = _router(h2, p["router_w"][j], p["router_b"][j], tm_router)
    d1, d2, src, tile_expert, tile_rows, n_valid = _route_plan(route, counts, tm)
    xs = _gather_rows(h2_rows, src, tile_rows, tm)
    y = _expert_ffn(xs, p["moe_w_gate"].reshape(-1, d, f_exp), p["moe_w_up"].reshape(-1, d, f_exp),
                    p["moe_w_down"].reshape(-1, f_exp, d), j * n_exp, tile_expert, n_valid, tm)
    return _combine(x, route, gt, y, d1, d2, tc, tiles_per_group)


def _trunk(x, mods, start, pool_past, attend, p, batch, seq, tm, tm_out, tm_ffn, tm_moe,
           k_stack, v_stack):
    depth = p["w_in"].shape[0]
    m, d = x.shape
    tpg = (m // mods[0][0].shape[0]) // tm
    tpg_out = (m // mods[0][0].shape[0]) // tm_out
    tpg_ffn = (m // mods[0][0].shape[0]) // tm_ffn
    pools = []
    sh_m, sc_m = mods[0][0], mods[0][1]
    h = _norm(x, p["g_norm_mix"], 0, sc_m, sh_m, tm, tpg)
    for l in range(depth):
        sh_m, sc_m, gt_m, sh_f, sc_f, gt_f = mods[l]
        if l > 0:
            h = _norm(x, p["g_norm_mix"], l, sc_m, sh_m, tm, tpg)
        g_q = jnp.tile(p["g_q"][l], 2).reshape(1, LANES)
        g_k = jnp.tile(p["g_k"][l], 2).reshape(1, LANES)
        q = _inproj(h, p["w_in"], l, 0, "q", tm, g128=g_q)
        k_stack, k_bf = _inproj(h, p["w_in"], l, 1, "k", tm, g128=g_k, stack=k_stack)
        v_stack, v_bf = _inproj(h, p["w_in"], l, 2, "v", tm, stack=v_stack)
        u = _inproj(h, p["w_in"], l, 3, "u", tm)
        lam = (jnp.exp(jnp.sum(p["lambda_q1"][l] * p["lambda_k1"][l]))
               - jnp.exp(jnp.sum(p["lambda_q2"][l] * p["lambda_k2"][l]))
               + _lambda_init(l)).astype(F32).reshape(1)
        g_sub = p["g_sub"][l].reshape(1, LANES)
        a = attend(l, q, k_bf, v_bf, k_stack, v_stack, lam, g_sub)
        pool_out, pool_new = pool_past(l, u)
        pools.append(pool_new)
        j = l // 2
        routed = l % 2 == 1 and tm_moe is not None
        x, h2, *h2_rows = _outproj(a, pool_out, x, p["w_o"], l, gt_m, p["g_norm_ffn"], sc_f, sh_f,
                                   tm_out, tpg_out, routed)
        if l % 2 == 0:
            x = _ffn(h2, x, gt_f, p["dense_w_gate"], p["dense_w_up"], p["dense_w_down"], j,
                     tm_ffn, 256, tpg_ffn)
        elif routed:
            x = _moe_routed(h2, h2_rows[0], x, gt_f, p, j, tm, tm_moe, tm_out, tpg_out)
        else:
            route, _ = _router(h2, p["router_w"][j], p["router_b"][j], tm)
            n_exp, _, f_exp = p["moe_w_gate"].shape[1:]
            x = _all_experts(h2, x, gt_f, route, p["moe_w_gate"].reshape(-1, d, f_exp),
                             p["moe_w_up"].reshape(-1, d, f_exp),
                             p["moe_w_down"].reshape(-1, f_exp, d), j * n_exp, tm_ffn, LANES, tpg_ffn)
    return x, k_stack, v_stack, jnp.stack(pools)


def kernel(x_prompt, x_sample, cache_k, cache_v, state_pool, page_table, c_prompt, c_sample,
           g_norm_mix, g_norm_ffn, w_ada, b_ada, w_in, g_q, g_k,
           lambda_q1, lambda_k1, lambda_q2, lambda_k2, g_sub, w_pool, pool_scale, w_o,
           dense_w_gate, dense_w_up, dense_w_down,
           router_w, router_b, moe_w_gate, moe_w_up, moe_w_down):
    p = dict(g_norm_mix=g_norm_mix.reshape(-1, 1, g_norm_mix.shape[-1]),
             g_norm_ffn=g_norm_ffn.reshape(-1, 1, g_norm_ffn.shape[-1]),
             w_in=w_in, g_q=g_q, g_k=g_k,
             lambda_q1=lambda_q1, lambda_k1=lambda_k1, lambda_q2=lambda_q2, lambda_k2=lambda_k2,
             g_sub=g_sub, w_pool=w_pool, pool_scale=pool_scale, w_o=w_o,
             dense_w_gate=dense_w_gate, dense_w_up=dense_w_up, dense_w_down=dense_w_down,
             router_w=router_w, router_b=router_b,
             moe_w_gate=moe_w_gate, moe_w_up=moe_w_up, moe_w_down=moe_w_down)
    depth = w_in.shape[0]
    bp, tp, d = x_prompt.shape
    bs, ts, _ = x_sample.shape
    width = N_HEADS * LANES
    c_pool = pool_scale.shape[-1]

    n_c = bp + bs
    rows = -(-n_c // 16) * 16
    c_all = jnp.concatenate([c_prompt, c_sample, jnp.zeros((rows - n_c, d), F32)], axis=0)
    mod = _ada(c_all, w_ada, b_ada).reshape(depth, rows, 6, d)
    mods_p = [[mod[l, :bp, i][:, None, :] for i in range(6)] for l in range(depth)]
    mods_s = [[jnp.repeat(mod[l, bp:n_c, i], ts, axis=0)[None] for i in range(6)]
              for l in range(depth)]

    zeros_past = jnp.zeros((bp, POOL_STATE + 1, c_pool), F32)

    def attend_p(l, q, k_bf, v_bf, ks, vs, lam, gs):
        return _attn_prompt(q, k_bf, v_bf, l, lam, gs, bp, tp, 512)

    def pool_p(l, u):
        u3 = u.reshape(bp, tp, c_pool)
        out = _pool(u3, zeros_past, w_pool, pool_scale, l, 0)
        return out.reshape(bp * tp, c_pool), u3[:, tp - POOL_STATE:]

    kp0 = jnp.zeros((depth, bp * tp * N_HEADS, LANES), F32)
    vp0 = jnp.zeros((depth, bp * tp * N_HEADS, LANES), F32)
    y_p, k_p, v_p, pool_p_new = _trunk(x_prompt.reshape(bp * tp, d), mods_p, 0, pool_p, attend_p,
                                       p, bp, tp, 1024, 256, 1024, 512, kp0, vp0)

    past_len = page_table.shape[1] * PAGE_SIZE
    t_pad = 16

    def attend_s(l, q, k_bf, v_bf, ks, vs, lam, gs):
        return _attn_decode(q, ks, vs, cache_k, cache_v, page_table, l, lam, gs, ts, 16)

    def pool_s(l, u):
        u3 = u.reshape(bs, ts, c_pool)
        past16 = jnp.concatenate([jnp.zeros((bs, 1, c_pool), F32), state_pool[l]], axis=1)
        u_pad = jnp.concatenate([u3, jnp.zeros((bs, t_pad - ts, c_pool), F32)], axis=1)
        out = _pool(u_pad, past16, w_pool, pool_scale, l, past_len)
        new = jnp.concatenate([state_pool[l], u3], axis=1)[:, -POOL_STATE:]
        return out[:, :ts].reshape(bs * ts, c_pool), new

    ks0 = jnp.zeros((depth, bs * ts * N_HEADS, LANES), F32)
    vs0 = jnp.zeros((depth, bs * ts * N_HEADS, LANES), F32)
    y_s, k_s, v_s, pool_s_new = _trunk(x_sample.reshape(bs * ts, d), mods_s, past_len, pool_s,
                                       attend_s, p, bs, ts, bs * ts, bs * ts, bs * ts, None,
                                       ks0, vs0)

    return (y_p.reshape(bp, tp, d), y_s.reshape(bs, ts, d),
            k_p.reshape(depth, bp, tp, N_HEADS, LANES), v_p.reshape(depth, bp, tp, N_HEADS, LANES),
            pool_p_new,
            k_s.reshape(depth, bs, ts, N_HEADS, LANES), v_s.reshape(depth, bs, ts, N_HEADS, LANES),
            pool_s_new)
```
